```python
import jax, jax.numpy as jnp
from jax import lax
import numpy as np

D_MODEL = 4096
BATCH = 1
SEQ = 8192
DEPTH = 1
DEC_BATCH = 32
DEC_SEQ = 8
PAST_LEN = 8192
PAGE_SIZE = 128

F32 = jnp.float32
HEAD_DIM = 128
MIX_WIDTH = D_MODEL
HGRN_WIDTH = MIX_WIDTH // 2
NSA_WIDTH = MIX_WIDTH - HGRN_WIDTH
HGRN_HEADS = HGRN_WIDTH // HEAD_DIM
HGRN_DK = HGRN_WIDTH // HGRN_HEADS
HGRN_DV = HGRN_WIDTH // HGRN_HEADS
HGRN_CHUNK = 64
NSA_HEADS = NSA_WIDTH // HEAD_DIM
NSA_KV_HEADS = 4
NSA_GROUP = NSA_HEADS // NSA_KV_HEADS
KV_WIDTH = NSA_KV_HEADS * HEAD_DIM
CMP_LEN = 32
CMP_STRIDE = 16
CMP_HIDDEN = 2 * HEAD_DIM
SEL_BLOCK = 64
SEL_TOP = 16
WINDOW = 512
Q_BLOCK = 128
NORM_EPS = 1e-6
COL_SIZES = (HGRN_WIDTH,) * 4 + (NSA_WIDTH,) + (KV_WIDTH,) * 6 + (3 * NSA_HEADS, NSA_WIDTH)
N_IN = 4 * HGRN_WIDTH + 2 * NSA_WIDTH + 6 * KV_WIDTH + 3 * NSA_HEADS

kernel_name = 'hymba_hgrn2_nsa_adaln_decode_step'


def _rms(x, g, eps=NORM_EPS):
    xf = x.astype(F32)
    return xf * lax.rsqrt(jnp.mean(xf * xf, axis=-1, keepdims=True) + eps) * g.astype(F32)


def _masked_softmax(s, mask):
    p = jax.nn.softmax(jnp.where(mask, s, -1e30), axis=-1)
    return jnp.where(mask, p, 0.0)


def _project(x, c, norm_g, w_ada, b_ada, w_in):
    mod = jax.nn.silu(c) @ w_ada + b_ada
    shift, scale, gate = jnp.split(mod[:, None, :], 3, axis=-1)
    h = (_rms(x, norm_g) * (1.0 + scale) + shift).astype(x.dtype)
    proj = h @ w_in
    cuts = np.cumsum(np.array(COL_SIZES))[:-1].tolist()
    return jnp.split(proj, cuts, axis=-1), gate


def _hgrn2_recurrence(q, logf, k, v, s0):
    B, T, H, DK = q.shape
    DV = v.shape[-1]
    C = min(HGRN_CHUNK, T)
    n_chunk = -(-T // C)
    pad = n_chunk * C - T

    def chunks(a):
        a = jnp.pad(a, ((0, 0), (0, pad), (0, 0), (0, 0)))
        return a.reshape(B, n_chunk, C, H, a.shape[-1]).swapaxes(0, 1)

    tri = jnp.tril(jnp.ones((C, C), bool))[None, :, :, None, None]

    def step(S, inp):
        qc, lfc, kc, vc = inp
        cum = jnp.cumsum(lfc, axis=1)
        o = jnp.einsum('bthk,bhkv->bthv', qc * jnp.exp(cum), S)
        decay = jnp.exp(jnp.where(tri, cum[:, :, None] - cum[:, None], -jnp.inf))
        a = jnp.sum(qc[:, :, None] * kc[:, None] * decay, axis=-1)
        o = o + jnp.einsum('btsh,bshv->bthv', a, vc)
        last = cum[:, -1]
        S = jnp.exp(last)[..., None] * S + jnp.einsum(
            'bshk,bshv->bhkv', kc * jnp.exp(last[:, None] - cum), vc)
        return S, o

    S, o = lax.scan(step, s0, (chunks(q), chunks(logf), chunks(k), chunks(v)))
    o = o.swapaxes(0, 1).reshape(B, n_chunk * C, H, DV)[:, :T]
    return o, S


def _hgrn2(hq, hf, hi, hz, s0, lb, out_g):
    B, T, _ = hq.shape
    shp = (B, T, HGRN_HEADS, HGRN_DK)
    f = lb + (1.0 - lb) * jax.nn.sigmoid(hf.astype(F32))
    logf = jnp.log(f).reshape(shp)
    k = (1.0 - f).reshape(shp)
    q = hq.astype(F32).reshape(shp)
    v = hi.astype(F32).reshape(B, T, HGRN_HEADS, HGRN_DV)
    o, S = _hgrn2_recurrence(q, logf, k, v, s0.astype(F32))
    o = _rms(o, out_g).reshape(B, T, HGRN_WIDTH) * jax.nn.silu(hz.astype(F32))
    return o, S


def _nsa_prepare(nq, kcr, vcr, ks, vs, kw, vw, ng, q_g, k_g):
    B, T, _ = nq.shape
    kvs = (B, T, NSA_KV_HEADS, HEAD_DIM)
    q = _rms(nq.reshape(B, T, NSA_HEADS, HEAD_DIM), q_g)
    rows = jnp.stack([kcr.reshape(kvs).astype(F32), vcr.reshape(kvs).astype(F32),
                      _rms(ks.reshape(kvs), k_g[1]), vs.reshape(kvs).astype(F32)], axis=2)
    win = jnp.stack([_rms(kw.reshape(kvs), k_g[2]), vw.reshape(kvs).astype(F32)], axis=2)
    gates = jax.nn.sigmoid(ng.astype(F32)).reshape(B, T, NSA_HEADS, 3)
    return q, rows, win, gates


def _compress(rows, pe, w1, w2):
    B, L, G, D = rows.shape
    r = CMP_LEN // CMP_STRIDE
    n_sub = L // CMP_STRIDE
    n_cmp = n_sub - r + 1
    sub = rows[:, :n_sub * CMP_STRIDE].reshape(B, n_sub, CMP_STRIDE, G, D).astype(F32)
    w1f = w1.astype(F32)
    w1s = w1f.reshape(r, CMP_STRIDE, D, CMP_HIDDEN)
    pre = jnp.einsum('pd,pdh->h', pe.astype(F32), w1f)
    for m in range(r):
        pre = pre + jnp.einsum('bnpgd,pdh->bngh', sub, w1s[m])[:, m:m + n_cmp]
    return jnp.einsum('bngh,hd->bngd', jax.nn.silu(pre), w2.astype(F32))


def _nsa_key_views(rows, pe, w1, w2, kc_g):
    B, L = rows.shape[:2]
    kc = _rms(_compress(rows[:, :, 0], pe[0], w1[0], w2[0]), kc_g)
    vc = _compress(rows[:, :, 1], pe[1], w1[1], w2[1])
    pad = (-L) % SEL_BLOCK
    sel = rows[:, :, 2:]
    if pad:
        sel = jnp.pad(sel, ((0, 0), (0, pad), (0, 0), (0, 0), (0, 0)))
    sel = sel.reshape(B, (L + pad) // SEL_BLOCK, SEL_BLOCK, 2, NSA_KV_HEADS, HEAD_DIM)
    return kc, vc, sel[:, :, :, 0], sel[:, :, :, 1]


def _nsa_core(q, q_pos, kc, vc, ksb, vsb, kw, vw, kw_pos, gates):
    B, Tq = q.shape[:2]
    n_cmp, n_sel = kc.shape[1], ksb.shape[1]
    qg = q.reshape(B, Tq, NSA_KV_HEADS, NSA_GROUP, HEAD_DIM) * (HEAD_DIM ** -0.5)
    cmp_lo = jnp.arange(n_cmp) * CMP_STRIDE
    m_c = ((cmp_lo + CMP_LEN - 1)[None, :] <= q_pos[:, None])[None, :, None, None, :]
    p_c = _masked_softmax(jnp.einsum('btgjd,bngd->btgjn', qg, kc), m_c)
    o_c = jnp.einsum('btgjn,bngd->btgjd', p_c, vc)
    sel_lo = jnp.arange(n_sel) * SEL_BLOCK
    overlap = ((cmp_lo[:, None] < sel_lo[None, :] + SEL_BLOCK)
               & (cmp_lo[:, None] + CMP_LEN > sel_lo[None, :])).astype(F32)
    imp = jnp.einsum('btgjn,ns->btgs', p_c, overlap)
    blk = jnp.arange(n_sel)[None, :]
    cur = (q_pos // SEL_BLOCK)[:, None]
    forced = (blk == 0) | (blk == cur) | (blk == cur - 1)
    causal = sel_lo[None, :] <= q_pos[:, None]
    score = jnp.where(causal[None, :, None], jnp.where(forced[None, :, None], jnp.inf, imp), -jnp.inf)
    _, idx = lax.top_k(score, min(SEL_TOP, n_sel))
    idx = idx.transpose(0, 2, 1, 3)
    nt = idx.shape[-1]
    take = jax.vmap(jax.vmap(lambda blocks, i: blocks[i]))
    ks = take(ksb.transpose(0, 3, 1, 2, 4), idx).reshape(B, NSA_KV_HEADS, Tq, nt * SEL_BLOCK, HEAD_DIM)
    vs = take(vsb.transpose(0, 3, 1, 2, 4), idx).reshape(B, NSA_KV_HEADS, Tq, nt * SEL_BLOCK, HEAD_DIM)
    key_pos = (idx[..., None] * SEL_BLOCK + jnp.arange(SEL_BLOCK)).reshape(B, NSA_KV_HEADS, Tq, nt * SEL_BLOCK)
    m_s = (key_pos <= q_pos[None, None, :, None])[:, :, :, None, :]
    p_s = _masked_softmax(jnp.einsum('btgjd,bgtkd->bgtjk', qg, ks), m_s)
    o_s = jnp.einsum('bgtjk,bgtkd->btgjd', p_s, vs)
    m_w = ((kw_pos[None, :] <= q_pos[:, None]) & (kw_pos[None, :] >= q_pos[:, None] - WINDOW)
           & (kw_pos[None, :] >= 0))[None, :, None, None, :]
    p_w = _masked_softmax(jnp.einsum('btgjd,blgd->btgjl', qg, kw), m_w)
    o_w = jnp.einsum('btgjl,blgd->btgjd', p_w, vw)
    g = gates.reshape(B, Tq, NSA_KV_HEADS, NSA_GROUP, 3, 1)
    o = g[..., 0, :] * o_c + g[..., 1, :] * o_s + g[..., 2, :] * o_w
    return o.reshape(B, Tq, NSA_WIDTH)


def _merge(x, gate, o_h, o_n, nz, nsa_out_g, w_out):
    o_n = _rms(o_n, nsa_out_g) * jax.nn.silu(nz.astype(F32))
    o = jnp.concatenate([o_h, o_n], axis=-1).astype(x.dtype)
    return x + gate * (o @ w_out)


def _prompt_layer(x, c, lb, norm_g, w_ada, b_ada, w_in, hgrn_out_g, q_g, k_g, pe, w1, w2, nsa_out_g, w_out):
    B, T, _ = x.shape
    parts, gate = _project(x, c, norm_g, w_ada, b_ada, w_in)
    hq, hf, hi, hz, nq, kcr, vcr, ks, vs, kw, vw, ng, nz = parts
    s0 = jnp.zeros((B, HGRN_HEADS, HGRN_DK, HGRN_DV), F32)
    o_h, s_new = _hgrn2(hq, hf, hi, hz, s0, lb, hgrn_out_g)
    q, rows, win, gates = _nsa_prepare(nq, kcr, vcr, ks, vs, kw, vw, ng, q_g, k_g)
    kc, vc, ksb, vsb = _nsa_key_views(rows, pe, w1, w2, k_g[0])
    win_pad = jnp.pad(win, ((0, 0), (WINDOW, 0), (0, 0), (0, 0), (0, 0)))

    def block(start):
        qb = lax.dynamic_slice_in_dim(q, start, Q_BLOCK, axis=1)
        gb = lax.dynamic_slice_in_dim(gates, start, Q_BLOCK, axis=1)
        wb = lax.dynamic_slice_in_dim(win_pad, start, WINDOW + Q_BLOCK, axis=1)
        q_pos = start + jnp.arange(Q_BLOCK)
        kw_pos = start - WINDOW + jnp.arange(WINDOW + Q_BLOCK)
        return _nsa_core(qb, q_pos, kc, vc, ksb, vsb, wb[:, :, 0], wb[:, :, 1], kw_pos, gb)

    o_n = lax.map(block, jnp.arange(T // Q_BLOCK) * Q_BLOCK)
    o_n = o_n.swapaxes(0, 1).reshape(B, T, NSA_WIDTH)
    y = _merge(x, gate, o_h, o_n, nz, nsa_out_g, w_out)
    return y, rows, win[:, -min(WINDOW, T):], s_new


def _sample_layer(x, c, cache, s_win, s_hgrn, page_table, lb, norm_g, w_ada, b_ada, w_in, hgrn_out_g,
                  q_g, k_g, pe, w1, w2, nsa_out_g, w_out):
    B, T, _ = x.shape
    parts, gate = _project(x, c, norm_g, w_ada, b_ada, w_in)
    hq, hf, hi, hz, nq, kcr, vcr, ks, vs, kw, vw, ng, nz = parts
    o_h, s_new = _hgrn2(hq, hf, hi, hz, s_hgrn, lb, hgrn_out_g)
    q, rows, win, gates = _nsa_prepare(nq, kcr, vcr, ks, vs, kw, vw, ng, q_g, k_g)
    past = cache[page_table]
    past = past.reshape(B, past.shape[1] * past.shape[2], 4, NSA_KV_HEADS, HEAD_DIM).astype(F32)
    past_len = past.shape[1]
    kc, vc, ksb, vsb = _nsa_key_views(jnp.concatenate([past, rows], axis=1), pe, w1, w2, k_g[0])
    n_buf = s_win.shape[1]
    w_full = jnp.concatenate([s_win.astype(F32), win], axis=1)
    q_pos = past_len + jnp.arange(T)
    kw_pos = past_len - n_buf + jnp.arange(n_buf + T)
    o_n = _nsa_core(q, q_pos, kc, vc, ksb, vsb, w_full[:, :, 0], w_full[:, :, 1], kw_pos, gates)
    y = _merge(x, gate, o_h, o_n, nz, nsa_out_g, w_out)
    return y, rows, w_full[:, -n_buf:], s_new


def setup_inputs(seed: int = 0) -> dict:
    key = jax.random.key(seed)
    ks = jax.random.split(key, 22)
    n_pages = PAST_LEN // PAGE_SIZE
    n_used = DEC_BATCH * n_pages
    n_pool = n_used + max(1, n_used // 4)
    win_buf = min(WINDOW, PAST_LEN)

    def nrm(k, shape, s=1.0):
        return s * jax.random.normal(k, shape, F32)

    page_table = jax.random.permutation(ks[5], n_pool)[:n_used].reshape(DEC_BATCH, n_pages).astype(jnp.int32)
    return {
        'x_prompt': nrm(ks[0], (BATCH, SEQ, D_MODEL)),
        'x_sample': nrm(ks[1], (DEC_BATCH, DEC_SEQ, D_MODEL)),
        'cache_kv': nrm(ks[2], (DEPTH, n_pool, PAGE_SIZE, 4, NSA_KV_HEADS, HEAD_DIM)),
        'state_win': nrm(ks[3], (DEPTH, DEC_BATCH, win_buf, 2, NSA_KV_HEADS, HEAD_DIM)),
        'state_hgrn': nrm(ks[4], (DEPTH, DEC_BATCH, HGRN_HEADS, HGRN_DK, HGRN_DV), 0.5),
        'page_table': page_table,
        'c_prompt': nrm(ks[6], (BATCH, D_MODEL)),
        'c_sample': nrm(ks[7], (DEC_BATCH, D_MODEL)),
        'norm_g': 1.0 + nrm(ks[8], (DEPTH, D_MODEL), 0.02),
        'w_ada': nrm(ks[9], (DEPTH, D_MODEL, 3 * D_MODEL), 0.5 * D_MODEL ** -0.5),
        'b_ada': nrm(ks[10], (DEPTH, 3 * D_MODEL), 0.01),
        'w_in': nrm(ks[11], (DEPTH, D_MODEL, N_IN), D_MODEL ** -0.5),
        'hgrn_lb': nrm(ks[12], (DEPTH + 1, HGRN_WIDTH), 0.5),
        'hgrn_out_g': 1.0 + nrm(ks[13], (DEPTH, HGRN_DV), 0.02),
        'q_norm_g': 1.0 + nrm(ks[14], (DEPTH, HEAD_DIM), 0.02),
        'k_norm_g': 1.0 + nrm(ks[15], (DEPTH, 3, HEAD_DIM), 0.02),
        'cmp_pe': nrm(ks[16], (DEPTH, 2, CMP_LEN, HEAD_DIM), 0.1),
        'cmp_w1': nrm(ks[17], (DEPTH, 2, CMP_LEN, HEAD_DIM, CMP_HIDDEN), (CMP_LEN * HEAD_DIM) ** -0.5),
        'cmp_w2': nrm(ks[18], (DEPTH, 2, CMP_HIDDEN, HEAD_DIM), CMP_HIDDEN ** -0.5),
        'nsa_out_g': 1.0 + nrm(ks[19], (DEPTH, NSA_WIDTH), 0.02),
        'w_out': nrm(ks[20], (DEPTH, MIX_WIDTH, D_MODEL), MIX_WIDTH ** -0.5),
    }


def reference(x_prompt, x_sample, cache_kv, state_win, state_hgrn, page_table, c_prompt, c_sample,
              norm_g, w_ada, b_ada, w_in, hgrn_lb, hgrn_out_g, q_norm_g, k_norm_g,
              cmp_pe, cmp_w1, cmp_w2, nsa_out_g, w_out):
    lbs = jnp.cumsum(jax.nn.softmax(hgrn_lb.astype(F32), axis=0), axis=0)
    xp, xs = x_prompt, x_sample
    kv_p, kv_s, win_p, win_s, st_p, st_s = [], [], [], [], [], []
    for l in range(DEPTH):
        xp, rows_p, wp, sp = _prompt_layer(
            xp, c_prompt, lbs[l], norm_g[l], w_ada[l], b_ada[l], w_in[l], hgrn_out_g[l],
            q_norm_g[l], k_norm_g[l], cmp_pe[l], cmp_w1[l], cmp_w2[l], nsa_out_g[l], w_out[l])
        xs, rows_s, ws, ss = _sample_layer(
            xs, c_sample, cache_kv[l], state_win[l], state_hgrn[l], page_table, lbs[l],
            norm_g[l], w_ada[l], b_ada[l], w_in[l], hgrn_out_g[l], q_norm_g[l], k_norm_g[l],
            cmp_pe[l], cmp_w1[l], cmp_w2[l], nsa_out_g[l], w_out[l])
        kv_p.append(rows_p); kv_s.append(rows_s)
        win_p.append(wp); win_s.append(ws)
        st_p.append(sp); st_s.append(ss)
    return (xp, xs, jnp.stack(kv_p), jnp.stack(kv_s), jnp.stack(win_p), jnp.stack(win_s),
            jnp.stack(st_p), jnp.stack(st_s))
```

```python
import functools

import jax
import jax.numpy as jnp
from jax import lax
from jax.experimental import pallas as pl
from jax.experimental.pallas import tpu as pltpu

F32 = jnp.float32
BF16 = jnp.bfloat16

D_MODEL = 4096
HEAD_DIM = 128
HGRN_WIDTH = 2048
NSA_WIDTH = 2048
HGRN_HEADS = 16
NSA_HEADS = 16
NSA_KV_HEADS = 4
NSA_GROUP = 4
KV_WIDTH = 512
CMP_LEN = 32
CMP_STRIDE = 16
CMP_HIDDEN = 256
SEL_BLOCK = 64
SEL_TOP = 16
WINDOW = 512
Q_BLOCK = 128
NORM_EPS = 1e-6
PAGE = 128

COL_HQ, COL_HF, COL_HI, COL_HZ, COL_NQ, COL_ROWS, COL_NZ = (i * 2048 for i in range(7))
COL_WIN = 14336
COL_NG = 15360
N_PROJ = 15872
SAMPLE_ROWS = 256
ROW_TILE = 256
PAGES_PER_TILE = 8
KV_TILE = 512
NEG = -1e30
VMEM_LIMIT = 56 * 1024 * 1024


def _cparams(sem):
    return pltpu.CompilerParams(dimension_semantics=sem, vmem_limit_bytes=VMEM_LIMIT)


def _dot(a, b):
    return jnp.dot(a.astype(BF16), b.astype(BF16), preferred_element_type=F32)


def _dot_nt(a, b):
    return lax.dot_general(a.astype(BF16), b.astype(BF16), (((1,), (1,)), ((), ())),
                           preferred_element_type=F32)


def _dot_tn(a, b):
    return lax.dot_general(a.astype(BF16), b.astype(BF16), (((0,), (0,)), ((), ())),
                           preferred_element_type=F32)


def _silu(x):
    return x * jax.nn.sigmoid(x)


def _ada_kernel(c_ref, w_ref, b_ref, o_ref):
    c = c_ref[...]
    o_ref[...] = _dot(_silu(c), w_ref[...]) + b_ref[...]


def _ada(c_all, w_ada, b_ada):
    m, d = c_all.shape
    n = w_ada.shape[1]
    tn = 512
    return pl.pallas_call(
        _ada_kernel,
        grid=(n // tn,),
        in_specs=[pl.BlockSpec((m, d), lambda j: (0, 0)),
                  pl.BlockSpec((d, tn), lambda j: (0, j)),
                  pl.BlockSpec((1, tn), lambda j: (0, j))],
        out_specs=pl.BlockSpec((m, tn), lambda j: (0, j)),
        out_shape=jax.ShapeDtypeStruct((m, n), F32),
        compiler_params=_cparams(("arbitrary",)),
        name="ada",
    )(c_all, w_ada, b_ada.reshape(1, n))


def _norm_kernel(xp_ref, xs_ref, g_ref, scp_ref, shp_ref, scs_ref, shs_ref, o_ref, *, n_prompt):
    i = pl.program_id(0)

    def body(x, scale, shift):
        r = x * lax.rsqrt(jnp.mean(x * x, axis=-1, keepdims=True) + NORM_EPS) * g_ref[...]
        o_ref[...] = (r * (1.0 + scale) + shift).astype(o_ref.dtype)

    @pl.when(i < n_prompt)
    def _():
        body(xp_ref[...], scp_ref[...], shp_ref[...])

    @pl.when(i >= n_prompt)
    def _():
        body(xs_ref[...], scs_ref[...], shs_ref[...])


def _norm(xp, xs, g, sc_p, sh_p, sc_s, sh_s):
    t, d = xp.shape
    n_prompt = t // ROW_TILE
    row = lambda i: (0, 0)
    return pl.pallas_call(
        functools.partial(_norm_kernel, n_prompt=n_prompt),
        grid=(n_prompt + 1,),
        in_specs=[pl.BlockSpec((ROW_TILE, d), lambda i: (jnp.minimum(i, n_prompt - 1), 0)),
                  pl.BlockSpec((SAMPLE_ROWS, d), row),
                  pl.BlockSpec((1, d), row), pl.BlockSpec((1, d), row), pl.BlockSpec((1, d), row),
                  pl.BlockSpec((SAMPLE_ROWS, d), row), pl.BlockSpec((SAMPLE_ROWS, d), row)],
        out_specs=pl.BlockSpec((ROW_TILE, d), lambda i: (i, 0)),
        out_shape=jax.ShapeDtypeStruct((t + SAMPLE_ROWS, d), BF16),
        compiler_params=_cparams(("arbitrary",)),
        name="norm",
    )(xp, xs, g, sc_p, sh_p, sc_s, sh_s)


def _mm_kernel(a_ref, w_ref, o_ref):
    o_ref[...] = jnp.dot(a_ref[...], w_ref[...], preferred_element_type=F32)


def _row_tile(m, cap=1100):
    best = 16
    for tm in range(16, cap + 1, 16):
        if m % tm == 0:
            best = tm
    return best


def _matmul(a, w, tn=512):
    m, k = a.shape
    n = w.shape[1]
    tm = _row_tile(m)
    return pl.pallas_call(
        _mm_kernel,
        grid=(m // tm, n // tn),
        in_specs=[pl.BlockSpec((tm, k), lambda i, j: (i, 0)),
                  pl.BlockSpec((k, tn), lambda i, j: (0, j))],
        out_specs=pl.BlockSpec((tm, tn), lambda i, j: (i, j)),
        out_shape=jax.ShapeDtypeStruct((m, n), F32),
        compiler_params=_cparams(("arbitrary", "arbitrary")),
        name="proj",
    )(a, w)


def _hgrn_kernel(*refs, chunk, levels, has_s0, layer):
    if has_s0:
        lbp_ref, og_ref, hq_ref, hf_ref, hi_ref, hz_ref, s0_ref, o_ref, sout_ref, st_ref = refs
    else:
        lbp_ref, og_ref, hq_ref, hf_ref, hi_ref, hz_ref, o_ref, sout_ref, st_ref = refs
        s0_ref = None
    C = chunk
    c = pl.program_id(1)
    n_chunks = pl.num_programs(1)

    @pl.when(c == 0)
    def _():
        for h in range(HGRN_HEADS):
            if has_s0:
                st_ref[h] = s0_ref[0, h].T
            else:
                st_ref[h] = jnp.zeros((HEAD_DIM, HEAD_DIM), F32)

    lbp = lbp_ref[...]
    e = jnp.exp(lbp - jnp.max(lbp, axis=0, keepdims=True))
    lb = jnp.sum(e[:layer + 1], axis=0, keepdims=True) / jnp.sum(e, axis=0, keepdims=True)

    f = lb + (1.0 - lb) * jax.nn.sigmoid(hf_ref[...])
    logf = jnp.log(f)
    k = 1.0 - f
    q = hq_ref[...]
    v = hi_ref[...]
    hz = hz_ref[...]
    n = q.shape[1]

    row = lax.broadcasted_iota(jnp.int32, (C, 1), 0)
    cum = logf
    s = 1
    while s < C:
        cum = cum + jnp.where(row >= s, pltpu.roll(cum, s, axis=0), 0.0)
        s *= 2
    last = cum[C - 1:C, :]
    qe = q * jnp.exp(cum)
    kd = k * jnp.exp(last - cum)
    elast = jnp.exp(last)

    lvl = []
    for h in levels:
        cum3 = cum.reshape(C // (2 * h), 2 * h, n)
        cmid = jnp.broadcast_to(cum3[:, h - 1:h, :], cum3.shape).reshape(C, n)
        right = ((row // h) % 2) == 1
        ql = jnp.where(right, q * jnp.exp(jnp.where(right, cum - cmid, 0.0)), 0.0)
        kl = jnp.where(right, 0.0, k * jnp.exp(jnp.where(right, 0.0, cmid - cum)))
        ti = lax.broadcasted_iota(jnp.int32, (C, C), 0)
        si = lax.broadcasted_iota(jnp.int32, (C, C), 1)
        mask = ((ti // (2 * h)) == (si // (2 * h))) & (((ti // h) % 2) == 1) & (((si // h) % 2) == 0)
        lvl.append((ql, kl, mask))

    nb = C // 8
    rin = lax.broadcasted_iota(jnp.int32, (nb, 8, 1), 1)
    og = og_ref[...]
    for h in range(HGRN_HEADS):
        sl = slice(h * HEAD_DIM, (h + 1) * HEAD_DIM)
        st = st_ref[h]
        q_h, k_h, v_h, cum_h = q[:, sl], k[:, sl], v[:, sl], cum[:, sl]
        o_h = _dot_nt(qe[:, sl], st)
        if levels:
            a = jnp.zeros((C, C), F32)
            for ql, kl, mask in lvl:
                a = a + jnp.where(mask, _dot_nt(ql[:, sl], kl[:, sl]), 0.0)
            o_h = o_h + _dot(a, v_h)
        q3 = q_h.reshape(nb, 8, HEAD_DIM)
        k3 = k_h.reshape(nb, 8, HEAD_DIM)
        v3 = v_h.reshape(nb, 8, HEAD_DIM)
        c3 = cum_h.reshape(nb, 8, HEAD_DIM)
        od = jnp.zeros((nb, 8, HEAD_DIM), F32)
        for s in range(8):
            dec = jnp.exp(jnp.where(rin >= s, c3 - c3[:, s:s + 1, :], -jnp.inf))
            w = q3 * k3[:, s:s + 1, :] * dec
            od = od + jnp.sum(w, axis=-1, keepdims=True) * v3[:, s:s + 1, :]
        o_h = o_h + od.reshape(C, HEAD_DIM)
        st_ref[h] = st * elast[:, sl] + _dot_tn(v_h, kd[:, sl])
        o_h = o_h * lax.rsqrt(jnp.mean(o_h * o_h, axis=-1, keepdims=True) + NORM_EPS) * og
        o_ref[:, sl] = (o_h * _silu(hz[:, sl])).astype(o_ref.dtype)

    @pl.when(c == n_chunks - 1)
    def _():
        for h in range(HGRN_HEADS):
            sout_ref[0, h] = st_ref[h].T


def _hgrn(proj, lb_param, out_g, s0, *, row0, n_batch, t_len, chunk, out_dtype, layer):
    n_chunks = t_len // chunk
    levels = []
    h = chunk // 2
    while h >= 8:
        levels.append(h)
        h //= 2
    blk0 = row0 // chunk
    rmap = lambda col: (lambda b, c: (blk0 + b * n_chunks + c, col))
    in_specs = [pl.BlockSpec(lb_param.shape, lambda b, c: (0, 0)),
                pl.BlockSpec((1, HEAD_DIM), lambda b, c: (0, 0)),
                pl.BlockSpec((chunk, HGRN_WIDTH), rmap(0)),
                pl.BlockSpec((chunk, HGRN_WIDTH), rmap(1)),
                pl.BlockSpec((chunk, HGRN_WIDTH), rmap(2)),
                pl.BlockSpec((chunk, HGRN_WIDTH), rmap(3))]
    args = [lb_param, out_g.reshape(1, HEAD_DIM), proj, proj, proj, proj]
    if s0 is not None:
        in_specs.append(pl.BlockSpec((1, HGRN_HEADS, HEAD_DIM, HEAD_DIM), lambda b, c: (b, 0, 0, 0)))
        args.append(s0)
    return pl.pallas_call(
        functools.partial(_hgrn_kernel, chunk=chunk, levels=tuple(levels), has_s0=s0 is not None, layer=layer),
        grid=(n_batch, n_chunks),
        in_specs=in_specs,
        out_specs=[pl.BlockSpec((chunk, HGRN_WIDTH), lambda b, c: (b * n_chunks + c, 0)),
                   pl.BlockSpec((1, HGRN_HEADS, HEAD_DIM, HEAD_DIM), lambda b, c: (b, 0, 0, 0))],
        out_shape=[jax.ShapeDtypeStruct((n_batch * t_len, HGRN_WIDTH), out_dtype),
                   jax.ShapeDtypeStruct((n_batch, HGRN_HEADS, HEAD_DIM, HEAD_DIM), F32)],
        scratch_shapes=[pltpu.VMEM((HGRN_HEADS, HEAD_DIM, HEAD_DIM), F32)],
        compiler_params=_cparams(("arbitrary", "arbitrary")),
        name="hgrn",
    )(*args)


def _head_rms(x, g, n_heads):
    outs = []
    for h in range(n_heads):
        xh = x[:, h * HEAD_DIM:(h + 1) * HEAD_DIM]
        outs.append(xh * lax.rsqrt(jnp.mean(xh * xh, axis=-1, keepdims=True) + NORM_EPS) * g)
    return jnp.concatenate(outs, axis=1)


def _prep_kernel(nq_ref, rows_ref, win_ref, ng_ref, qg_ref, kg_ref, q_ref, rows_o, win_o, selb_o, winb_o, gate_o):
    kg = kg_ref[...]
    q_ref[...] = _head_rms(nq_ref[...], qg_ref[...], NSA_HEADS) * (HEAD_DIM ** -0.5)
    rows = rows_ref[...]
    ks = _head_rms(rows[:, 2 * KV_WIDTH:3 * KV_WIDTH], kg[1:2], NSA_KV_HEADS)
    rows_o[:, :2 * KV_WIDTH] = rows[:, :2 * KV_WIDTH]
    rows_o[:, 2 * KV_WIDTH:3 * KV_WIDTH] = ks
    rows_o[:, 3 * KV_WIDTH:] = rows[:, 3 * KV_WIDTH:]
    selb_o[:, :KV_WIDTH] = ks.astype(BF16)
    selb_o[:, KV_WIDTH:] = rows[:, 3 * KV_WIDTH:].astype(BF16)
    win = win_ref[...]
    kw = _head_rms(win[:, :KV_WIDTH], kg[2:3], NSA_KV_HEADS)
    win_o[:, :KV_WIDTH] = kw
    win_o[:, KV_WIDTH:] = win[:, KV_WIDTH:]
    winb_o[:, :KV_WIDTH] = kw.astype(BF16)
    winb_o[:, KV_WIDTH:] = win[:, KV_WIDTH:].astype(BF16)
    gate_o[...] = jax.nn.sigmoid(ng_ref[...])


def _prep(proj, q_g, k_g):
    m = proj.shape[0]
    t = ROW_TILE
    return pl.pallas_call(
        _prep_kernel,
        grid=(m // t,),
        in_specs=[pl.BlockSpec((t, NSA_WIDTH), lambda i: (i, COL_NQ // 2048)),
                  pl.BlockSpec((t, 2048), lambda i: (i, COL_ROWS // 2048)),
                  pl.BlockSpec((t, 1024), lambda i: (i, COL_WIN // 1024)),
                  pl.BlockSpec((t, 128), lambda i: (i, COL_NG // 128)),
                  pl.BlockSpec((1, HEAD_DIM), lambda i: (0, 0)),
                  pl.BlockSpec((3, HEAD_DIM), lambda i: (0, 0))],
        out_specs=[pl.BlockSpec((t, 2048), lambda i: (i, 0)),
                   pl.BlockSpec((t, 2048), lambda i: (i, 0)),
                   pl.BlockSpec((t, 1024), lambda i: (i, 0)),
                   pl.BlockSpec((t, 1024), lambda i: (i, 0)),
                   pl.BlockSpec((t, 1024), lambda i: (i, 0)),
                   pl.BlockSpec((t, 128), lambda i: (i, 0))],
        out_shape=[jax.ShapeDtypeStruct((m, 2048), F32),
                   jax.ShapeDtypeStruct((m, 2048), F32),
                   jax.ShapeDtypeStruct((m, 1024), F32),
                   jax.ShapeDtypeStruct((m, 1024), BF16),
                   jax.ShapeDtypeStruct((m, 1024), BF16),
                   jax.ShapeDtypeStruct((m, 128), F32)],
        compiler_params=_cparams(("arbitrary",)),
        name="prep",
    )(proj, proj, proj, proj, q_g.reshape(1, HEAD_DIM), k_g)


def _compress_kernel(pt_ref, *refs):
    pages = refs[:PAGES_PER_TILE]
    wc_ref, w1f_ref, pe_ref, w2_ref, kcg_ref, kc_ref, vc_ref, cvec_ref, carry_ref = refs[PAGES_PER_TILE:]
    b = pl.program_id(0)
    i = pl.program_id(1)
    nsub = PAGE // CMP_STRIDE
    ntile = nsub * PAGES_PER_TILE

    @pl.when((b == 0) & (i == 0))
    def _():
        for ch in range(2):
            pe = jnp.broadcast_to(pe_ref[ch], (8, CMP_LEN * HEAD_DIM))
            cvec_ref[ch] = _dot(pe, w1f_ref[ch])

    @pl.when(i == 0)
    def _():
        carry_ref[...] = jnp.zeros(carry_ref.shape, F32)

    for ch in range(2):
        u = jnp.zeros((NSA_KV_HEADS * ntile, 2 * CMP_HIDDEN), F32)
        for pp in range(CMP_STRIDE // 2):
            cols = []
            for p in (2 * pp, 2 * pp + 1):
                pieces = []
                for g in range(NSA_KV_HEADS):
                    for r in range(PAGES_PER_TILE):
                        pieces.append(pages[r][0, pl.ds(p, nsub, stride=CMP_STRIDE), ch * NSA_KV_HEADS + g, :])
                cols.append(jnp.concatenate(pieces, axis=0))
            lhs = jnp.concatenate(cols, axis=1)
            u = u + _dot(lhs, wc_ref[ch, pp])
        cvec = cvec_ref[ch][0:1, :]
        rowi = lax.broadcasted_iota(jnp.int32, (ntile, 1), 0)
        for g in range(NSA_KV_HEADS):
            ug = u[g * ntile:(g + 1) * ntile]
            u0, u1 = ug[:, :CMP_HIDDEN], ug[:, CMP_HIDDEN:]
            prev = jnp.where(rowi == 0, carry_ref[ch, g][0:1, :], pltpu.roll(u0, 1, axis=0))
            carry_ref[ch, g] = jnp.broadcast_to(u0[ntile - 1:ntile, :], (8, CMP_HIDDEN))
            pre = prev + u1 + cvec
            out = _dot(_silu(pre), w2_ref[ch])
            if ch == 0:
                out = out * lax.rsqrt(jnp.mean(out * out, axis=-1, keepdims=True) + NORM_EPS) * kcg_ref[...]
                kc_ref[0, g] = out.astype(kc_ref.dtype)
            else:
                vc_ref[0, g] = out.astype(vc_ref.dtype)


def _compress(src, page_table, wc, w1f, pe_flat, w2, kc_g):
    nb, n_pages = page_table.shape
    n_tiles = n_pages // PAGES_PER_TILE
    nsub = PAGE // CMP_STRIDE
    ntile = nsub * PAGES_PER_TILE

    def page_spec(r):
        return pl.BlockSpec((1, PAGE, 2 * NSA_KV_HEADS, HEAD_DIM),
                            lambda b, i, pt: (pt[b, i * PAGES_PER_TILE + r], 0, 0, 0))

    const = lambda *shape: pl.BlockSpec(shape, lambda b, i, pt: (0,) * len(shape))
    out_spec = pl.BlockSpec((1, NSA_KV_HEADS, ntile, HEAD_DIM), lambda b, i, pt: (b, 0, i, 0))
    grid_spec = pltpu.PrefetchScalarGridSpec(
        num_scalar_prefetch=1,
        grid=(nb, n_tiles),
        in_specs=[page_spec(r) for r in range(PAGES_PER_TILE)] + [
            const(2, CMP_STRIDE // 2, 2 * HEAD_DIM, 2 * CMP_HIDDEN),
            const(2, CMP_LEN * HEAD_DIM, CMP_HIDDEN),
            const(2, 1, CMP_LEN * HEAD_DIM),
            const(2, CMP_HIDDEN, HEAD_DIM),
            const(1, HEAD_DIM)],
        out_specs=[out_spec, out_spec],
        scratch_shapes=[pltpu.VMEM((2, 8, CMP_HIDDEN), F32),
                        pltpu.VMEM((2, NSA_KV_HEADS, 8, CMP_HIDDEN), F32)],
    )
    shape = jax.ShapeDtypeStruct((nb, NSA_KV_HEADS, n_tiles * ntile, HEAD_DIM), BF16)
    return pl.pallas_call(
        _compress_kernel,
        grid_spec=grid_spec,
        out_shape=[shape, shape],
        compiler_params=_cparams(("arbitrary", "arbitrary")),
        name="compress",
    )(page_table, *([src] * PAGES_PER_TILE), wc, w1f, pe_flat, w2, kc_g.reshape(1, HEAD_DIM))


def _split3(x):
    hi = x.astype(BF16)
    r = x - hi.astype(F32)
    mid = r.astype(BF16)
    lo = (r - mid.astype(F32)).astype(BF16)
    return hi, mid, lo


def _cmp_branch(qg, kc, vc, tpos4, n_sub):
    s = _dot_nt(qg, kc)
    npr = lax.broadcasted_iota(jnp.int32, (1, n_sub), 1)
    valid = (npr >= 1) & (npr * CMP_STRIDE + (CMP_STRIDE - 1) <= tpos4)
    s = jnp.where(valid, s, NEG)
    m = jnp.max(s, axis=-1, keepdims=True)
    p = jnp.where(valid, jnp.exp(s - m), 0.0)
    l = jnp.sum(p, axis=-1, keepdims=True)
    p = p * jnp.where(l > 0.0, 1.0 / l, 0.0)
    return _dot(p, vc), p


def _importance(psum, n_sub, nsp):
    npr = lax.broadcasted_iota(jnp.int32, (n_sub, nsp), 0)
    sb = lax.broadcasted_iota(jnp.int32, (n_sub, nsp), 1)
    r = SEL_BLOCK // CMP_STRIDE
    ov = ((npr >= r * sb) & (npr <= r * sb + r) & (npr >= 1)).astype(BF16)
    hi, mid, lo = _split3(psum)
    dot = lambda a: jnp.dot(a, ov, preferred_element_type=F32)
    return dot(hi) + dot(mid) + dot(lo)


def _select(imp, tpos, nsp):
    sb = lax.broadcasted_iota(jnp.int32, (1, nsp), 1)
    cur = tpos // SEL_BLOCK
    causal = sb * SEL_BLOCK <= tpos
    forced = (sb == 0) | (sb == cur) | (sb == cur - 1)
    score = jnp.where(causal, jnp.where(forced, -NEG, imp), NEG)
    sel = jnp.zeros(score.shape, jnp.bool_)
    sbf = sb.astype(F32)
    for _ in range(SEL_TOP):
        m = jnp.max(score, axis=-1, keepdims=True)
        idx = jnp.min(jnp.where(score == m, sbf, float(nsp)), axis=-1, keepdims=True)
        hit = sbf == idx
        sel = sel | hit
        score = jnp.where(hit, NEG, score)
    return sel & causal


def _online_step(carry, s, vt):
    m, l, acc = carry
    m_new = jnp.maximum(m, jnp.max(s, axis=-1, keepdims=True))
    alpha = jnp.exp(m - m_new)
    p = jnp.exp(s - m_new)
    l = alpha * l + jnp.sum(p, axis=-1, keepdims=True)
    acc = alpha * acc + _dot(p, vt)
    return m_new, l, acc


def _block_onehot(key0, n_keys, nsp):
    kb = (key0 + lax.broadcasted_iota(jnp.int32, (n_keys, 1), 0)) // SEL_BLOCK
    return (kb == lax.broadcasted_iota(jnp.int32, (1, nsp), 1)).astype(BF16)


def _window_branch(qg, kw, vw, kpos, tpos4):
    s = _dot_nt(qg, kw)
    mask = (kpos <= tpos4) & (kpos >= tpos4 - WINDOW) & (kpos >= 0)
    s = jnp.where(mask, s, NEG)
    m = jnp.max(s, axis=-1, keepdims=True)
    p = jnp.where(mask, jnp.exp(s - m), 0.0)
    l = jnp.sum(p, axis=-1, keepdims=True)
    return _dot(p, vw) * jnp.where(l > 0.0, 1.0 / l, 0.0)


def _gate_col(gates, g, br, tq):
    return jnp.concatenate(
        [gates[:, 3 * (NSA_GROUP * g + j) + br:3 * (NSA_GROUP * g + j) + br + 1] for j in range(NSA_GROUP)], axis=0)


def _finish(o_ref, oacc_ref, nz, og):
    o = oacc_ref[...]
    o = o * lax.rsqrt(jnp.mean(o * o, axis=-1, keepdims=True) + NORM_EPS) * og
    o_ref[...] = (o * _silu(nz)).astype(o_ref.dtype)


def _nsa_prompt_kernel(q_ref, gate_ref, nz_ref, kc_ref, vc_ref, selb_ref, *rest, n_sub, nsp):
    nwin = WINDOW // Q_BLOCK + 1
    win_refs = rest[:nwin]
    og_ref, o_ref, oacc_ref = rest[nwin:]
    i = pl.program_id(0)
    tq = Q_BLOCK
    t0 = i * tq
    tpos = t0 + lax.broadcasted_iota(jnp.int32, (tq, 1), 0)
    tpos4 = jnp.concatenate([tpos] * NSA_GROUP, axis=0)
    q = q_ref[...].astype(BF16)
    gates = gate_ref[...]

    qgs, ocs, imps = [], [], []
    for g in range(NSA_KV_HEADS):
        qg = jnp.concatenate([q[:, (NSA_GROUP * g + j) * HEAD_DIM:(NSA_GROUP * g + j + 1) * HEAD_DIM]
                              for j in range(NSA_GROUP)], axis=0)
        oc, p = _cmp_branch(qg, kc_ref[0, g], vc_ref[0, g], tpos4, n_sub)
        psum = p[0:tq] + p[tq:2 * tq] + p[2 * tq:3 * tq] + p[3 * tq:4 * tq]
        qgs.append(qg)
        ocs.append(oc)
        imps.append(_importance(psum, n_sub, nsp))
    sel = _select(jnp.concatenate(imps, axis=0), tpos4, nsp)

    kpos = t0 - WINDOW + lax.broadcasted_iota(jnp.int32, (1, WINDOW + tq), 1)
    n_full = t0 // KV_TILE
    kcol = lax.broadcasted_iota(jnp.int32, (1, KV_TILE), 1)
    for g in range(NSA_KV_HEADS):
        qg = qgs[g]
        bias = jnp.where(sel[g * tq:(g + 1) * tq], 0.0, NEG).astype(BF16)
        qaug = jnp.concatenate([qg, jnp.concatenate([bias] * NSA_GROUP, axis=0)], axis=1)

        def scores(c):
            k0 = pl.multiple_of(c * KV_TILE, KV_TILE)
            kt = selb_ref[pl.ds(k0, KV_TILE), pl.ds(g * HEAD_DIM, HEAD_DIM)]
            vt = selb_ref[pl.ds(k0, KV_TILE), pl.ds(KV_WIDTH + g * HEAD_DIM, HEAD_DIM)]
            kaug = jnp.concatenate([kt, _block_onehot(k0, KV_TILE, nsp)], axis=1)
            return _dot_nt(qaug, kaug), vt, k0

        def body(c, carry):
            s, vt, _ = scores(c)
            return _online_step(carry, s, vt)

        init = (jnp.full((NSA_GROUP * tq, 1), NEG, F32), jnp.zeros((NSA_GROUP * tq, 1), F32),
                jnp.zeros((NSA_GROUP * tq, HEAD_DIM), F32))
        carry = lax.fori_loop(0, n_full, body, init)
        s, vt, k0 = scores(n_full)
        s = jnp.where(k0 + kcol <= tpos4, s, NEG)
        m, l, acc = _online_step(carry, s, vt)
        o_s = acc / l

        kw = jnp.concatenate([w[:, g * HEAD_DIM:(g + 1) * HEAD_DIM] for w in win_refs], axis=0)
        vw = jnp.concatenate([w[:, KV_WIDTH + g * HEAD_DIM:KV_WIDTH + (g + 1) * HEAD_DIM] for w in win_refs], axis=0)
        o_w = _window_branch(qg, kw, vw, kpos, tpos4)

        o_g = (_gate_col(gates, g, 0, tq) * ocs[g] + _gate_col(gates, g, 1, tq) * o_s
               + _gate_col(gates, g, 2, tq) * o_w)
        for j in range(NSA_GROUP):
            hd = NSA_GROUP * g + j
            oacc_ref[:, hd * HEAD_DIM:(hd + 1) * HEAD_DIM] = o_g[j * tq:(j + 1) * tq]

    _finish(o_ref, oacc_ref, nz_ref[...], og_ref[...])


def _nsa_prompt(qn, gates, proj, kc, vc, selb, winb, nsa_g, t_len):
    n_sub = kc.shape[2]
    n_sel = t_len // SEL_BLOCK
    nsp = -(-n_sel // 128) * 128
    nq = t_len // Q_BLOCK
    nwin = WINDOW // Q_BLOCK + 1
    win_specs = [pl.BlockSpec((Q_BLOCK, 2 * KV_WIDTH),
                              (lambda r: (lambda i: (jnp.maximum(i - (nwin - 1) + r, 0), 0)))(r))
                 for r in range(nwin)]
    return pl.pallas_call(
        functools.partial(_nsa_prompt_kernel, n_sub=n_sub, nsp=nsp),
        grid=(nq,),
        in_specs=[pl.BlockSpec((Q_BLOCK, NSA_WIDTH), lambda i: (i, 0)),
                  pl.BlockSpec((Q_BLOCK, 128), lambda i: (i, 0)),
                  pl.BlockSpec((Q_BLOCK, NSA_WIDTH), lambda i: (i, COL_NZ // 2048)),
                  pl.BlockSpec((1, NSA_KV_HEADS, n_sub, HEAD_DIM), lambda i: (0, 0, 0, 0)),
                  pl.BlockSpec((1, NSA_KV_HEADS, n_sub, HEAD_DIM), lambda i: (0, 0, 0, 0)),
                  pl.BlockSpec((t_len, 2 * KV_WIDTH), lambda i: (0, 0), pipeline_mode=pl.Buffered(1))]
        + win_specs + [pl.BlockSpec((1, NSA_WIDTH), lambda i: (0, 0))],
        out_specs=pl.BlockSpec((Q_BLOCK, NSA_WIDTH), lambda i: (i, 0)),
        out_shape=jax.ShapeDtypeStruct((t_len, NSA_WIDTH), BF16),
        scratch_shapes=[pltpu.VMEM((Q_BLOCK, NSA_WIDTH), F32)],
        compiler_params=_cparams(("arbitrary",)),
        name="nsa_prompt",
    )(qn, gates, proj, kc, vc, selb, *([winb] * nwin), nsa_g.reshape(1, NSA_WIDTH))


def _nsa_sample_kernel(pt_ref, *refs, n_sub, nsp, past_len, t_new):
    pages = refs[:PAGES_PER_TILE]
    (q_ref, gate_ref, nz_ref, kc_ref, vc_ref, newsel_ref, swin_ref, newwin_ref, og_ref,
     o_ref, qaug_ref, oc_ref, m_ref, l_ref, acc_ref, oacc_ref) = refs[PAGES_PER_TILE:]
    tau = pl.program_id(1)
    n_tau = pl.num_programs(1)
    tq = t_new
    rows = NSA_GROUP * tq
    tpos = past_len + lax.broadcasted_iota(jnp.int32, (tq, 1), 0)
    tpos4 = jnp.concatenate([tpos] * NSA_GROUP, axis=0)

    def group_q():
        q = q_ref[...].astype(BF16)
        return [jnp.concatenate([q[:, (NSA_GROUP * g + j) * HEAD_DIM:(NSA_GROUP * g + j + 1) * HEAD_DIM]
                                 for j in range(NSA_GROUP)], axis=0) for g in range(NSA_KV_HEADS)]

    @pl.when(tau == 0)
    def _():
        qgs = group_q()
        imps = []
        for g in range(NSA_KV_HEADS):
            oc, p = _cmp_branch(qgs[g], kc_ref[0, g], vc_ref[0, g], tpos4, n_sub)
            oc_ref[g] = oc
            psum = p[0:tq]
            for j in range(1, NSA_GROUP):
                psum = psum + p[j * tq:(j + 1) * tq]
            imps.append(_importance(psum, n_sub, nsp))
        sel = _select(jnp.concatenate(imps, axis=0), tpos4, nsp)
        for g in range(NSA_KV_HEADS):
            bias = jnp.where(sel[g * tq:(g + 1) * tq], 0.0, NEG).astype(BF16)
            qaug_ref[g] = jnp.concatenate([qgs[g], jnp.concatenate([bias] * NSA_GROUP, axis=0)], axis=1)
            m_ref[g] = jnp.full((rows, 1), NEG, F32)
            l_ref[g] = jnp.zeros((rows, 1), F32)
            acc_ref[g] = jnp.zeros((rows, HEAD_DIM), F32)

    n_keys = PAGES_PER_TILE * PAGE
    onehot = _block_onehot(tau * n_keys, n_keys, nsp)
    for g in range(NSA_KV_HEADS):
        kt = jnp.concatenate([p[0, :, g * HEAD_DIM:(g + 1) * HEAD_DIM] for p in pages], axis=0).astype(BF16)
        vt = jnp.concatenate([p[0, :, KV_WIDTH + g * HEAD_DIM:KV_WIDTH + (g + 1) * HEAD_DIM] for p in pages], axis=0)
        s = _dot_nt(qaug_ref[g], jnp.concatenate([kt, onehot], axis=1))
        m, l, acc = _online_step((m_ref[g], l_ref[g], acc_ref[g]), s, vt)
        m_ref[g] = m
        l_ref[g] = l
        acc_ref[g] = acc

    @pl.when(tau == n_tau - 1)
    def _():
        qgs = group_q()
        gates = gate_ref[...]
        n_buf = swin_ref.shape[1]
        kpos = past_len - n_buf + lax.broadcasted_iota(jnp.int32, (1, n_buf + tq), 1)
        newcol = lax.broadcasted_iota(jnp.int32, (1, tq), 1)
        last_blk = past_len // SEL_BLOCK
        for g in range(NSA_KV_HEADS):
            qa = qaug_ref[g]
            kn = newsel_ref[:, g * HEAD_DIM:(g + 1) * HEAD_DIM]
            vn = newsel_ref[:, KV_WIDTH + g * HEAD_DIM:KV_WIDTH + (g + 1) * HEAD_DIM]
            bias_col = qa[:, HEAD_DIM + last_blk:HEAD_DIM + last_blk + 1].astype(F32)
            s = _dot_nt(qgs[g], kn) + bias_col
            s = jnp.where(past_len + newcol <= tpos4, s, NEG)
            m, l, acc = _online_step((m_ref[g], l_ref[g], acc_ref[g]), s, vn)
            o_s = acc / l
            kw = jnp.concatenate([swin_ref[0, :, g * HEAD_DIM:(g + 1) * HEAD_DIM],
                                  newwin_ref[:, g * HEAD_DIM:(g + 1) * HEAD_DIM]], axis=0)
            vw = jnp.concatenate([swin_ref[0, :, KV_WIDTH + g * HEAD_DIM:KV_WIDTH + (g + 1) * HEAD_DIM],
                                  newwin_ref[:, KV_WIDTH + g * HEAD_DIM:KV_WIDTH + (g + 1) * HEAD_DIM]], axis=0)
            o_w = _window_branch(qgs[g], kw, vw, kpos, tpos4)
            o_g = (_gate_col(gates, g, 0, tq) * oc_ref[g] + _gate_col(gates, g, 1, tq) * o_s
                   + _gate_col(gates, g, 2, tq) * o_w)
            for j in range(NSA_GROUP):
                hd = NSA_GROUP * g + j
                oacc_ref[:, hd * HEAD_DIM:(hd + 1) * HEAD_DIM] = o_g[j * tq:(j + 1) * tq]
        _finish(o_ref, oacc_ref, nz_ref[...], og_ref[...])


def _nsa_sample(cache, page_table, qn, gates, proj, kc, vc, rows_all, state_win, win_all, nsa_g, row0, t_new):
    nb, n_pages = page_table.shape
    past_len = n_pages * PAGE
    n_sub = kc.shape[2]
    n_sel = past_len // SEL_BLOCK + 1
    nsp = -(-n_sel // 128) * 128
    n_tau = n_pages // PAGES_PER_TILE
    rb = row0 // t_new
    n_buf = state_win.shape[1]
    rows = NSA_GROUP * t_new

    def page_spec(r):
        return pl.BlockSpec((1, PAGE, 2 * KV_WIDTH),
                            lambda b, i, pt: (pt[b, i * PAGES_PER_TILE + r], 0, 1))

    grid_spec = pltpu.PrefetchScalarGridSpec(
        num_scalar_prefetch=1,
        grid=(nb, n_tau),
        in_specs=[page_spec(r) for r in range(PAGES_PER_TILE)] + [
            pl.BlockSpec((t_new, NSA_WIDTH), lambda b, i, pt: (rb + b, 0)),
            pl.BlockSpec((t_new, 128), lambda b, i, pt: (rb + b, 0)),
            pl.BlockSpec((t_new, NSA_WIDTH), lambda b, i, pt: (rb + b, COL_NZ // 2048)),
            pl.BlockSpec((1, NSA_KV_HEADS, n_sub, HEAD_DIM), lambda b, i, pt: (b, 0, 0, 0)),
            pl.BlockSpec((1, NSA_KV_HEADS, n_sub, HEAD_DIM), lambda b, i, pt: (b, 0, 0, 0)),
            pl.BlockSpec((t_new, 2 * KV_WIDTH), lambda b, i, pt: (rb + b, 1)),
            pl.BlockSpec((1, n_buf, 2 * KV_WIDTH), lambda b, i, pt: (b, 0, 0)),
            pl.BlockSpec((t_new, 2 * KV_WIDTH), lambda b, i, pt: (rb + b, 0)),
            pl.BlockSpec((1, NSA_WIDTH), lambda b, i, pt: (0, 0))],
        out_specs=pl.BlockSpec((t_new, NSA_WIDTH), lambda b, i, pt: (b, 0)),
        scratch_shapes=[pltpu.VMEM((NSA_KV_HEADS, rows, HEAD_DIM + nsp), BF16),
                        pltpu.VMEM((NSA_KV_HEADS, rows, HEAD_DIM), F32),
                        pltpu.VMEM((NSA_KV_HEADS, rows, 1), F32),
                        pltpu.VMEM((NSA_KV_HEADS, rows, 1), F32),
                        pltpu.VMEM((NSA_KV_HEADS, rows, HEAD_DIM), F32),
                        pltpu.VMEM((t_new, NSA_WIDTH), F32)],
    )
    return pl.pallas_call(
        functools.partial(_nsa_sample_kernel, n_sub=n_sub, nsp=nsp, past_len=past_len, t_new=t_new),
        grid_spec=grid_spec,
        out_shape=jax.ShapeDtypeStruct((nb * t_new, NSA_WIDTH), F32),
        compiler_params=_cparams(("arbitrary", "arbitrary")),
        name="nsa_sample",
    )(page_table, *([cache] * PAGES_PER_TILE), qn, gates, proj, kc, vc, rows_all, state_win, win_all,
      nsa_g.reshape(1, NSA_WIDTH))


def _out_kernel(oh_ref, on_ref, wh_ref, wn_ref, x_ref, gate_ref, y_ref):
    acc = _dot(oh_ref[...], wh_ref[...]) + _dot(on_ref[...], wn_ref[...])
    y_ref[...] = x_ref[...] + gate_ref[...] * acc


def _out_proj(o_h, o_n, w_out, x, gate, tm, tn=512):
    m, d = x.shape
    half = o_h.shape[1]
    grow = gate.shape[0]
    gmap = (lambda i, j: (0, j)) if grow == 1 else (lambda i, j: (i, j))
    return pl.pallas_call(
        _out_kernel,
        grid=(m // tm, d // tn),
        in_specs=[pl.BlockSpec((tm, half), lambda i, j: (i, 0)),
                  pl.BlockSpec((tm, half), lambda i, j: (i, 0)),
                  pl.BlockSpec((half, tn), lambda i, j: (0, j)),
                  pl.BlockSpec((half, tn), lambda i, j: (1, j)),
                  pl.BlockSpec((tm, tn), lambda i, j: (i, j)),
                  pl.BlockSpec((1 if grow == 1 else tm, tn), gmap)],
        out_specs=pl.BlockSpec((tm, tn), lambda i, j: (i, j)),
        out_shape=jax.ShapeDtypeStruct((m, d), F32),
        compiler_params=_cparams(("arbitrary", "arbitrary")),
        name="out_proj",
    )(o_h, o_n, w_out, w_out, x, gate)


def _layer(xp, xs, c_all, cache, s_win, s_hgrn, page_table, lb_param, layer, norm_g, w_ada, b_ada, w_in,
           hgrn_out_g, q_g, k_g, pe, w1, w2, nsa_out_g, w_out):
    t_len = xp.shape[0]
    nb, t_new, _ = xs.shape
    n_s = nb * t_new
    d = D_MODEL

    mod = _ada(c_all, w_ada, b_ada)
    shift, scale, gate = mod[:, :d], mod[:, d:2 * d], mod[:, 2 * d:]
    rep = lambda a: jnp.pad(jnp.repeat(a[1:1 + nb], t_new, axis=0), ((0, SAMPLE_ROWS - n_s), (0, 0)))
    xs_pad = jnp.pad(xs.reshape(n_s, d), ((0, SAMPLE_ROWS - n_s), (0, 0)))
    h_all = _norm(xp, xs_pad, norm_g.reshape(1, d), scale[0:1], shift[0:1], rep(scale), rep(shift))

    w_in_r = jnp.concatenate(
        [w_in[:, :12288], w_in[:, 13360:15408], w_in[:, 12288:13312], w_in[:, 13312:13360],
         jnp.zeros((d, N_PROJ - 15408), w_in.dtype)], axis=1).astype(BF16)
    proj = _matmul(h_all, w_in_r)

    oh_p, st_p = _hgrn(proj, lb_param, hgrn_out_g, None, row0=0, n_batch=1, t_len=t_len,
                       chunk=min(64, t_len), out_dtype=BF16, layer=layer)
    oh_s, st_s = _hgrn(proj, lb_param, hgrn_out_g, s_hgrn, row0=t_len, n_batch=nb, t_len=t_new,
                       chunk=t_new, out_dtype=F32, layer=layer)

    qn, rows_all, win_all, selb, winb, gates = _prep(proj, q_g, k_g)

    w1r = w1.reshape(2, 2, CMP_STRIDE // 2, 2 * HEAD_DIM, CMP_HIDDEN)
    wc = jnp.concatenate([w1r[:, 0], w1r[:, 1]], axis=-1).astype(BF16)
    w1f = w1.reshape(2, CMP_LEN * HEAD_DIM, CMP_HIDDEN).astype(BF16)
    pe_flat = pe.reshape(2, 1, CMP_LEN * HEAD_DIM)
    w2b = w2.astype(BF16)

    n_pp = t_len // PAGE
    m_all = t_len + SAMPLE_ROWS
    kc_p, vc_p = _compress(rows_all.reshape(m_all // PAGE, PAGE, 16, HEAD_DIM),
                           jnp.arange(n_pp, dtype=jnp.int32).reshape(1, n_pp), wc, w1f, pe_flat, w2b, k_g[0])
    cache2 = cache.reshape(cache.shape[0], PAGE, 2048)
    kc_s, vc_s = _compress(cache.reshape(cache.shape[0], PAGE, 16, HEAD_DIM), page_table, wc, w1f, pe_flat, w2b, k_g[0])

    on_p = _nsa_prompt(qn, gates, proj, kc_p, vc_p, selb, winb, nsa_out_g, t_len)
    s_win2 = s_win.reshape(nb, s_win.shape[1], 2 * KV_WIDTH)
    on_s = _nsa_sample(cache2, page_table, qn, gates, proj, kc_s, vc_s, rows_all, s_win2, win_all,
                       nsa_out_g, t_len, t_new)

    w_out_b = w_out.astype(BF16)
    y_p = _out_proj(oh_p, on_p, w_out_b, xp, gate[0:1], tm=min(1024, t_len))
    y_s = _out_proj(oh_s, on_s, w_out_b, xs.reshape(n_s, d), jnp.repeat(gate[1:1 + nb], t_new, axis=0), tm=n_s)

    n_w = min(WINDOW, t_len)
    kv_p = rows_all[:t_len].reshape(1, t_len, 4, NSA_KV_HEADS, HEAD_DIM)
    kv_s = rows_all[t_len:t_len + n_s].reshape(nb, t_new, 4, NSA_KV_HEADS, HEAD_DIM)
    win_p = win_all[t_len - n_w:t_len].reshape(1, n_w, 2, NSA_KV_HEADS, HEAD_DIM)
    win_new = win_all[t_len:t_len + n_s].reshape(nb, t_new, 2, NSA_KV_HEADS, HEAD_DIM)
    n_buf = s_win.shape[1]
    win_s = jnp.concatenate([s_win, win_new], axis=1)[:, -n_buf:]
    return y_p, y_s.reshape(nb, t_new, d), kv_p, kv_s, win_p, win_s, st_p, st_s


def kernel(x_prompt, x_sample, cache_kv, state_win, state_hgrn, page_table, c_prompt, c_sample, norm_g, w_ada,
           b_ada, w_in, hgrn_lb, hgrn_out_g, q_norm_g, k_norm_g, cmp_pe, cmp_w1, cmp_w2, nsa_out_g, w_out):
    depth = w_in.shape[0]
    assert depth == 1 and x_prompt.shape[0] == 1
    nb = x_sample.shape[0]
    c_all = jnp.concatenate([c_prompt, c_sample], axis=0)
    c_all = jnp.pad(c_all, ((0, (-c_all.shape[0]) % 8), (0, 0)))
    l = 0
    outs = _layer(x_prompt[0], x_sample, c_all, cache_kv[l], state_win[l], state_hgrn[l], page_table,
                  hgrn_lb, l, norm_g[l], w_ada[l], b_ada[l], w_in[l], hgrn_out_g[l], q_norm_g[l], k_norm_g[l],
                  cmp_pe[l], cmp_w1[l], cmp_w2[l], nsa_out_g[l], w_out[l])
    y_p, y_s, kv_p, kv_s, win_p, win_s, st_p, st_s = outs
    return (y_p[None], y_s, kv_p[None], kv_s[None], win_p[None], win_s[None], st_p[None], st_s[None])
```

```python
import functools

import jax
import jax.numpy as jnp
from jax import lax
from jax.experimental import pallas as pl
from jax.experimental.pallas import tpu as pltpu

F32 = jnp.float32
BF16 = jnp.bfloat16

D_MODEL = 4096
HEAD_DIM = 128
HGRN_WIDTH = 2048
NSA_WIDTH = 2048
HGRN_HEADS = 16
NSA_HEADS = 16
NSA_KV_HEADS = 4
NSA_GROUP = 4
KV_WIDTH = 512
CMP_LEN = 32
CMP_STRIDE = 16
CMP_HIDDEN = 256
SEL_BLOCK = 64
SEL_TOP = 16
WINDOW = 512
Q_BLOCK = 128
NORM_EPS = 1e-6
PAGE = 128

COL_HQ, COL_HF, COL_HI, COL_HZ, COL_NQ, COL_ROWS, COL_NZ = (i * 2048 for i in range(7))
COL_WIN = 14336
COL_NG = 15360
N_PROJ = 15872
SAMPLE_ROWS = 256
ROW_TILE = 256
PAGES_PER_TILE = 8
KV_TILE = 512
NEG = -1e30
LOG2E = 1.4426950408889634
VMEM_LIMIT = 56 * 1024 * 1024


def _cparams(sem):
    return pltpu.CompilerParams(dimension_semantics=sem, vmem_limit_bytes=VMEM_LIMIT)


def _dot(a, b):
    return jnp.dot(a.astype(BF16), b.astype(BF16), preferred_element_type=F32)


def _dot_nt(a, b):
    return lax.dot_general(a.astype(BF16), b.astype(BF16), (((1,), (1,)), ((), ())),
                           preferred_element_type=F32)


def _dot_tn(a, b):
    return lax.dot_general(a.astype(BF16), b.astype(BF16), (((0,), (0,)), ((), ())),
                           preferred_element_type=F32)


def _silu(x):
    return x * jax.nn.sigmoid(x)


def _ada_kernel(c_ref, w_ref, b_ref, o_ref):
    c = c_ref[...]
    o_ref[...] = _dot(_silu(c), w_ref[...]) + b_ref[...]


def _ada(c_all, w_ada, b_ada):
    m, d = c_all.shape
    n = w_ada.shape[1]
    tn = 512
    return pl.pallas_call(
        _ada_kernel,
        grid=(n // tn,),
        in_specs=[pl.BlockSpec((m, d), lambda j: (0, 0)),
                  pl.BlockSpec((d, tn), lambda j: (0, j)),
                  pl.BlockSpec((1, tn), lambda j: (0, j))],
        out_specs=pl.BlockSpec((m, tn), lambda j: (0, j)),
        out_shape=jax.ShapeDtypeStruct((m, n), F32),
        compiler_params=_cparams(("arbitrary",)),
        name="ada",
    )(c_all, w_ada, b_ada.reshape(1, n))


def _norm_kernel(xp_ref, xs_ref, g_ref, scp_ref, shp_ref, scs_ref, shs_ref, o_ref, *, n_prompt):
    i = pl.program_id(0)

    def body(x, scale, shift):
        r = x * lax.rsqrt(jnp.mean(x * x, axis=-1, keepdims=True) + NORM_EPS) * g_ref[...]
        o_ref[...] = (r * (1.0 + scale) + shift).astype(o_ref.dtype)

    @pl.when(i < n_prompt)
    def _():
        body(xp_ref[...], scp_ref[...], shp_ref[...])

    @pl.when(i >= n_prompt)
    def _():
        body(xs_ref[...], scs_ref[...], shs_ref[...])


def _norm(xp, xs, g, sc_p, sh_p, sc_s, sh_s):
    t, d = xp.shape
    n_prompt = t // ROW_TILE
    row = lambda i: (0, 0)
    return pl.pallas_call(
        functools.partial(_norm_kernel, n_prompt=n_prompt),
        grid=(n_prompt + 1,),
        in_specs=[pl.BlockSpec((ROW_TILE, d), lambda i: (jnp.minimum(i, n_prompt - 1), 0)),
                  pl.BlockSpec((SAMPLE_ROWS, d), row),
                  pl.BlockSpec((1, d), row), pl.BlockSpec((1, d), row), pl.BlockSpec((1, d), row),
                  pl.BlockSpec((SAMPLE_ROWS, d), row), pl.BlockSpec((SAMPLE_ROWS, d), row)],
        out_specs=pl.BlockSpec((ROW_TILE, d), lambda i: (i, 0)),
        out_shape=jax.ShapeDtypeStruct((t + SAMPLE_ROWS, d), BF16),
        compiler_params=_cparams(("arbitrary",)),
        name="norm",
    )(xp, xs, g, sc_p, sh_p, sc_s, sh_s)


PROJ_TN = 512
_N_MAIN_A, _N_NZ, _N_WIN = 24, 4, 2


def _proj_main_tile(j):
    return jnp.where(j < _N_MAIN_A, j,
                     jnp.where(j < _N_MAIN_A + _N_NZ, _N_MAIN_A - 1,
                               jnp.where(j < _N_MAIN_A + _N_NZ + _N_WIN, j - _N_NZ, _N_MAIN_A + _N_WIN - 1)))


def _proj_tail_tile(j):
    return jnp.where(j < _N_MAIN_A, 0,
                     jnp.where(j < _N_MAIN_A + _N_NZ, j - _N_MAIN_A,
                               jnp.where(j < _N_MAIN_A + _N_NZ + _N_WIN, _N_NZ - 1, _N_NZ)))


def _mm_kernel(a_ref, wm_ref, wt_ref, o_ref):
    j = pl.program_id(1)
    tail = ((j >= _N_MAIN_A) & (j < _N_MAIN_A + _N_NZ)) | (j == _N_MAIN_A + _N_NZ + _N_WIN)

    @pl.when(tail)
    def _():
        o_ref[...] = jnp.dot(a_ref[...], wt_ref[...], preferred_element_type=F32)

    @pl.when(jnp.logical_not(tail))
    def _():
        o_ref[...] = jnp.dot(a_ref[...], wm_ref[...], preferred_element_type=F32)


def _row_tile(m, cap=1100):
    best = 16
    for tm in range(16, cap + 1, 16):
        if m % tm == 0:
            best = tm
    return best


def _matmul(a, w_main, w_tail):
    m, k = a.shape
    tn = PROJ_TN
    tm = _row_tile(m)
    return pl.pallas_call(
        _mm_kernel,
        grid=(m // tm, N_PROJ // tn),
        in_specs=[pl.BlockSpec((tm, k), lambda i, j: (i, 0)),
                  pl.BlockSpec((k, tn), lambda i, j: (0, _proj_main_tile(j))),
                  pl.BlockSpec((k, tn), lambda i, j: (0, _proj_tail_tile(j)))],
        out_specs=pl.BlockSpec((tm, tn), lambda i, j: (i, j)),
        out_shape=jax.ShapeDtypeStruct((m, N_PROJ), F32),
        compiler_params=_cparams(("arbitrary", "arbitrary")),
        name="proj",
    )(a, w_main, w_tail)


def _hgrn_kernel(*refs, chunk, levels, has_s0, layer):
    if has_s0:
        lbp_ref, og_ref, hq_ref, hf_ref, hi_ref, hz_ref, s0_ref, o_ref, sout_ref, st_ref = refs
    else:
        lbp_ref, og_ref, hq_ref, hf_ref, hi_ref, hz_ref, o_ref, sout_ref, st_ref = refs
        s0_ref = None
    C = chunk
    c = pl.program_id(1)
    n_chunks = pl.num_programs(1)

    @pl.when(c == 0)
    def _():
        for h in range(HGRN_HEADS):
            if has_s0:
                st_ref[h] = s0_ref[0, h].T
            else:
                st_ref[h] = jnp.zeros((HEAD_DIM, HEAD_DIM), F32)

    lbp = lbp_ref[...]
    e = jnp.exp(lbp - jnp.max(lbp, axis=0, keepdims=True))
    lb = jnp.sum(e[:layer + 1], axis=0, keepdims=True) / jnp.sum(e, axis=0, keepdims=True)

    f = lb + (1.0 - lb) * jax.nn.sigmoid(hf_ref[...])
    logf = jnp.log(f)
    k = 1.0 - f
    q = hq_ref[...]
    v = hi_ref[...]
    hz = hz_ref[...]
    n = q.shape[1]

    row = lax.broadcasted_iota(jnp.int32, (C, 1), 0)
    cum = logf
    s = 1
    while s < C:
        cum = cum + jnp.where(row >= s, pltpu.roll(cum, s, axis=0), 0.0)
        s *= 2
    last = cum[C - 1:C, :]
    qe = q * jnp.exp(cum)
    kd = k * jnp.exp(last - cum)
    elast = jnp.exp(last)

    lvl = []
    for h in levels:
        cum3 = cum.reshape(C // (2 * h), 2 * h, n)
        cmid = jnp.broadcast_to(cum3[:, h - 1:h, :], cum3.shape).reshape(C, n)
        right = ((row // h) % 2) == 1
        ql = jnp.where(right, q * jnp.exp(jnp.where(right, cum - cmid, 0.0)), 0.0)
        kl = jnp.where(right, 0.0, k * jnp.exp(jnp.where(right, 0.0, cmid - cum)))
        ti = lax.broadcasted_iota(jnp.int32, (C, C), 0)
        si = lax.broadcasted_iota(jnp.int32, (C, C), 1)
        mask = ((ti // (2 * h)) == (si // (2 * h))) & (((ti // h) % 2) == 1) & (((si // h) % 2) == 0)
        lvl.append((ql, kl, mask))

    nb = C // 8
    rin = lax.broadcasted_iota(jnp.int32, (nb, 8, 1), 1)
    og = og_ref[...]
    for h in range(HGRN_HEADS):
        sl = slice(h * HEAD_DIM, (h + 1) * HEAD_DIM)
        st = st_ref[h]
        q_h, k_h, v_h, cum_h = q[:, sl], k[:, sl], v[:, sl], cum[:, sl]
        o_h = _dot_nt(qe[:, sl], st)
        if levels:
            a = jnp.zeros((C, C), F32)
            for ql, kl, mask in lvl:
                a = a + jnp.where(mask, _dot_nt(ql[:, sl], kl[:, sl]), 0.0)
            o_h = o_h + _dot(a, v_h)
        q3 = q_h.reshape(nb, 8, HEAD_DIM)
        k3 = k_h.reshape(nb, 8, HEAD_DIM)
        v3 = v_h.reshape(nb, 8, HEAD_DIM)
        c3 = cum_h.reshape(nb, 8, HEAD_DIM)
        od = jnp.zeros((nb, 8, HEAD_DIM), F32)
        for s in range(8):
            dec = jnp.exp(jnp.where(rin >= s, c3 - c3[:, s:s + 1, :], -jnp.inf))
            w = q3 * k3[:, s:s + 1, :] * dec
            od = od + jnp.sum(w, axis=-1, keepdims=True) * v3[:, s:s + 1, :]
        o_h = o_h + od.reshape(C, HEAD_DIM)
        st_ref[h] = st * elast[:, sl] + _dot_tn(v_h, kd[:, sl])
        o_h = o_h * lax.rsqrt(jnp.mean(o_h * o_h, axis=-1, keepdims=True) + NORM_EPS) * og
        o_ref[:, sl] = (o_h * _silu(hz[:, sl])).astype(o_ref.dtype)

    @pl.when(c == n_chunks - 1)
    def _():
        for h in range(HGRN_HEADS):
            sout_ref[0, h] = st_ref[h].T


def _hgrn(proj, lb_param, out_g, s0, *, row0, n_batch, t_len, chunk, out_dtype, layer):
    n_chunks = t_len // chunk
    levels = []
    h = chunk // 2
    while h >= 8:
        levels.append(h)
        h //= 2
    blk0 = row0 // chunk
    rmap = lambda col: (lambda b, c: (blk0 + b * n_chunks + c, col))
    in_specs = [pl.BlockSpec(lb_param.shape, lambda b, c: (0, 0)),
                pl.BlockSpec((1, HEAD_DIM), lambda b, c: (0, 0)),
                pl.BlockSpec((chunk, HGRN_WIDTH), rmap(0)),
                pl.BlockSpec((chunk, HGRN_WIDTH), rmap(1)),
                pl.BlockSpec((chunk, HGRN_WIDTH), rmap(2)),
                pl.BlockSpec((chunk, HGRN_WIDTH), rmap(3))]
    args = [lb_param, out_g.reshape(1, HEAD_DIM), proj, proj, proj, proj]
    if s0 is not None:
        in_specs.append(pl.BlockSpec((1, HGRN_HEADS, HEAD_DIM, HEAD_DIM), lambda b, c: (b, 0, 0, 0)))
        args.append(s0)
    return pl.pallas_call(
        functools.partial(_hgrn_kernel, chunk=chunk, levels=tuple(levels), has_s0=s0 is not None, layer=layer),
        grid=(n_batch, n_chunks),
        in_specs=in_specs,
        out_specs=[pl.BlockSpec((chunk, HGRN_WIDTH), lambda b, c: (b * n_chunks + c, 0)),
                   pl.BlockSpec((1, HGRN_HEADS, HEAD_DIM, HEAD_DIM), lambda b, c: (b, 0, 0, 0))],
        out_shape=[jax.ShapeDtypeStruct((n_batch * t_len, HGRN_WIDTH), out_dtype),
                   jax.ShapeDtypeStruct((n_batch, HGRN_HEADS, HEAD_DIM, HEAD_DIM), F32)],
        scratch_shapes=[pltpu.VMEM((HGRN_HEADS, HEAD_DIM, HEAD_DIM), F32)],
        compiler_params=_cparams(("arbitrary", "arbitrary")),
        name="hgrn",
    )(*args)


def _head_rms(x, g, n_heads):
    outs = []
    for h in range(n_heads):
        xh = x[:, h * HEAD_DIM:(h + 1) * HEAD_DIM]
        outs.append(xh * lax.rsqrt(jnp.mean(xh * xh, axis=-1, keepdims=True) + NORM_EPS) * g)
    return jnp.concatenate(outs, axis=1)


def _prep_kernel(nq_ref, rows_ref, win_ref, ng_ref, qg_ref, kg_ref, q_ref, rows_o, win_o, selb_o, winb_o, gate_o):
    kg = kg_ref[...]
    q_ref[...] = _head_rms(nq_ref[...], qg_ref[...], NSA_HEADS) * (HEAD_DIM ** -0.5 * LOG2E)
    rows = rows_ref[...]
    ks = _head_rms(rows[:, 2 * KV_WIDTH:3 * KV_WIDTH], kg[1:2], NSA_KV_HEADS)
    for cg in range(4 * NSA_KV_HEADS):
        sl = slice(cg * HEAD_DIM, (cg + 1) * HEAD_DIM)
        if 2 * NSA_KV_HEADS <= cg < 3 * NSA_KV_HEADS:
            rows_o[:, cg, :] = ks[:, (cg - 2 * NSA_KV_HEADS) * HEAD_DIM:(cg - 2 * NSA_KV_HEADS + 1) * HEAD_DIM]
        else:
            rows_o[:, cg, :] = rows[:, sl]
    selb_o[:, :KV_WIDTH] = ks.astype(BF16)
    selb_o[:, KV_WIDTH:] = rows[:, 3 * KV_WIDTH:].astype(BF16)
    win = win_ref[...]
    kw = _head_rms(win[:, :KV_WIDTH], kg[2:3], NSA_KV_HEADS)
    for cg in range(2 * NSA_KV_HEADS):
        sl = slice(cg * HEAD_DIM, (cg + 1) * HEAD_DIM)
        win_o[:, cg, :] = kw[:, sl] if cg < NSA_KV_HEADS else win[:, sl]
    winb_o[:, :KV_WIDTH] = kw.astype(BF16)
    winb_o[:, KV_WIDTH:] = win[:, KV_WIDTH:].astype(BF16)
    gate_o[...] = jax.nn.sigmoid(ng_ref[...])


def _prep(proj, q_g, k_g, row0, n_rows):
    t = min(ROW_TILE, n_rows)
    r0 = row0 // t
    return pl.pallas_call(
        _prep_kernel,
        grid=(n_rows // t,),
        in_specs=[pl.BlockSpec((t, NSA_WIDTH), lambda i: (r0 + i, COL_NQ // 2048)),
                  pl.BlockSpec((t, 2048), lambda i: (r0 + i, COL_ROWS // 2048)),
                  pl.BlockSpec((t, 1024), lambda i: (r0 + i, COL_WIN // 1024)),
                  pl.BlockSpec((t, 128), lambda i: (r0 + i, COL_NG // 128)),
                  pl.BlockSpec((1, HEAD_DIM), lambda i: (0, 0)),
                  pl.BlockSpec((3, HEAD_DIM), lambda i: (0, 0))],
        out_specs=[pl.BlockSpec((t, 2048), lambda i: (i, 0)),
                   pl.BlockSpec((t, 4 * NSA_KV_HEADS, HEAD_DIM), lambda i: (i, 0, 0)),
                   pl.BlockSpec((t, 2 * NSA_KV_HEADS, HEAD_DIM), lambda i: (i, 0, 0)),
                   pl.BlockSpec((t, 1024), lambda i: (i, 0)),
                   pl.BlockSpec((t, 1024), lambda i: (i, 0)),
                   pl.BlockSpec((t, 128), lambda i: (i, 0))],
        out_shape=[jax.ShapeDtypeStruct((n_rows, 2048), F32),
                   jax.ShapeDtypeStruct((n_rows, 4 * NSA_KV_HEADS, HEAD_DIM), F32),
                   jax.ShapeDtypeStruct((n_rows, 2 * NSA_KV_HEADS, HEAD_DIM), F32),
                   jax.ShapeDtypeStruct((n_rows, 1024), BF16),
                   jax.ShapeDtypeStruct((n_rows, 1024), BF16),
                   jax.ShapeDtypeStruct((n_rows, 128), F32)],
        compiler_params=_cparams(("arbitrary",)),
        name="prep",
    )(proj, proj, proj, proj, q_g.reshape(1, HEAD_DIM), k_g)


def _compress_kernel(pt_ref, *refs):
    pages = refs[:PAGES_PER_TILE]
    wc_ref, w1f_ref, pe_ref, w2_ref, kcg_ref, kc_ref, vc_ref, cvec_ref, carry_ref, out_scr = refs[PAGES_PER_TILE:]
    b = pl.program_id(0)
    i = pl.program_id(1)
    nsub = PAGE // CMP_STRIDE
    ntile = nsub * PAGES_PER_TILE
    G = NSA_KV_HEADS
    rows = ntile * G

    @pl.when((b == 0) & (i == 0))
    def _():
        for ch in range(2):
            pe = jnp.broadcast_to(pe_ref[ch], (8, CMP_LEN * HEAD_DIM))
            cvec_ref[ch] = _dot(pe, w1f_ref[ch])

    @pl.when(i == 0)
    def _():
        carry_ref[...] = jnp.zeros(carry_ref.shape, F32)

    def gathered(p):
        lo, hi = [], []
        for r in range(PAGES_PER_TILE):
            a = pages[r][0, pl.ds(p, nsub // 2, stride=2 * CMP_STRIDE), :, :]
            bb = pages[r][0, pl.ds(p + CMP_STRIDE, nsub // 2, stride=2 * CMP_STRIDE), :, :]
            lo.append(jnp.concatenate([a[:, :G], bb[:, :G]], axis=1).reshape(nsub * G, HEAD_DIM))
            hi.append(jnp.concatenate([a[:, G:], bb[:, G:]], axis=1).reshape(nsub * G, HEAD_DIM))
        return jnp.concatenate(lo, axis=0).astype(BF16), jnp.concatenate(hi, axis=0).astype(BF16)

    u = [jnp.zeros((rows, 2 * CMP_HIDDEN), F32) for _ in range(2)]
    for pp in range(CMP_STRIDE // 2):
        xa = gathered(2 * pp)
        xb = gathered(2 * pp + 1)
        for ch in range(2):
            lhs = jnp.concatenate([xa[ch], xb[ch]], axis=1)
            u[ch] = u[ch] + jnp.dot(lhs, wc_ref[ch, pp], preferred_element_type=F32)

    row8 = lax.broadcasted_iota(jnp.int32, (8, 1), 0)
    for ch in range(2):
        u0, u1 = u[ch][:, :CMP_HIDDEN], u[ch][:, CMP_HIDDEN:]
        rolled = pltpu.roll(u0, G, axis=0)
        head = jnp.where(row8 < G, carry_ref[ch], rolled[0:8])
        carry_ref[ch] = rolled[0:8]
        prev = jnp.concatenate([head, rolled[8:]], axis=0)
        pre = prev + u1 + cvec_ref[ch][0:1, :]
        out = _dot(_silu(pre), w2_ref[ch])
        if ch == 0:
            out = out * lax.rsqrt(jnp.mean(out * out, axis=-1, keepdims=True) + NORM_EPS) * kcg_ref[...]
        out_scr[...] = out
        dst = kc_ref if ch == 0 else vc_ref
        for g in range(G):
            dst[0, g] = out_scr[pl.ds(g, ntile, stride=G), :].astype(dst.dtype)


def _compress(src, page_table, wc, w1f, pe_flat, w2, kc_g):
    nb, n_pages = page_table.shape
    n_tiles = n_pages // PAGES_PER_TILE
    nsub = PAGE // CMP_STRIDE
    ntile = nsub * PAGES_PER_TILE

    def page_spec(r):
        return pl.BlockSpec((1, PAGE, 2 * NSA_KV_HEADS, HEAD_DIM),
                            lambda b, i, pt: (pt[b, i * PAGES_PER_TILE + r], 0, 0, 0))

    const = lambda *shape: pl.BlockSpec(shape, lambda b, i, pt: (0,) * len(shape))
    out_spec = pl.BlockSpec((1, NSA_KV_HEADS, ntile, HEAD_DIM), lambda b, i, pt: (b, 0, i, 0))
    grid_spec = pltpu.PrefetchScalarGridSpec(
        num_scalar_prefetch=1,
        grid=(nb, n_tiles),
        in_specs=[page_spec(r) for r in range(PAGES_PER_TILE)] + [
            const(2, CMP_STRIDE // 2, 2 * HEAD_DIM, 2 * CMP_HIDDEN),
            const(2, CMP_LEN * HEAD_DIM, CMP_HIDDEN),
            const(2, 1, CMP_LEN * HEAD_DIM),
            const(2, CMP_HIDDEN, HEAD_DIM),
            const(1, HEAD_DIM)],
        out_specs=[out_spec, out_spec],
        scratch_shapes=[pltpu.VMEM((2, 8, CMP_HIDDEN), F32),
                        pltpu.VMEM((2, 8, CMP_HIDDEN), F32),
                        pltpu.VMEM((NSA_KV_HEADS * ntile, HEAD_DIM), F32)],
    )
    shape = jax.ShapeDtypeStruct((nb, NSA_KV_HEADS, n_tiles * ntile, HEAD_DIM), BF16)
    return pl.pallas_call(
        _compress_kernel,
        grid_spec=grid_spec,
        out_shape=[shape, shape],
        compiler_params=_cparams(("arbitrary", "arbitrary")),
        name="compress",
    )(page_table, *([src] * PAGES_PER_TILE), wc, w1f, pe_flat, w2, kc_g.reshape(1, HEAD_DIM))


def _split3(x):
    hi = x.astype(BF16)
    r = x - hi.astype(F32)
    mid = r.astype(BF16)
    lo = (r - mid.astype(F32)).astype(BF16)
    return hi, mid, lo


def _cmp_branch(qg, kc, vc, tpos4, n_sub):
    s = _dot_nt(qg, kc)
    npr = lax.broadcasted_iota(jnp.int32, (1, n_sub), 1)
    valid = (npr >= 1) & (npr * CMP_STRIDE + (CMP_STRIDE - 1) <= tpos4)
    s = jnp.where(valid, s, NEG)
    m = jnp.max(s, axis=-1, keepdims=True)
    p = jnp.where(valid, jnp.exp2(s - m), 0.0)
    l = jnp.sum(p, axis=-1, keepdims=True)
    p = p * jnp.where(l > 0.0, 1.0 / l, 0.0)
    return _dot(p, vc), p


def _softmax_cols(s, mask):
    s = jnp.where(mask, s, NEG)
    m = jnp.max(s, axis=0, keepdims=True)
    p = jnp.where(mask, jnp.exp2(s - m), 0.0)
    l = jnp.sum(p, axis=0, keepdims=True)
    return p * jnp.where(l > 0.0, 1.0 / l, 0.0)


def _select_t(imp, tpos, nsp):
    sb = lax.broadcasted_iota(jnp.int32, (nsp, 1), 0)
    cur = tpos // SEL_BLOCK
    causal = sb * SEL_BLOCK <= tpos
    forced = (sb == 0) | (sb == cur) | (sb == cur - 1)
    score = jnp.where(causal, jnp.where(forced, -NEG, imp), NEG)
    sel = jnp.zeros(score.shape, jnp.bool_)
    sbf = sb.astype(F32)
    for _ in range(SEL_TOP):
        m = jnp.max(score, axis=0, keepdims=True)
        idx = jnp.min(jnp.where(score == m, sbf, float(nsp)), axis=0, keepdims=True)
        hit = sbf == idx
        sel = sel | hit
        score = jnp.where(hit, NEG, score)
    return sel & causal


def _importance(psum, n_sub, nsp):
    npr = lax.broadcasted_iota(jnp.int32, (n_sub, nsp), 0)
    sb = lax.broadcasted_iota(jnp.int32, (n_sub, nsp), 1)
    r = SEL_BLOCK // CMP_STRIDE
    ov = ((npr >= r * sb) & (npr <= r * sb + r) & (npr >= 1)).astype(BF16)
    hi, mid, lo = _split3(psum)
    dot = lambda a: jnp.dot(a, ov, preferred_element_type=F32)
    return dot(hi) + dot(mid) + dot(lo)


def _select(imp, tpos, nsp):
    sb = lax.broadcasted_iota(jnp.int32, (1, nsp), 1)
    cur = tpos // SEL_BLOCK
    causal = sb * SEL_BLOCK <= tpos
    forced = (sb == 0) | (sb == cur) | (sb == cur - 1)
    score = jnp.where(causal, jnp.where(forced, -NEG, imp), NEG)
    sel = jnp.zeros(score.shape, jnp.bool_)
    sbf = sb.astype(F32)
    for _ in range(SEL_TOP):
        m = jnp.max(score, axis=-1, keepdims=True)
        idx = jnp.min(jnp.where(score == m, sbf, float(nsp)), axis=-1, keepdims=True)
        hit = sbf == idx
        sel = sel | hit
        score = jnp.where(hit, NEG, score)
    return sel & causal


def _online_step(carry, s, vt):
    m, l, acc = carry
    m_new = jnp.maximum(m, jnp.max(s, axis=-1, keepdims=True))
    alpha = jnp.exp2(m - m_new)
    p = jnp.exp2(s - m_new)
    l = alpha * l + jnp.sum(p, axis=-1, keepdims=True)
    acc = alpha * acc + _dot(p, vt)
    return m_new, l, acc


def _block_onehot(key0, n_keys, nsp):
    shift = SEL_BLOCK.bit_length() - 1
    kb = jnp.right_shift(lax.broadcasted_iota(jnp.int32, (n_keys, 1), 0), shift)
    sb = lax.broadcasted_iota(jnp.int32, (1, nsp), 1) - jnp.right_shift(key0, shift)
    return (kb == sb).astype(BF16)


def _window_branch(qg, kw, vw, kpos, tpos4):
    s = _dot_nt(qg, kw)
    mask = (kpos <= tpos4) & (kpos >= tpos4 - WINDOW) & (kpos >= 0)
    s = jnp.where(mask, s, NEG)
    m = jnp.max(s, axis=-1, keepdims=True)
    p = jnp.where(mask, jnp.exp2(s - m), 0.0)
    l = jnp.sum(p, axis=-1, keepdims=True)
    return _dot(p, vw) * jnp.where(l > 0.0, 1.0 / l, 0.0)


def _gate_col(gates, g, br, tq):
    return jnp.concatenate(
        [gates[:, 3 * (NSA_GROUP * g + j) + br:3 * (NSA_GROUP * g + j) + br + 1] for j in range(NSA_GROUP)], axis=0)


def _finish(o_ref, oacc_ref, nz, og):
    o = oacc_ref[...]
    o = o * lax.rsqrt(jnp.mean(o * o, axis=-1, keepdims=True) + NORM_EPS) * og
    o_ref[...] = (o * _silu(nz)).astype(o_ref.dtype)


def _nsa_prompt_kernel(q_ref, gate_ref, nz_ref, kc_ref, vc_ref, selb_ref, *rest, n_sub, nsp):
    nwin = WINDOW // Q_BLOCK + 1
    win_refs = rest[:nwin]
    og_ref, o_ref, oacc_ref = rest[nwin:]
    i = pl.program_id(0)
    tq = Q_BLOCK
    lanes = NSA_GROUP * tq
    t0 = i * tq
    tpos = t0 + lax.broadcasted_iota(jnp.int32, (1, tq), 1)
    tpos4 = jnp.concatenate([tpos] * NSA_GROUP, axis=1)
    q = q_ref[...]
    gates_t = gate_ref[...].T

    def gate_row(g, br):
        rows = [3 * (NSA_GROUP * g + j) + br for j in range(NSA_GROUP)]
        return jnp.concatenate([gates_t[r:r + 1, :] for r in rows], axis=1)

    npr = lax.broadcasted_iota(jnp.int32, (n_sub, 1), 0)
    cvalid = (npr >= 1) & (npr * CMP_STRIDE + (CMP_STRIDE - 1) <= tpos4)
    sb_o = lax.broadcasted_iota(jnp.int32, (nsp, n_sub), 0)
    np_o = lax.broadcasted_iota(jnp.int32, (nsp, n_sub), 1)
    r = SEL_BLOCK // CMP_STRIDE
    ovt = ((np_o >= r * sb_o) & (np_o <= r * sb_o + r) & (np_o >= 1)).astype(BF16)

    qts, octs, imps = [], [], []
    for g in range(NSA_KV_HEADS):
        qt = jnp.concatenate([q[:, (NSA_GROUP * g + j) * HEAD_DIM:(NSA_GROUP * g + j + 1) * HEAD_DIM].T
                              for j in range(NSA_GROUP)], axis=1).astype(BF16)
        p = _softmax_cols(jnp.dot(kc_ref[0, g], qt, preferred_element_type=F32), cvalid)
        octs.append(_dot_tn(vc_ref[0, g], p))
        psum = p[:, 0:tq]
        for j in range(1, NSA_GROUP):
            psum = psum + p[:, j * tq:(j + 1) * tq]
        imps.append(sum(jnp.dot(ovt, part, preferred_element_type=F32) for part in _split3(psum)))
        qts.append(qt)
    sel = _select_t(jnp.concatenate(imps, axis=1), tpos4, nsp)
    bias_t = jnp.where(sel, 0.0, NEG).astype(BF16)

    kpos = t0 - WINDOW + lax.broadcasted_iota(jnp.int32, (WINDOW + tq, 1), 0)
    wmask = (kpos <= tpos4) & (kpos >= tpos4 - WINDOW) & (kpos >= 0)
    n_full = t0 // KV_TILE
    krow = lax.broadcasted_iota(jnp.int32, (KV_TILE, 1), 0)
    qaugs = []
    for g in range(NSA_KV_HEADS):
        bg = bias_t[:, g * tq:(g + 1) * tq]
        qaugs.append(jnp.concatenate([qts[g], jnp.concatenate([bg] * NSA_GROUP, axis=1)], axis=0))

    def tile_step(c, carries, diagonal):
        k0 = pl.multiple_of(c * KV_TILE, KV_TILE)
        onehot = _block_onehot(k0, KV_TILE, nsp)
        out = []
        for g in range(NSA_KV_HEADS):
            m, l, acc = carries[g]
            kt = selb_ref[pl.ds(k0, KV_TILE), pl.ds(g * HEAD_DIM, HEAD_DIM)]
            vt = selb_ref[pl.ds(k0, KV_TILE), pl.ds(KV_WIDTH + g * HEAD_DIM, HEAD_DIM)]
            s = jnp.dot(jnp.concatenate([kt, onehot], axis=1), qaugs[g], preferred_element_type=F32)
            if diagonal:
                s = jnp.where(k0 + krow <= tpos4, s, NEG)
            m_new = jnp.maximum(m, jnp.max(s, axis=0, keepdims=True))
            alpha = jnp.exp2(m - m_new)
            p = jnp.exp2(s - m_new)
            l = alpha * l + jnp.sum(p, axis=0, keepdims=True)
            out.append((m_new, l, alpha * acc + _dot_tn(vt, p)))
        return tuple(out)

    init = tuple((jnp.full((1, lanes), NEG, F32), jnp.zeros((1, lanes), F32), jnp.zeros((HEAD_DIM, lanes), F32))
                 for _ in range(NSA_KV_HEADS))
    carries = lax.fori_loop(0, n_full, lambda c, cr: tile_step(c, cr, False), init)
    carries = tile_step(n_full, carries, True)

    for g in range(NSA_KV_HEADS):
        qt = qts[g]
        m, l, acc = carries[g]
        o_s = acc * (1.0 / l)

        kw = jnp.concatenate([w[:, g * HEAD_DIM:(g + 1) * HEAD_DIM] for w in win_refs], axis=0)
        vw = jnp.concatenate([w[:, KV_WIDTH + g * HEAD_DIM:KV_WIDTH + (g + 1) * HEAD_DIM] for w in win_refs], axis=0)
        o_w = _dot_tn(vw, _softmax_cols(jnp.dot(kw, qt, preferred_element_type=F32), wmask))

        o_g = gate_row(g, 0) * octs[g] + gate_row(g, 1) * o_s + gate_row(g, 2) * o_w
        for j in range(NSA_GROUP):
            hd = NSA_GROUP * g + j
            oacc_ref[:, hd * HEAD_DIM:(hd + 1) * HEAD_DIM] = o_g[:, j * tq:(j + 1) * tq].T

    _finish(o_ref, oacc_ref, nz_ref[...], og_ref[...])


def _nsa_prompt(qn, gates, proj, kc, vc, selb, winb, nsa_g, t_len):
    n_sub = kc.shape[2]
    n_sel = t_len // SEL_BLOCK
    nsp = -(-n_sel // 128) * 128
    nq = t_len // Q_BLOCK
    nwin = WINDOW // Q_BLOCK + 1
    win_specs = [pl.BlockSpec((Q_BLOCK, 2 * KV_WIDTH),
                              (lambda r: (lambda i: (jnp.maximum(i - (nwin - 1) + r, 0), 0)))(r))
                 for r in range(nwin)]
    return pl.pallas_call(
        functools.partial(_nsa_prompt_kernel, n_sub=n_sub, nsp=nsp),
        grid=(nq,),
        in_specs=[pl.BlockSpec((Q_BLOCK, NSA_WIDTH), lambda i: (i, 0)),
                  pl.BlockSpec((Q_BLOCK, 128), lambda i: (i, 0)),
                  pl.BlockSpec((Q_BLOCK, NSA_WIDTH), lambda i: (i, COL_NZ // 2048)),
                  pl.BlockSpec((1, NSA_KV_HEADS, n_sub, HEAD_DIM), lambda i: (0, 0, 0, 0)),
                  pl.BlockSpec((1, NSA_KV_HEADS, n_sub, HEAD_DIM), lambda i: (0, 0, 0, 0)),
                  pl.BlockSpec((t_len, 2 * KV_WIDTH), lambda i: (0, 0), pipeline_mode=pl.Buffered(1))]
        + win_specs + [pl.BlockSpec((1, NSA_WIDTH), lambda i: (0, 0))],
        out_specs=pl.BlockSpec((Q_BLOCK, NSA_WIDTH), lambda i: (i, 0)),
        out_shape=jax.ShapeDtypeStruct((t_len, NSA_WIDTH), BF16),
        scratch_shapes=[pltpu.VMEM((Q_BLOCK, NSA_WIDTH), F32)],
        compiler_params=_cparams(("arbitrary",)),
        name="nsa_prompt",
    )(qn, gates, proj, kc, vc, selb, *([winb] * nwin), nsa_g.reshape(1, NSA_WIDTH))


def _nsa_sample_kernel(pt_ref, *refs, n_sub, nsp, past_len, t_new):
    pages = refs[:PAGES_PER_TILE]
    (q_ref, gate_ref, nz_ref, kc_ref, vc_ref, newsel_ref, swin_ref, newwin_ref, og_ref,
     o_ref, qaug_ref, oc_ref, m_ref, l_ref, acc_ref, oacc_ref) = refs[PAGES_PER_TILE:]
    tau = pl.program_id(1)
    n_tau = pl.num_programs(1)
    tq = t_new
    rows = NSA_GROUP * tq
    tpos = past_len + lax.broadcasted_iota(jnp.int32, (tq, 1), 0)
    tpos4 = jnp.concatenate([tpos] * NSA_GROUP, axis=0)

    def group_q():
        q = q_ref[...].astype(BF16)
        return [jnp.concatenate([q[:, (NSA_GROUP * g + j) * HEAD_DIM:(NSA_GROUP * g + j + 1) * HEAD_DIM]
                                 for j in range(NSA_GROUP)], axis=0) for g in range(NSA_KV_HEADS)]

    @pl.when(tau == 0)
    def _():
        qgs = group_q()
        imps = []
        for g in range(NSA_KV_HEADS):
            oc, p = _cmp_branch(qgs[g], kc_ref[0, g], vc_ref[0, g], tpos4, n_sub)
            oc_ref[g] = oc
            psum = p[0:tq]
            for j in range(1, NSA_GROUP):
                psum = psum + p[j * tq:(j + 1) * tq]
            imps.append(_importance(psum, n_sub, nsp))
        sel = _select(jnp.concatenate(imps, axis=0), tpos4, nsp)
        for g in range(NSA_KV_HEADS):
            bias = jnp.where(sel[g * tq:(g + 1) * tq], 0.0, NEG).astype(BF16)
            qaug_ref[g] = jnp.concatenate([qgs[g], jnp.concatenate([bias] * NSA_GROUP, axis=0)], axis=1)
            m_ref[g] = jnp.full((rows, 1), NEG, F32)
            l_ref[g] = jnp.zeros((rows, 1), F32)
            acc_ref[g] = jnp.zeros((rows, HEAD_DIM), F32)

    n_keys = PAGES_PER_TILE * PAGE
    onehot = _block_onehot(tau * n_keys, n_keys, nsp)
    for g in range(NSA_KV_HEADS):
        kt = jnp.concatenate([p[0, :, g, :] for p in pages], axis=0).astype(BF16)
        vt = jnp.concatenate([p[0, :, NSA_KV_HEADS + g, :] for p in pages], axis=0)
        s = _dot_nt(qaug_ref[g], jnp.concatenate([kt, onehot], axis=1))
        m, l, acc = _online_step((m_ref[g], l_ref[g], acc_ref[g]), s, vt)
        m_ref[g] = m
        l_ref[g] = l
        acc_ref[g] = acc

    @pl.when(tau == n_tau - 1)
    def _():
        qgs = group_q()
        gates = gate_ref[...]
        n_buf = swin_ref.shape[1]
        kpos = past_len - n_buf + lax.broadcasted_iota(jnp.int32, (1, n_buf + tq), 1)
        newcol = lax.broadcasted_iota(jnp.int32, (1, tq), 1)
        last_blk = past_len // SEL_BLOCK
        for g in range(NSA_KV_HEADS):
            qa = qaug_ref[g]
            kn = newsel_ref[:, g, :]
            vn = newsel_ref[:, NSA_KV_HEADS + g, :]
            bias_col = qa[:, HEAD_DIM + last_blk:HEAD_DIM + last_blk + 1].astype(F32)
            s = _dot_nt(qgs[g], kn) + bias_col
            s = jnp.where(past_len + newcol <= tpos4, s, NEG)
            m, l, acc = _online_step((m_ref[g], l_ref[g], acc_ref[g]), s, vn)
            o_s = acc / l
            kw = jnp.concatenate([swin_ref[0, :, g, :], newwin_ref[:, g, :]], axis=0)
            vw = jnp.concatenate([swin_ref[0, :, NSA_KV_HEADS + g, :], newwin_ref[:, NSA_KV_HEADS + g, :]], axis=0)
            o_w = _window_branch(qgs[g], kw, vw, kpos, tpos4)
            o_g = (_gate_col(gates, g, 0, tq) * oc_ref[g] + _gate_col(gates, g, 1, tq) * o_s
                   + _gate_col(gates, g, 2, tq) * o_w)
            for j in range(NSA_GROUP):
                hd = NSA_GROUP * g + j
                oacc_ref[:, hd * HEAD_DIM:(hd + 1) * HEAD_DIM] = o_g[j * tq:(j + 1) * tq]
        _finish(o_ref, oacc_ref, nz_ref[...], og_ref[...])


def _nsa_sample(cache, page_table, qn, gates, proj, kc, vc, rows_s, state_win, win_s, nsa_g, row0, t_new):
    nb, n_pages = page_table.shape
    past_len = n_pages * PAGE
    n_sub = kc.shape[2]
    n_sel = past_len // SEL_BLOCK + 1
    nsp = -(-n_sel // 128) * 128
    n_tau = n_pages // PAGES_PER_TILE
    rb = row0 // t_new
    n_buf = state_win.shape[1]
    rows = NSA_GROUP * t_new
    cg = 2 * NSA_KV_HEADS

    def page_spec(r):
        return pl.BlockSpec((1, PAGE, cg, HEAD_DIM),
                            lambda b, i, pt: (pt[b, i * PAGES_PER_TILE + r], 0, 1, 0))

    grid_spec = pltpu.PrefetchScalarGridSpec(
        num_scalar_prefetch=1,
        grid=(nb, n_tau),
        in_specs=[page_spec(r) for r in range(PAGES_PER_TILE)] + [
            pl.BlockSpec((t_new, NSA_WIDTH), lambda b, i, pt: (b, 0)),
            pl.BlockSpec((t_new, 128), lambda b, i, pt: (b, 0)),
            pl.BlockSpec((t_new, NSA_WIDTH), lambda b, i, pt: (rb + b, COL_NZ // 2048)),
            pl.BlockSpec((1, NSA_KV_HEADS, n_sub, HEAD_DIM), lambda b, i, pt: (b, 0, 0, 0)),
            pl.BlockSpec((1, NSA_KV_HEADS, n_sub, HEAD_DIM), lambda b, i, pt: (b, 0, 0, 0)),
            pl.BlockSpec((t_new, cg, HEAD_DIM), lambda b, i, pt: (b, 1, 0)),
            pl.BlockSpec((1, n_buf, cg, HEAD_DIM), lambda b, i, pt: (b, 0, 0, 0)),
            pl.BlockSpec((t_new, cg, HEAD_DIM), lambda b, i, pt: (b, 0, 0)),
            pl.BlockSpec((1, NSA_WIDTH), lambda b, i, pt: (0, 0))],
        out_specs=pl.BlockSpec((t_new, NSA_WIDTH), lambda b, i, pt: (b, 0)),
        scratch_shapes=[pltpu.VMEM((NSA_KV_HEADS, rows, HEAD_DIM + nsp), BF16),
                        pltpu.VMEM((NSA_KV_HEADS, rows, HEAD_DIM), F32),
                        pltpu.VMEM((NSA_KV_HEADS, rows, 1), F32),
                        pltpu.VMEM((NSA_KV_HEADS, rows, 1), F32),
                        pltpu.VMEM((NSA_KV_HEADS, rows, HEAD_DIM), F32),
                        pltpu.VMEM((t_new, NSA_WIDTH), F32)],
    )
    return pl.pallas_call(
        functools.partial(_nsa_sample_kernel, n_sub=n_sub, nsp=nsp, past_len=past_len, t_new=t_new),
        grid_spec=grid_spec,
        out_shape=jax.ShapeDtypeStruct((nb * t_new, NSA_WIDTH), F32),
        compiler_params=_cparams(("arbitrary", "arbitrary")),
        name="nsa_sample",
    )(page_table, *([cache] * PAGES_PER_TILE), qn, gates, proj, kc, vc, rows_s, state_win, win_s,
      nsa_g.reshape(1, NSA_WIDTH))


def _out_kernel(oh_ref, on_ref, wh_ref, wn_ref, x_ref, gate_ref, y_ref):
    acc = _dot(oh_ref[...], wh_ref[...]) + _dot(on_ref[...], wn_ref[...])
    y_ref[...] = x_ref[...] + gate_ref[...] * acc


def _out_proj(o_h, o_n, w_out, x, gate, tm, tn=512):
    m, d = x.shape
    half = o_h.shape[1]
    grow = gate.shape[0]
    gmap = (lambda i, j: (0, j)) if grow == 1 else (lambda i, j: (i, j))
    return pl.pallas_call(
        _out_kernel,
        grid=(m // tm, d // tn),
        in_specs=[pl.BlockSpec((tm, half), lambda i, j: (i, 0)),
                  pl.BlockSpec((tm, half), lambda i, j: (i, 0)),
                  pl.BlockSpec((half, tn), lambda i, j: (0, j)),
                  pl.BlockSpec((half, tn), lambda i, j: (1, j)),
                  pl.BlockSpec((tm, tn), lambda i, j: (i, j)),
                  pl.BlockSpec((1 if grow == 1 else tm, tn), gmap)],
        out_specs=pl.BlockSpec((tm, tn), lambda i, j: (i, j)),
        out_shape=jax.ShapeDtypeStruct((m, d), F32),
        compiler_params=_cparams(("arbitrary", "arbitrary")),
        name="out_proj",
    )(o_h, o_n, w_out, w_out, x, gate)


def _layer(xp, xs, c_all, cache, s_win, s_hgrn, page_table, lb_param, layer, norm_g, w_ada, b_ada, w_in,
           hgrn_out_g, q_g, k_g, pe, w1, w2, nsa_out_g, w_out):
    t_len = xp.shape[0]
    nb, t_new, _ = xs.shape
    n_s = nb * t_new
    d = D_MODEL

    mod = _ada(c_all, w_ada, b_ada)
    shift, scale, gate = mod[:, :d], mod[:, d:2 * d], mod[:, 2 * d:]
    rep = lambda a: jnp.pad(jnp.repeat(a[1:1 + nb], t_new, axis=0), ((0, SAMPLE_ROWS - n_s), (0, 0)))
    xs_pad = jnp.pad(xs.reshape(n_s, d), ((0, SAMPLE_ROWS - n_s), (0, 0)))
    h_all = _norm(xp, xs_pad, norm_g.reshape(1, d), scale[0:1], shift[0:1], rep(scale), rep(shift))

    n_main = (_N_MAIN_A + _N_WIN) * PROJ_TN
    w_main = w_in[:, :n_main].astype(BF16)
    n_ng = 3 * NSA_HEADS
    w_tail = jnp.concatenate(
        [w_in[:, n_main + n_ng:], w_in[:, n_main:n_main + n_ng],
         jnp.zeros((d, PROJ_TN - n_ng), w_in.dtype)], axis=1).astype(BF16)
    proj = _matmul(h_all, w_main, w_tail)

    oh_p, st_p = _hgrn(proj, lb_param, hgrn_out_g, None, row0=0, n_batch=1, t_len=t_len,
                       chunk=min(64, t_len), out_dtype=BF16, layer=layer)
    oh_s, st_s = _hgrn(proj, lb_param, hgrn_out_g, s_hgrn, row0=t_len, n_batch=nb, t_len=t_new,
                       chunk=t_new, out_dtype=F32, layer=layer)

    qn_p, rows_p, win_p3, selb_p, winb_p, gates_p = _prep(proj, q_g, k_g, 0, t_len)
    qn_s, rows_s, win_s3, _, _, gates_s = _prep(proj, q_g, k_g, t_len, SAMPLE_ROWS)

    w1r = w1.reshape(2, 2, CMP_STRIDE // 2, 2 * HEAD_DIM, CMP_HIDDEN)
    wc = jnp.concatenate([w1r[:, 0], w1r[:, 1]], axis=-1).astype(BF16)
    w1f = w1.reshape(2, CMP_LEN * HEAD_DIM, CMP_HIDDEN).astype(BF16)
    pe_flat = pe.reshape(2, 1, CMP_LEN * HEAD_DIM)
    w2b = w2.astype(BF16)

    n_pp = t_len // PAGE
    kc_p, vc_p = _compress(rows_p.reshape(n_pp, PAGE, 4 * NSA_KV_HEADS, HEAD_DIM),
                           jnp.arange(n_pp, dtype=jnp.int32).reshape(1, n_pp), wc, w1f, pe_flat, w2b, k_g[0])
    cache4 = cache.reshape(cache.shape[0], PAGE, 4 * NSA_KV_HEADS, HEAD_DIM)
    kc_s, vc_s = _compress(cache4, page_table, wc, w1f, pe_flat, w2b, k_g[0])

    on_p = _nsa_prompt(qn_p, gates_p, proj, kc_p, vc_p, selb_p, winb_p, nsa_out_g, t_len)
    n_buf = s_win.shape[1]
    s_win4 = s_win.reshape(nb, n_buf, 2 * NSA_KV_HEADS, HEAD_DIM)
    on_s = _nsa_sample(cache4, page_table, qn_s, gates_s, proj, kc_s, vc_s, rows_s, s_win4, win_s3,
                       nsa_out_g, t_len, t_new)

    w_out_b = w_out.astype(BF16)
    y_p = _out_proj(oh_p, on_p, w_out_b, xp, gate[0:1], tm=min(1024, t_len))
    y_s = _out_proj(oh_s, on_s, w_out_b, xs.reshape(n_s, d), jnp.repeat(gate[1:1 + nb], t_new, axis=0), tm=n_s)

    n_w = min(WINDOW, t_len)
    kv_p = rows_p.reshape(1, t_len, 4, NSA_KV_HEADS, HEAD_DIM)
    kv_s = rows_s[:n_s].reshape(nb, t_new, 4, NSA_KV_HEADS, HEAD_DIM)
    win_p = win_p3[t_len - n_w:].reshape(1, n_w, 2, NSA_KV_HEADS, HEAD_DIM)
    win_new = win_s3[:n_s].reshape(nb, t_new, 2, NSA_KV_HEADS, HEAD_DIM)
    win_s = jnp.concatenate([s_win, win_new], axis=1)[:, -n_buf:]
    return y_p, y_s.reshape(nb, t_new, d), kv_p, kv_s, win_p, win_s, st_p, st_s


def kernel(x_prompt, x_sample, cache_kv, state_win, state_hgrn, page_table, c_prompt, c_sample, norm_g, w_ada,
           b_ada, w_in, hgrn_lb, hgrn_out_g, q_norm_g, k_norm_g, cmp_pe, cmp_w1, cmp_w2, nsa_out_g, w_out):
    depth = w_in.shape[0]
    assert depth == 1 and x_prompt.shape[0] == 1
    nb = x_sample.shape[0]
    c_all = jnp.concatenate([c_prompt, c_sample], axis=0)
    c_all = jnp.pad(c_all, ((0, (-c_all.shape[0]) % 8), (0, 0)))
    l = 0
    outs = _layer(x_prompt[0], x_sample, c_all, cache_kv[l], state_win[l], state_hgrn[l], page_table,
                  hgrn_lb, l, norm_g[l], w_ada[l], b_ada[l], w_in[l], hgrn_out_g[l], q_norm_g[l], k_norm_g[l],
                  cmp_pe[l], cmp_w1[l], cmp_w2[l], nsa_out_g[l], w_out[l])
    y_p, y_s, kv_p, kv_s, win_p, win_s, st_p, st_s = outs
    return (y_p[None], y_s, kv_p[None], kv_s[None], win_p[None], win_s[None], st_p[None], st_s[None])
```

```python
import functools

import jax
import jax.numpy as jnp
from jax import lax
from jax.experimental import pallas as pl
from jax.experimental.pallas import tpu as pltpu

F32 = jnp.float32
BF16 = jnp.bfloat16

D_MODEL = 4096
HEAD_DIM = 128
HGRN_WIDTH = 2048
NSA_WIDTH = 2048
HGRN_HEADS = 16
NSA_HEADS = 16
NSA_KV_HEADS = 4
NSA_GROUP = 4
KV_WIDTH = 512
CMP_LEN = 32
CMP_STRIDE = 16
CMP_HIDDEN = 256
SEL_BLOCK = 64
SEL_TOP = 16
WINDOW = 512
Q_BLOCK = 128
NORM_EPS = 1e-6
PAGE = 128

COL_HQ, COL_HF, COL_HI, COL_HZ, COL_NQ, COL_ROWS, COL_NZ = (i * 2048 for i in range(7))
COL_WIN = 14336
COL_NG = 15360
N_PROJ = 15872
SAMPLE_ROWS = 256
ROW_TILE = 256
PAGES_PER_TILE = 8
KV_TILE = 512
NEG = -1e30
LOG2E = 1.4426950408889634
VMEM_LIMIT = 56 * 1024 * 1024


def _cparams(sem):
    return pltpu.CompilerParams(dimension_semantics=sem, vmem_limit_bytes=VMEM_LIMIT)


def _dot(a, b):
    return jnp.dot(a.astype(BF16), b.astype(BF16), preferred_element_type=F32)


def _dot_nt(a, b):
    return lax.dot_general(a.astype(BF16), b.astype(BF16), (((1,), (1,)), ((), ())),
                           preferred_element_type=F32)


def _dot_tn(a, b):
    return lax.dot_general(a.astype(BF16), b.astype(BF16), (((0,), (0,)), ((), ())),
                           preferred_element_type=F32)


def _silu(x):
    return x * jax.nn.sigmoid(x)


def _ada_kernel(c_ref, w_ref, b_ref, o_ref):
    c = c_ref[...]
    o_ref[...] = _dot(_silu(c), w_ref[...]) + b_ref[...]


def _ada(c_all, w_ada, b_ada):
    m, d = c_all.shape
    n = w_ada.shape[1]
    tn = 512
    return pl.pallas_call(
        _ada_kernel,
        grid=(n // tn,),
        in_specs=[pl.BlockSpec((m, d), lambda j: (0, 0)),
                  pl.BlockSpec((d, tn), lambda j: (0, j)),
                  pl.BlockSpec((1, tn), lambda j: (0, j))],
        out_specs=pl.BlockSpec((m, tn), lambda j: (0, j)),
        out_shape=jax.ShapeDtypeStruct((m, n), F32),
        compiler_params=_cparams(("arbitrary",)),
        name="ada",
    )(c_all, w_ada, b_ada.reshape(1, n))


def _norm_kernel(xp_ref, xs_ref, g_ref, scp_ref, shp_ref, scs_ref, shs_ref, o_ref, *, n_prompt):
    i = pl.program_id(0)

    def body(x, scale, shift):
        r = x * lax.rsqrt(jnp.mean(x * x, axis=-1, keepdims=True) + NORM_EPS) * g_ref[...]
        o_ref[...] = (r * (1.0 + scale) + shift).astype(o_ref.dtype)

    @pl.when(i < n_prompt)
    def _():
        body(xp_ref[...], scp_ref[...], shp_ref[...])

    @pl.when(i >= n_prompt)
    def _():
        body(xs_ref[...], scs_ref[...], shs_ref[...])


def _norm(xp, xs, g, sc_p, sh_p, sc_s, sh_s):
    t, d = xp.shape
    n_prompt = t // ROW_TILE
    row = lambda i: (0, 0)
    return pl.pallas_call(
        functools.partial(_norm_kernel, n_prompt=n_prompt),
        grid=(n_prompt + 1,),
        in_specs=[pl.BlockSpec((ROW_TILE, d), lambda i: (jnp.minimum(i, n_prompt - 1), 0)),
                  pl.BlockSpec((SAMPLE_ROWS, d), row),
                  pl.BlockSpec((1, d), row), pl.BlockSpec((1, d), row), pl.BlockSpec((1, d), row),
                  pl.BlockSpec((SAMPLE_ROWS, d), row), pl.BlockSpec((SAMPLE_ROWS, d), row)],
        out_specs=pl.BlockSpec((ROW_TILE, d), lambda i: (i, 0)),
        out_shape=jax.ShapeDtypeStruct((t + SAMPLE_ROWS, d), BF16),
        compiler_params=_cparams(("arbitrary",)),
        name="norm",
    )(xp, xs, g, sc_p, sh_p, sc_s, sh_s)


PROJ_TN = 512
_N_MAIN_A, _N_NZ, _N_WIN = 24, 4, 2


def _proj_main_tile(j):
    return jnp.where(j < _N_MAIN_A, j,
                     jnp.where(j < _N_MAIN_A + _N_NZ, _N_MAIN_A - 1,
                               jnp.where(j < _N_MAIN_A + _N_NZ + _N_WIN, j - _N_NZ, _N_MAIN_A + _N_WIN - 1)))


def _proj_tail_tile(j):
    return jnp.where(j < _N_MAIN_A, 0,
                     jnp.where(j < _N_MAIN_A + _N_NZ, j - _N_MAIN_A,
                               jnp.where(j < _N_MAIN_A + _N_NZ + _N_WIN, _N_NZ - 1, _N_NZ)))


def _mm_kernel(a_ref, wm_ref, wt_ref, o_ref, wb_ref):
    j = pl.program_id(0)
    i = pl.program_id(1)
    tail = ((j >= _N_MAIN_A) & (j < _N_MAIN_A + _N_NZ)) | (j == _N_MAIN_A + _N_NZ + _N_WIN)

    @pl.when((i == 0) & tail)
    def _():
        wb_ref[...] = wt_ref[...]

    @pl.when((i == 0) & jnp.logical_not(tail))
    def _():
        wb_ref[...] = wm_ref[...].astype(BF16)

    o_ref[...] = jnp.dot(a_ref[...], wb_ref[...], preferred_element_type=F32)


def _row_tile(m, cap=1100):
    best = 16
    for tm in range(16, cap + 1, 16):
        if m % tm == 0:
            best = tm
    return best


def _matmul(a, w_main, w_tail):
    m, k = a.shape
    tn = PROJ_TN
    tm = _row_tile(m)
    return pl.pallas_call(
        _mm_kernel,
        grid=(N_PROJ // tn, m // tm),
        in_specs=[pl.BlockSpec((tm, k), lambda j, i: (i, 0)),
                  pl.BlockSpec((k, tn), lambda j, i: (0, _proj_main_tile(j))),
                  pl.BlockSpec((k, tn), lambda j, i: (0, _proj_tail_tile(j)))],
        out_specs=pl.BlockSpec((tm, tn), lambda j, i: (i, j)),
        out_shape=jax.ShapeDtypeStruct((m, N_PROJ), F32),
        scratch_shapes=[pltpu.VMEM((k, tn), BF16)],
        compiler_params=_cparams(("arbitrary", "arbitrary")),
        name="proj",
    )(a, w_main, w_tail)


def _hgrn_kernel(*refs, chunk, levels, has_s0, layer):
    if has_s0:
        lbp_ref, og_ref, hq_ref, hf_ref, hi_ref, hz_ref, s0_ref, o_ref, sout_ref, st_ref = refs
    else:
        lbp_ref, og_ref, hq_ref, hf_ref, hi_ref, hz_ref, o_ref, sout_ref, st_ref = refs
        s0_ref = None
    C = chunk
    c = pl.program_id(1)
    n_chunks = pl.num_programs(1)

    @pl.when(c == 0)
    def _():
        for h in range(HGRN_HEADS):
            if has_s0:
                st_ref[h] = s0_ref[0, h].T
            else:
                st_ref[h] = jnp.zeros((HEAD_DIM, HEAD_DIM), F32)

    lbp = lbp_ref[...]
    e = jnp.exp(lbp - jnp.max(lbp, axis=0, keepdims=True))
    lb = jnp.sum(e[:layer + 1], axis=0, keepdims=True) / jnp.sum(e, axis=0, keepdims=True)

    f = lb + (1.0 - lb) * jax.nn.sigmoid(hf_ref[...])
    logf = jnp.log(f)
    k = 1.0 - f
    q = hq_ref[...]
    v = hi_ref[...]
    hz = hz_ref[...]
    n = q.shape[1]

    row = lax.broadcasted_iota(jnp.int32, (C, 1), 0)
    cum = logf
    s = 1
    while s < C:
        cum = cum + jnp.where(row >= s, pltpu.roll(cum, s, axis=0), 0.0)
        s *= 2
    last = cum[C - 1:C, :]
    qe = q * jnp.exp(cum)
    kd = k * jnp.exp(last - cum)
    elast = jnp.exp(last)

    lvl = []
    for h in levels:
        cum3 = cum.reshape(C // (2 * h), 2 * h, n)
        cmid = jnp.broadcast_to(cum3[:, h - 1:h, :], cum3.shape).reshape(C, n)
        right = ((row // h) % 2) == 1
        ql = jnp.where(right, q * jnp.exp(jnp.where(right, cum - cmid, 0.0)), 0.0)
        kl = jnp.where(right, 0.0, k * jnp.exp(jnp.where(right, 0.0, cmid - cum)))
        ti = lax.broadcasted_iota(jnp.int32, (C, C), 0)
        si = lax.broadcasted_iota(jnp.int32, (C, C), 1)
        mask = ((ti // (2 * h)) == (si // (2 * h))) & (((ti // h) % 2) == 1) & (((si // h) % 2) == 0)
        lvl.append((ql, kl, mask))

    nb = C // 8
    rin = lax.broadcasted_iota(jnp.int32, (nb, 8, 1), 1)
    og = og_ref[...]
    for h in range(HGRN_HEADS):
        sl = slice(h * HEAD_DIM, (h + 1) * HEAD_DIM)
        st = st_ref[h]
        q_h, k_h, v_h, cum_h = q[:, sl], k[:, sl], v[:, sl], cum[:, sl]
        o_h = _dot_nt(qe[:, sl], st)
        if levels:
            a = jnp.zeros((C, C), F32)
            for ql, kl, mask in lvl:
                a = a + jnp.where(mask, _dot_nt(ql[:, sl], kl[:, sl]), 0.0)
            o_h = o_h + _dot(a, v_h)
        q3 = q_h.reshape(nb, 8, HEAD_DIM)
        k3 = k_h.reshape(nb, 8, HEAD_DIM)
        v3 = v_h.reshape(nb, 8, HEAD_DIM)
        c3 = cum_h.reshape(nb, 8, HEAD_DIM)
        od = jnp.zeros((nb, 8, HEAD_DIM), F32)
        for s in range(8):
            dec = jnp.exp(jnp.where(rin >= s, c3 - c3[:, s:s + 1, :], -jnp.inf))
            w = q3 * k3[:, s:s + 1, :] * dec
            od = od + jnp.sum(w, axis=-1, keepdims=True) * v3[:, s:s + 1, :]
        o_h = o_h + od.reshape(C, HEAD_DIM)
        st_ref[h] = st * elast[:, sl] + _dot_tn(v_h, kd[:, sl])
        o_h = o_h * lax.rsqrt(jnp.mean(o_h * o_h, axis=-1, keepdims=True) + NORM_EPS) * og
        o_ref[:, sl] = (o_h * _silu(hz[:, sl])).astype(o_ref.dtype)

    @pl.when(c == n_chunks - 1)
    def _():
        for h in range(HGRN_HEADS):
            sout_ref[0, h] = st_ref[h].T


def _hgrn(proj, lb_param, out_g, s0, *, row0, n_batch, t_len, chunk, out_dtype, layer):
    n_chunks = t_len // chunk
    levels = []
    h = chunk // 2
    while h >= 8:
        levels.append(h)
        h //= 2
    blk0 = row0 // chunk
    rmap = lambda col: (lambda b, c: (blk0 + b * n_chunks + c, col))
    in_specs = [pl.BlockSpec(lb_param.shape, lambda b, c: (0, 0)),
                pl.BlockSpec((1, HEAD_DIM), lambda b, c: (0, 0)),
                pl.BlockSpec((chunk, HGRN_WIDTH), rmap(0)),
                pl.BlockSpec((chunk, HGRN_WIDTH), rmap(1)),
                pl.BlockSpec((chunk, HGRN_WIDTH), rmap(2)),
                pl.BlockSpec((chunk, HGRN_WIDTH), rmap(3))]
    args = [lb_param, out_g.reshape(1, HEAD_DIM), proj, proj, proj, proj]
    if s0 is not None:
        in_specs.append(pl.BlockSpec((1, HGRN_HEADS, HEAD_DIM, HEAD_DIM), lambda b, c: (b, 0, 0, 0)))
        args.append(s0)
    return pl.pallas_call(
        functools.partial(_hgrn_kernel, chunk=chunk, levels=tuple(levels), has_s0=s0 is not None, layer=layer),
        grid=(n_batch, n_chunks),
        in_specs=in_specs,
        out_specs=[pl.BlockSpec((chunk, HGRN_WIDTH), lambda b, c: (b * n_chunks + c, 0)),
                   pl.BlockSpec((1, HGRN_HEADS, HEAD_DIM, HEAD_DIM), lambda b, c: (b, 0, 0, 0))],
        out_shape=[jax.ShapeDtypeStruct((n_batch * t_len, HGRN_WIDTH), out_dtype),
                   jax.ShapeDtypeStruct((n_batch, HGRN_HEADS, HEAD_DIM, HEAD_DIM), F32)],
        scratch_shapes=[pltpu.VMEM((HGRN_HEADS, HEAD_DIM, HEAD_DIM), F32)],
        compiler_params=_cparams(("arbitrary", "arbitrary")),
        name="hgrn",
    )(*args)


def _head_rms(x, g, n_heads):
    outs = []
    for h in range(n_heads):
        xh = x[:, h * HEAD_DIM:(h + 1) * HEAD_DIM]
        outs.append(xh * lax.rsqrt(jnp.mean(xh * xh, axis=-1, keepdims=True) + NORM_EPS) * g)
    return jnp.concatenate(outs, axis=1)


def _prep_kernel(nq_ref, rows_ref, win_ref, ng_ref, qg_ref, kg_ref,
                 q_ref, rows_o, win_o, selk_o, selvt_o, wink_o, winvt_o, gate_o):
    kg = kg_ref[...]
    q_ref[...] = _head_rms(nq_ref[...], qg_ref[...], NSA_HEADS) * (HEAD_DIM ** -0.5 * LOG2E)
    rows = rows_ref[...]
    ks = _head_rms(rows[:, 2 * KV_WIDTH:3 * KV_WIDTH], kg[1:2], NSA_KV_HEADS)
    for cg in range(4 * NSA_KV_HEADS):
        sl = slice(cg * HEAD_DIM, (cg + 1) * HEAD_DIM)
        if 2 * NSA_KV_HEADS <= cg < 3 * NSA_KV_HEADS:
            rows_o[:, cg, :] = ks[:, (cg - 2 * NSA_KV_HEADS) * HEAD_DIM:(cg - 2 * NSA_KV_HEADS + 1) * HEAD_DIM]
        else:
            rows_o[:, cg, :] = rows[:, sl]
    selk_o[...] = ks.astype(BF16)
    selvt_o[...] = rows[:, 3 * KV_WIDTH:].T.astype(BF16)
    win = win_ref[...]
    kw = _head_rms(win[:, :KV_WIDTH], kg[2:3], NSA_KV_HEADS)
    for cg in range(2 * NSA_KV_HEADS):
        sl = slice(cg * HEAD_DIM, (cg + 1) * HEAD_DIM)
        win_o[:, cg, :] = kw[:, sl] if cg < NSA_KV_HEADS else win[:, sl]
    wink_o[...] = kw.astype(BF16)
    winvt_o[...] = win[:, KV_WIDTH:].T.astype(BF16)
    gate_o[...] = jax.nn.sigmoid(ng_ref[...])


def _prep(proj, q_g, k_g, row0, n_rows):
    t = min(ROW_TILE, n_rows)
    r0 = row0 // t
    return pl.pallas_call(
        _prep_kernel,
        grid=(n_rows // t,),
        in_specs=[pl.BlockSpec((t, NSA_WIDTH), lambda i: (r0 + i, COL_NQ // 2048)),
                  pl.BlockSpec((t, 2048), lambda i: (r0 + i, COL_ROWS // 2048)),
                  pl.BlockSpec((t, 1024), lambda i: (r0 + i, COL_WIN // 1024)),
                  pl.BlockSpec((t, 128), lambda i: (r0 + i, COL_NG // 128)),
                  pl.BlockSpec((1, HEAD_DIM), lambda i: (0, 0)),
                  pl.BlockSpec((3, HEAD_DIM), lambda i: (0, 0))],
        out_specs=[pl.BlockSpec((t, 2048), lambda i: (i, 0)),
                   pl.BlockSpec((t, 4 * NSA_KV_HEADS, HEAD_DIM), lambda i: (i, 0, 0)),
                   pl.BlockSpec((t, 2 * NSA_KV_HEADS, HEAD_DIM), lambda i: (i, 0, 0)),
                   pl.BlockSpec((t, KV_WIDTH), lambda i: (i, 0)),
                   pl.BlockSpec((KV_WIDTH, t), lambda i: (0, i)),
                   pl.BlockSpec((t, KV_WIDTH), lambda i: (i, 0)),
                   pl.BlockSpec((KV_WIDTH, t), lambda i: (0, i)),
                   pl.BlockSpec((t, 128), lambda i: (i, 0))],
        out_shape=[jax.ShapeDtypeStruct((n_rows, 2048), F32),
                   jax.ShapeDtypeStruct((n_rows, 4 * NSA_KV_HEADS, HEAD_DIM), F32),
                   jax.ShapeDtypeStruct((n_rows, 2 * NSA_KV_HEADS, HEAD_DIM), F32),
                   jax.ShapeDtypeStruct((n_rows, KV_WIDTH), BF16),
                   jax.ShapeDtypeStruct((KV_WIDTH, n_rows), BF16),
                   jax.ShapeDtypeStruct((n_rows, KV_WIDTH), BF16),
                   jax.ShapeDtypeStruct((KV_WIDTH, n_rows), BF16),
                   jax.ShapeDtypeStruct((n_rows, 128), F32)],
        compiler_params=_cparams(("arbitrary",)),
        name="prep",
    )(proj, proj, proj, proj, q_g.reshape(1, HEAD_DIM), k_g)


def _compress_kernel(pt_ref, *refs):
    pages = refs[:PAGES_PER_TILE]
    wc_ref, w1f_ref, pe_ref, w2_ref, kcg_ref, kc_ref, vc_ref, cvec_ref, carry_ref, out_scr = refs[PAGES_PER_TILE:]
    b = pl.program_id(0)
    i = pl.program_id(1)
    nsub = PAGE // CMP_STRIDE
    ntile = nsub * PAGES_PER_TILE
    G = NSA_KV_HEADS
    rows = ntile * G

    @pl.when((b == 0) & (i == 0))
    def _():
        for ch in range(2):
            pe = jnp.broadcast_to(pe_ref[ch], (8, CMP_LEN * HEAD_DIM))
            cvec_ref[ch] = _dot(pe, w1f_ref[ch])

    @pl.when(i == 0)
    def _():
        carry_ref[...] = jnp.zeros(carry_ref.shape, F32)

    def gathered(p):
        lo, hi = [], []
        for r in range(PAGES_PER_TILE):
            a = pages[r][0, pl.ds(p, nsub // 2, stride=2 * CMP_STRIDE), :, :]
            bb = pages[r][0, pl.ds(p + CMP_STRIDE, nsub // 2, stride=2 * CMP_STRIDE), :, :]
            lo.append(jnp.concatenate([a[:, :G], bb[:, :G]], axis=1).reshape(nsub * G, HEAD_DIM))
            hi.append(jnp.concatenate([a[:, G:], bb[:, G:]], axis=1).reshape(nsub * G, HEAD_DIM))
        return jnp.concatenate(lo, axis=0).astype(BF16), jnp.concatenate(hi, axis=0).astype(BF16)

    u = [jnp.zeros((rows, 2 * CMP_HIDDEN), F32) for _ in range(2)]
    for pp in range(CMP_STRIDE // 2):
        xa = gathered(2 * pp)
        xb = gathered(2 * pp + 1)
        for ch in range(2):
            lhs = jnp.concatenate([xa[ch], xb[ch]], axis=1)
            u[ch] = u[ch] + jnp.dot(lhs, wc_ref[ch, pp], preferred_element_type=F32)

    row8 = lax.broadcasted_iota(jnp.int32, (8, 1), 0)
    for ch in range(2):
        u0, u1 = u[ch][:, :CMP_HIDDEN], u[ch][:, CMP_HIDDEN:]
        rolled = pltpu.roll(u0, G, axis=0)
        head = jnp.where(row8 < G, carry_ref[ch], rolled[0:8])
        carry_ref[ch] = rolled[0:8]
        prev = jnp.concatenate([head, rolled[8:]], axis=0)
        pre = prev + u1 + cvec_ref[ch][0:1, :]
        out = _dot(_silu(pre), w2_ref[ch])
        if ch == 0:
            out = out * lax.rsqrt(jnp.mean(out * out, axis=-1, keepdims=True) + NORM_EPS) * kcg_ref[...]
        out_scr[...] = out
        dst = kc_ref if ch == 0 else vc_ref
        for g in range(G):
            dst[0, g] = out_scr[pl.ds(g, ntile, stride=G), :].astype(dst.dtype)


def _compress(src, page_table, wc, w1f, pe_flat, w2, kc_g):
    nb, n_pages = page_table.shape
    n_tiles = n_pages // PAGES_PER_TILE
    nsub = PAGE // CMP_STRIDE
    ntile = nsub * PAGES_PER_TILE

    def page_spec(r):
        return pl.BlockSpec((1, PAGE, 2 * NSA_KV_HEADS, HEAD_DIM),
                            lambda b, i, pt: (pt[b, i * PAGES_PER_TILE + r], 0, 0, 0))

    const = lambda *shape: pl.BlockSpec(shape, lambda b, i, pt: (0,) * len(shape))
    out_spec = pl.BlockSpec((1, NSA_KV_HEADS, ntile, HEAD_DIM), lambda b, i, pt: (b, 0, i, 0))
    grid_spec = pltpu.PrefetchScalarGridSpec(
        num_scalar_prefetch=1,
        grid=(nb, n_tiles),
        in_specs=[page_spec(r) for r in range(PAGES_PER_TILE)] + [
            const(2, CMP_STRIDE // 2, 2 * HEAD_DIM, 2 * CMP_HIDDEN),
            const(2, CMP_LEN * HEAD_DIM, CMP_HIDDEN),
            const(2, 1, CMP_LEN * HEAD_DIM),
            const(2, CMP_HIDDEN, HEAD_DIM),
            const(1, HEAD_DIM)],
        out_specs=[out_spec, out_spec],
        scratch_shapes=[pltpu.VMEM((2, 8, CMP_HIDDEN), F32),
                        pltpu.VMEM((2, 8, CMP_HIDDEN), F32),
                        pltpu.VMEM((NSA_KV_HEADS * ntile, HEAD_DIM), F32)],
    )
    shape = jax.ShapeDtypeStruct((nb, NSA_KV_HEADS, n_tiles * ntile, HEAD_DIM), BF16)
    return pl.pallas_call(
        _compress_kernel,
        grid_spec=grid_spec,
        out_shape=[shape, shape],
        compiler_params=_cparams(("arbitrary", "arbitrary")),
        name="compress",
    )(page_table, *([src] * PAGES_PER_TILE), wc, w1f, pe_flat, w2, kc_g.reshape(1, HEAD_DIM))


def _split3(x):
    hi = x.astype(BF16)
    r = x - hi.astype(F32)
    mid = r.astype(BF16)
    lo = (r - mid.astype(F32)).astype(BF16)
    return hi, mid, lo


def _cmp_branch(qg, kc, vc, tpos4, n_sub):
    s = _dot_nt(qg, kc)
    npr = lax.broadcasted_iota(jnp.int32, (1, n_sub), 1)
    valid = (npr >= 1) & (npr * CMP_STRIDE + (CMP_STRIDE - 1) <= tpos4)
    s = jnp.where(valid, s, NEG)
    m = jnp.max(s, axis=-1, keepdims=True)
    p = jnp.where(valid, jnp.exp2(s - m), 0.0)
    l = jnp.sum(p, axis=-1, keepdims=True)
    p = p * jnp.where(l > 0.0, 1.0 / l, 0.0)
    return _dot(p, vc), p


def _softmax_cols(s, mask):
    s = jnp.where(mask, s, NEG)
    m = jnp.max(s, axis=0, keepdims=True)
    p = jnp.where(mask, jnp.exp2(s - m), 0.0)
    l = jnp.sum(p, axis=0, keepdims=True)
    return p * jnp.where(l > 0.0, 1.0 / l, 0.0)


def _select_t(imp, tpos, nsp):
    sb = lax.broadcasted_iota(jnp.int32, (nsp, 1), 0)
    cur = tpos // SEL_BLOCK
    causal = sb * SEL_BLOCK <= tpos
    forced = (sb == 0) | (sb == cur) | (sb == cur - 1)
    score = jnp.where(causal, jnp.where(forced, -NEG, imp), NEG)
    sel = jnp.zeros(score.shape, jnp.bool_)
    sbf = sb.astype(F32)
    for _ in range(SEL_TOP):
        m = jnp.max(score, axis=0, keepdims=True)
        idx = jnp.min(jnp.where(score == m, sbf, float(nsp)), axis=0, keepdims=True)
        hit = sbf == idx
        sel = sel | hit
        score = jnp.where(hit, NEG, score)
    return sel & causal


def _importance(psum, n_sub, nsp):
    npr = lax.broadcasted_iota(jnp.int32, (n_sub, nsp), 0)
    sb = lax.broadcasted_iota(jnp.int32, (n_sub, nsp), 1)
    r = SEL_BLOCK // CMP_STRIDE
    ov = ((npr >= r * sb) & (npr <= r * sb + r) & (npr >= 1)).astype(BF16)
    hi, mid, lo = _split3(psum)
    dot = lambda a: jnp.dot(a, ov, preferred_element_type=F32)
    return dot(hi) + dot(mid) + dot(lo)


def _select(imp, tpos, nsp):
    sb = lax.broadcasted_iota(jnp.int32, (1, nsp), 1)
    cur = tpos // SEL_BLOCK
    causal = sb * SEL_BLOCK <= tpos
    forced = (sb == 0) | (sb == cur) | (sb == cur - 1)
    score = jnp.where(causal, jnp.where(forced, -NEG, imp), NEG)
    sel = jnp.zeros(score.shape, jnp.bool_)
    sbf = sb.astype(F32)
    for _ in range(SEL_TOP):
        m = jnp.max(score, axis=-1, keepdims=True)
        idx = jnp.min(jnp.where(score == m, sbf, float(nsp)), axis=-1, keepdims=True)
        hit = sbf == idx
        sel = sel | hit
        score = jnp.where(hit, NEG, score)
    return sel & causal


def _online_step(carry, s, vt):
    m, l, acc = carry
    m_new = jnp.maximum(m, jnp.max(s, axis=-1, keepdims=True))
    alpha = jnp.exp2(m - m_new)
    p = jnp.exp2(s - m_new)
    l = alpha * l + jnp.sum(p, axis=-1, keepdims=True)
    acc = alpha * acc + _dot(p, vt)
    return m_new, l, acc


def _block_onehot(key0, n_keys, nsp):
    shift = SEL_BLOCK.bit_length() - 1
    kb = jnp.right_shift(lax.broadcasted_iota(jnp.int32, (n_keys, 1), 0), shift)
    sb = lax.broadcasted_iota(jnp.int32, (1, nsp), 1) - jnp.right_shift(key0, shift)
    return (kb == sb).astype(BF16)


def _window_branch(qg, kw, vw, kpos, tpos4):
    s = _dot_nt(qg, kw)
    mask = (kpos <= tpos4) & (kpos >= tpos4 - WINDOW) & (kpos >= 0)
    s = jnp.where(mask, s, NEG)
    m = jnp.max(s, axis=-1, keepdims=True)
    p = jnp.where(mask, jnp.exp2(s - m), 0.0)
    l = jnp.sum(p, axis=-1, keepdims=True)
    return _dot(p, vw) * jnp.where(l > 0.0, 1.0 / l, 0.0)


def _gate_col(gates, g, br, tq):
    return jnp.concatenate(
        [gates[:, 3 * (NSA_GROUP * g + j) + br:3 * (NSA_GROUP * g + j) + br + 1] for j in range(NSA_GROUP)], axis=0)


def _finish(o_ref, oacc_ref, nz, og):
    o = oacc_ref[...]
    o = o * lax.rsqrt(jnp.mean(o * o, axis=-1, keepdims=True) + NORM_EPS) * og
    o_ref[...] = (o * _silu(nz)).astype(o_ref.dtype)


def _nsa_prompt_kernel(q_ref, gate_ref, nz_ref, kc_ref, vc_ref, selk_ref, selvt_ref, *rest, n_sub, nsp):
    nwin = WINDOW // Q_BLOCK + 1
    wink_refs = rest[:nwin]
    winvt_refs = rest[nwin:2 * nwin]
    og_ref, o_ref, oacc_ref = rest[2 * nwin:]
    i = pl.program_id(0)
    tq = Q_BLOCK
    lanes = NSA_GROUP * tq
    t0 = i * tq
    tpos = t0 + lax.broadcasted_iota(jnp.int32, (1, tq), 1)
    tpos4 = jnp.concatenate([tpos] * NSA_GROUP, axis=1)
    q = q_ref[...]
    gates_t = gate_ref[...].T

    def gate_row(g, br):
        rows = [3 * (NSA_GROUP * g + j) + br for j in range(NSA_GROUP)]
        return jnp.concatenate([gates_t[r:r + 1, :] for r in rows], axis=1)

    npr = lax.broadcasted_iota(jnp.int32, (n_sub, 1), 0)
    cvalid = (npr >= 1) & (npr * CMP_STRIDE + (CMP_STRIDE - 1) <= tpos4)
    sb_o = lax.broadcasted_iota(jnp.int32, (nsp, n_sub), 0)
    np_o = lax.broadcasted_iota(jnp.int32, (nsp, n_sub), 1)
    r = SEL_BLOCK // CMP_STRIDE
    ovt = ((np_o >= r * sb_o) & (np_o <= r * sb_o + r) & (np_o >= 1)).astype(BF16)

    groups = range(NSA_KV_HEADS)
    qts = [jnp.concatenate([q[:, (NSA_GROUP * g + j) * HEAD_DIM:(NSA_GROUP * g + j + 1) * HEAD_DIM].T
                            for j in range(NSA_GROUP)], axis=1).astype(BF16) for g in groups]

    cs = [jnp.dot(kc_ref[0, g], qts[g], preferred_element_type=F32) for g in groups]
    cps = [_softmax_cols(s, cvalid) for s in cs]
    octs = [_dot_tn(vc_ref[0, g], cps[g]) for g in groups]
    imps = []
    for g in groups:
        psum = cps[g][:, 0:tq]
        for j in range(1, NSA_GROUP):
            psum = psum + cps[g][:, j * tq:(j + 1) * tq]
        imps.append(sum(jnp.dot(ovt, part, preferred_element_type=F32) for part in _split3(psum)))
    sel = _select_t(jnp.concatenate(imps, axis=1), tpos4, nsp)
    bias_t = jnp.where(sel, 0.0, NEG).astype(BF16)

    kpos = t0 - WINDOW + lax.broadcasted_iota(jnp.int32, (WINDOW + tq, 1), 0)
    wmask = (kpos <= tpos4) & (kpos >= tpos4 - WINDOW) & (kpos >= 0)
    ws = [jnp.dot(jnp.concatenate([w[:, g * HEAD_DIM:(g + 1) * HEAD_DIM] for w in wink_refs], axis=0), qts[g],
                  preferred_element_type=F32) for g in groups]
    wps = [_softmax_cols(s, wmask).astype(BF16) for s in ws]
    owts = [jnp.dot(jnp.concatenate([w[g * HEAD_DIM:(g + 1) * HEAD_DIM, :] for w in winvt_refs], axis=1), wps[g],
                    preferred_element_type=F32) for g in groups]
    n_full = t0 // KV_TILE
    krow = lax.broadcasted_iota(jnp.int32, (KV_TILE, 1), 0)
    qaugs = []
    for g in range(NSA_KV_HEADS):
        bg = bias_t[:, g * tq:(g + 1) * tq]
        qaugs.append(jnp.concatenate([qts[g], jnp.concatenate([bg] * NSA_GROUP, axis=1)], axis=0))

    def tile_step(c, carries, diagonal):
        k0 = pl.multiple_of(c * KV_TILE, KV_TILE)
        onehot = _block_onehot(k0, KV_TILE, nsp)
        scores = []
        for g in range(NSA_KV_HEADS):
            kt = selk_ref[pl.ds(k0, KV_TILE), pl.ds(g * HEAD_DIM, HEAD_DIM)]
            s = jnp.dot(jnp.concatenate([kt, onehot], axis=1), qaugs[g], preferred_element_type=F32)
            if diagonal:
                s = jnp.where(k0 + krow <= tpos4, s, NEG)
            scores.append(s)
        stats = []
        for g in range(NSA_KV_HEADS):
            m, l, _ = carries[g]
            m_new = jnp.maximum(m, jnp.max(scores[g], axis=0, keepdims=True))
            alpha = jnp.exp2(m - m_new)
            p = jnp.exp2(scores[g] - m_new)
            stats.append((m_new, alpha * l + jnp.sum(p, axis=0, keepdims=True), alpha, p.astype(BF16)))
        out = []
        for g in range(NSA_KV_HEADS):
            m_new, l, alpha, p = stats[g]
            vtt = selvt_ref[pl.ds(g * HEAD_DIM, HEAD_DIM), pl.ds(k0, KV_TILE)]
            out.append((m_new, l, alpha * carries[g][2] + jnp.dot(vtt, p, preferred_element_type=F32)))
        return tuple(out)

    init = tuple((jnp.full((1, lanes), NEG, F32), jnp.zeros((1, lanes), F32), jnp.zeros((HEAD_DIM, lanes), F32))
                 for _ in range(NSA_KV_HEADS))
    carries = lax.fori_loop(0, n_full, lambda c, cr: tile_step(c, cr, False), init)
    carries = tile_step(n_full, carries, True)

    for g in groups:
        m, l, acc = carries[g]
        o_s = acc * (1.0 / l)
        o_g = gate_row(g, 0) * octs[g] + gate_row(g, 1) * o_s + gate_row(g, 2) * owts[g]
        for j in range(NSA_GROUP):
            hd = NSA_GROUP * g + j
            oacc_ref[:, hd * HEAD_DIM:(hd + 1) * HEAD_DIM] = o_g[:, j * tq:(j + 1) * tq].T

    _finish(o_ref, oacc_ref, nz_ref[...], og_ref[...])


def _nsa_prompt(qn, gates, proj, kc, vc, selk, selvt, wink, winvt, nsa_g, t_len):
    n_sub = kc.shape[2]
    n_sel = t_len // SEL_BLOCK
    nsp = -(-n_sel // 128) * 128
    nq = t_len // Q_BLOCK
    nwin = WINDOW // Q_BLOCK + 1
    wblk = lambda r: (lambda i: jnp.maximum(i - (nwin - 1) + r, 0))
    win_specs = ([pl.BlockSpec((Q_BLOCK, KV_WIDTH), (lambda f: (lambda i: (f(i), 0)))(wblk(r))) for r in range(nwin)]
                 + [pl.BlockSpec((KV_WIDTH, Q_BLOCK), (lambda f: (lambda i: (0, f(i))))(wblk(r))) for r in range(nwin)])
    return pl.pallas_call(
        functools.partial(_nsa_prompt_kernel, n_sub=n_sub, nsp=nsp),
        grid=(nq,),
        in_specs=[pl.BlockSpec((Q_BLOCK, NSA_WIDTH), lambda i: (i, 0)),
                  pl.BlockSpec((Q_BLOCK, 128), lambda i: (i, 0)),
                  pl.BlockSpec((Q_BLOCK, NSA_WIDTH), lambda i: (i, COL_NZ // 2048)),
                  pl.BlockSpec((1, NSA_KV_HEADS, n_sub, HEAD_DIM), lambda i: (0, 0, 0, 0)),
                  pl.BlockSpec((1, NSA_KV_HEADS, n_sub, HEAD_DIM), lambda i: (0, 0, 0, 0)),
                  pl.BlockSpec((t_len, KV_WIDTH), lambda i: (0, 0), pipeline_mode=pl.Buffered(1)),
                  pl.BlockSpec((KV_WIDTH, t_len), lambda i: (0, 0), pipeline_mode=pl.Buffered(1))]
        + win_specs + [pl.BlockSpec((1, NSA_WIDTH), lambda i: (0, 0))],
        out_specs=pl.BlockSpec((Q_BLOCK, NSA_WIDTH), lambda i: (i, 0)),
        out_shape=jax.ShapeDtypeStruct((t_len, NSA_WIDTH), BF16),
        scratch_shapes=[pltpu.VMEM((Q_BLOCK, NSA_WIDTH), F32)],
        compiler_params=_cparams(("arbitrary",)),
        name="nsa_prompt",
    )(qn, gates, proj, kc, vc, selk, selvt, *([wink] * nwin), *([winvt] * nwin), nsa_g.reshape(1, NSA_WIDTH))


def _nsa_sample_kernel(pt_ref, *refs, n_sub, nsp, past_len, t_new):
    pages = refs[:PAGES_PER_TILE]
    (q_ref, gate_ref, nz_ref, kc_ref, vc_ref, newsel_ref, swin_ref, newwin_ref, og_ref,
     o_ref, qbd_ref, bias_ref, oc_ref, m_ref, l_ref, acc_ref, oacc_ref) = refs[PAGES_PER_TILE:]
    tau = pl.program_id(1)
    n_tau = pl.num_programs(1)
    tq = t_new
    G = NSA_KV_HEADS
    rows = NSA_GROUP * tq
    lanes = G * rows
    tpos = past_len + lax.broadcasted_iota(jnp.int32, (tq, 1), 0)
    tpos4 = jnp.concatenate([tpos] * NSA_GROUP, axis=0)
    lane_i = lax.broadcasted_iota(jnp.int32, (1, lanes), 1)

    def group_q():
        q = q_ref[...]
        return [jnp.concatenate([q[:, (NSA_GROUP * g + j) * HEAD_DIM:(NSA_GROUP * g + j + 1) * HEAD_DIM]
                                 for j in range(NSA_GROUP)], axis=0) for g in range(G)]

    def to_col(row_vec):
        eye = lax.broadcasted_iota(jnp.int32, (lanes, lanes), 0) == lax.broadcasted_iota(jnp.int32, (lanes, lanes), 1)
        return jnp.sum(jnp.where(eye, jnp.broadcast_to(row_vec, (lanes, lanes)), 0.0), axis=1, keepdims=True)

    def online(s, v_all, m, l, acc):
        m_new = jnp.maximum(m, jnp.max(s, axis=0, keepdims=True))
        alpha = jnp.exp2(m - m_new)
        p = jnp.exp2(s - m_new)
        l_new = alpha * l + jnp.sum(p, axis=0, keepdims=True)
        pv = _dot_tn(p, v_all)
        upd = jnp.concatenate([pv[g * rows:(g + 1) * rows, g * HEAD_DIM:(g + 1) * HEAD_DIM] for g in range(G)], axis=0)
        return m_new, l_new, to_col(alpha) * acc + upd

    @pl.when(tau == 0)
    def _():
        qgs = group_q()
        imps = []
        for g in range(G):
            oc, p = _cmp_branch(qgs[g].astype(BF16), kc_ref[0, g], vc_ref[0, g], tpos4, n_sub)
            oc_ref[g] = oc
            psum = p[0:tq]
            for j in range(1, NSA_GROUP):
                psum = psum + p[j * tq:(j + 1) * tq]
            imps.append(_importance(psum, n_sub, nsp))
        sel = _select(jnp.concatenate(imps, axis=0), tpos4, nsp)
        q_t = jnp.concatenate(qgs, axis=0).T
        qbd_ref[...] = jnp.concatenate(
            [jnp.where(lane_i // rows == g, q_t, 0.0) for g in range(G)], axis=0).astype(BF16)
        bias = jnp.where(sel, 0.0, NEG)
        bias = jnp.concatenate([bias[g * tq:(g + 1) * tq] for g in range(G) for _ in range(NSA_GROUP)], axis=0)
        for c in range(nsp // lanes):
            bias_ref[c * lanes:(c + 1) * lanes, :] = bias[:, c * lanes:(c + 1) * lanes].T
        m_ref[...] = jnp.full((1, lanes), NEG, F32)
        l_ref[...] = jnp.zeros((1, lanes), F32)
        acc_ref[...] = jnp.zeros((lanes, HEAD_DIM), F32)

    planes = [jnp.swapaxes(p[0], 0, 1) for p in pages]
    k_all = jnp.concatenate([jnp.concatenate([pl_[g] for g in range(G)], axis=1) for pl_ in planes], axis=0)
    v_all = jnp.concatenate([jnp.concatenate([pl_[G + g] for g in range(G)], axis=1) for pl_ in planes], axis=0)
    n_keys = PAGES_PER_TILE * PAGE
    n_blk = n_keys // SEL_BLOCK
    b_rows = bias_ref[pl.ds(pl.multiple_of(tau * n_blk, n_blk), n_blk), :]
    bias_keys = jnp.broadcast_to(b_rows[:, None, :], (n_blk, SEL_BLOCK, lanes)).reshape(n_keys, lanes)
    s = jnp.dot(k_all.astype(BF16), qbd_ref[...], preferred_element_type=F32) + bias_keys
    m, l, acc = online(s, v_all, m_ref[...], l_ref[...], acc_ref[...])
    m_ref[...] = m
    l_ref[...] = l
    acc_ref[...] = acc

    @pl.when(tau == n_tau - 1)
    def _():
        qgs = group_q()
        gates = gate_ref[...]
        n_buf = swin_ref.shape[1]
        kpos = past_len - n_buf + lax.broadcasted_iota(jnp.int32, (1, n_buf + tq), 1)
        last_blk = past_len // SEL_BLOCK
        pad = 16
        new = newsel_ref[...]
        zpad = jnp.zeros((pad - tq, G * HEAD_DIM), F32)
        kn = jnp.concatenate([jnp.concatenate([new[:, g, :] for g in range(G)], axis=1), zpad], axis=0)
        vn = jnp.concatenate([jnp.concatenate([new[:, G + g, :] for g in range(G)], axis=1), zpad], axis=0)
        krow = lax.broadcasted_iota(jnp.int32, (pad, 1), 0)
        s_new = jnp.dot(kn.astype(BF16), qbd_ref[...], preferred_element_type=F32) + bias_ref[last_blk:last_blk + 1, :]
        s_new = jnp.where((krow <= lane_i % tq) & (krow < tq), s_new, NEG)
        m, l, acc = online(s_new, vn, m_ref[...], l_ref[...], acc_ref[...])
        o_sel = acc * to_col(1.0 / l)
        swin = jnp.swapaxes(swin_ref[0], 0, 1)
        nwin = newwin_ref[...]
        for g in range(G):
            kw = jnp.concatenate([swin[g], nwin[:, g, :]], axis=0)
            vw = jnp.concatenate([swin[G + g], nwin[:, G + g, :]], axis=0)
            o_w = _window_branch(qgs[g], kw, vw, kpos, tpos4)
            o_g = (_gate_col(gates, g, 0, tq) * oc_ref[g] + _gate_col(gates, g, 1, tq) * o_sel[g * rows:(g + 1) * rows]
                   + _gate_col(gates, g, 2, tq) * o_w)
            for j in range(NSA_GROUP):
                hd = NSA_GROUP * g + j
                oacc_ref[:, hd * HEAD_DIM:(hd + 1) * HEAD_DIM] = o_g[j * tq:(j + 1) * tq]
        _finish(o_ref, oacc_ref, nz_ref[...], og_ref[...])


def _nsa_sample(cache, page_table, qn, gates, proj, kc, vc, rows_s, state_win, win_s, nsa_g, row0, t_new):
    nb, n_pages = page_table.shape
    past_len = n_pages * PAGE
    n_sub = kc.shape[2]
    n_sel = past_len // SEL_BLOCK + 1
    nsp = -(-n_sel // 128) * 128
    n_tau = n_pages // PAGES_PER_TILE
    rb = row0 // t_new
    n_buf = state_win.shape[1]
    rows = NSA_GROUP * t_new
    lanes = NSA_KV_HEADS * rows
    assert lanes == 128 and nsp % lanes == 0, "the sample kernel packs all query rows of a batch into one lane tile"
    cg = 2 * NSA_KV_HEADS

    def page_spec(r):
        return pl.BlockSpec((1, PAGE, cg, HEAD_DIM),
                            lambda b, i, pt: (pt[b, i * PAGES_PER_TILE + r], 0, 1, 0))

    grid_spec = pltpu.PrefetchScalarGridSpec(
        num_scalar_prefetch=1,
        grid=(nb, n_tau),
        in_specs=[page_spec(r) for r in range(PAGES_PER_TILE)] + [
            pl.BlockSpec((t_new, NSA_WIDTH), lambda b, i, pt: (b, 0)),
            pl.BlockSpec((t_new, 128), lambda b, i, pt: (b, 0)),
            pl.BlockSpec((t_new, NSA_WIDTH), lambda b, i, pt: (rb + b, COL_NZ // 2048)),
            pl.BlockSpec((1, NSA_KV_HEADS, n_sub, HEAD_DIM), lambda b, i, pt: (b, 0, 0, 0)),
            pl.BlockSpec((1, NSA_KV_HEADS, n_sub, HEAD_DIM), lambda b, i, pt: (b, 0, 0, 0)),
            pl.BlockSpec((t_new, cg, HEAD_DIM), lambda b, i, pt: (b, 1, 0)),
            pl.BlockSpec((1, n_buf, cg, HEAD_DIM), lambda b, i, pt: (b, 0, 0, 0)),
            pl.BlockSpec((t_new, cg, HEAD_DIM), lambda b, i, pt: (b, 0, 0)),
            pl.BlockSpec((1, NSA_WIDTH), lambda b, i, pt: (0, 0))],
        out_specs=pl.BlockSpec((t_new, NSA_WIDTH), lambda b, i, pt: (b, 0)),
        scratch_shapes=[pltpu.VMEM((NSA_KV_HEADS * HEAD_DIM, lanes), BF16),
                        pltpu.VMEM((nsp, lanes), F32),
                        pltpu.VMEM((NSA_KV_HEADS, rows, HEAD_DIM), F32),
                        pltpu.VMEM((1, lanes), F32),
                        pltpu.VMEM((1, lanes), F32),
                        pltpu.VMEM((lanes, HEAD_DIM), F32),
                        pltpu.VMEM((t_new, NSA_WIDTH), F32)],
    )
    return pl.pallas_call(
        functools.partial(_nsa_sample_kernel, n_sub=n_sub, nsp=nsp, past_len=past_len, t_new=t_new),
        grid_spec=grid_spec,
        out_shape=jax.ShapeDtypeStruct((nb * t_new, NSA_WIDTH), F32),
        compiler_params=_cparams(("arbitrary", "arbitrary")),
        name="nsa_sample",
    )(page_table, *([cache] * PAGES_PER_TILE), qn, gates, proj, kc, vc, rows_s, state_win, win_s,
      nsa_g.reshape(1, NSA_WIDTH))


def _out_kernel(oh_ref, on_ref, wh_ref, wn_ref, x_ref, gate_ref, y_ref):
    acc = _dot(oh_ref[...], wh_ref[...]) + _dot(on_ref[...], wn_ref[...])
    y_ref[...] = x_ref[...] + gate_ref[...] * acc


def _out_proj(o_h, o_n, w_out, x, gate, tm, tn=512):
    m, d = x.shape
    half = o_h.shape[1]
    grow = gate.shape[0]
    gmap = (lambda i, j: (0, j)) if grow == 1 else (lambda i, j: (i, j))
    return pl.pallas_call(
        _out_kernel,
        grid=(m // tm, d // tn),
        in_specs=[pl.BlockSpec((tm, half), lambda i, j: (i, 0)),
                  pl.BlockSpec((tm, half), lambda i, j: (i, 0)),
                  pl.BlockSpec((half, tn), lambda i, j: (0, j)),
                  pl.BlockSpec((half, tn), lambda i, j: (1, j)),
                  pl.BlockSpec((tm, tn), lambda i, j: (i, j)),
                  pl.BlockSpec((1 if grow == 1 else tm, tn), gmap)],
        out_specs=pl.BlockSpec((tm, tn), lambda i, j: (i, j)),
        out_shape=jax.ShapeDtypeStruct((m, d), F32),
        compiler_params=_cparams(("arbitrary", "arbitrary")),
        name="out_proj",
    )(o_h, o_n, w_out, w_out, x, gate)


def _layer(xp, xs, c_all, cache, s_win, s_hgrn, page_table, lb_param, layer, norm_g, w_ada, b_ada, w_in,
           hgrn_out_g, q_g, k_g, pe, w1, w2, nsa_out_g, w_out):
    t_len = xp.shape[0]
    nb, t_new, _ = xs.shape
    n_s = nb * t_new
    d = D_MODEL

    mod = _ada(c_all, w_ada, b_ada)
    shift, scale, gate = mod[:, :d], mod[:, d:2 * d], mod[:, 2 * d:]
    rep = lambda a: jnp.pad(jnp.repeat(a[1:1 + nb], t_new, axis=0), ((0, SAMPLE_ROWS - n_s), (0, 0)))
    xs_pad = jnp.pad(xs.reshape(n_s, d), ((0, SAMPLE_ROWS - n_s), (0, 0)))
    h_all = _norm(xp, xs_pad, norm_g.reshape(1, d), scale[0:1], shift[0:1], rep(scale), rep(shift))

    n_main = (_N_MAIN_A + _N_WIN) * PROJ_TN
    w_main = w_in
    n_ng = 3 * NSA_HEADS
    w_tail = jnp.concatenate(
        [w_in[:, n_main + n_ng:], w_in[:, n_main:n_main + n_ng],
         jnp.zeros((d, PROJ_TN - n_ng), w_in.dtype)], axis=1).astype(BF16)
    proj = _matmul(h_all, w_main, w_tail)

    oh_p, st_p = _hgrn(proj, lb_param, hgrn_out_g, None, row0=0, n_batch=1, t_len=t_len,
                       chunk=min(64, t_len), out_dtype=BF16, layer=layer)
    oh_s, st_s = _hgrn(proj, lb_param, hgrn_out_g, s_hgrn, row0=t_len, n_batch=nb, t_len=t_new,
                       chunk=t_new, out_dtype=F32, layer=layer)

    qn_p, rows_p, win_p3, selk_p, selvt_p, wink_p, winvt_p, gates_p = _prep(proj, q_g, k_g, 0, t_len)
    qn_s, rows_s, win_s3, _, _, _, _, gates_s = _prep(proj, q_g, k_g, t_len, SAMPLE_ROWS)

    w1r = w1.reshape(2, 2, CMP_STRIDE // 2, 2 * HEAD_DIM, CMP_HIDDEN)
    wc = jnp.concatenate([w1r[:, 0], w1r[:, 1]], axis=-1).astype(BF16)
    w1f = w1.reshape(2, CMP_LEN * HEAD_DIM, CMP_HIDDEN).astype(BF16)
    pe_flat = pe.reshape(2, 1, CMP_LEN * HEAD_DIM)
    w2b = w2.astype(BF16)

    n_pp = t_len // PAGE
    kc_p, vc_p = _compress(rows_p.reshape(n_pp, PAGE, 4 * NSA_KV_HEADS, HEAD_DIM),
                           jnp.arange(n_pp, dtype=jnp.int32).reshape(1, n_pp), wc, w1f, pe_flat, w2b, k_g[0])
    cache4 = cache.reshape(cache.shape[0], PAGE, 4 * NSA_KV_HEADS, HEAD_DIM)
    kc_s, vc_s = _compress(cache4, page_table, wc, w1f, pe_flat, w2b, k_g[0])

    on_p = _nsa_prompt(qn_p, gates_p, proj, kc_p, vc_p, selk_p, selvt_p, wink_p, winvt_p, nsa_out_g, t_len)
    n_buf = s_win.shape[1]
    s_win4 = s_win.reshape(nb, n_buf, 2 * NSA_KV_HEADS, HEAD_DIM)
    on_s = _nsa_sample(cache4, page_table, qn_s, gates_s, proj, kc_s, vc_s, rows_s, s_win4, win_s3,
                       nsa_out_g, t_len, t_new)

    w_out_b = w_out.astype(BF16)
    y_p = _out_proj(oh_p, on_p, w_out_b, xp, gate[0:1], tm=min(1024, t_len))
    y_s = _out_proj(oh_s, on_s, w_out_b, xs.reshape(n_s, d), jnp.repeat(gate[1:1 + nb], t_new, axis=0), tm=n_s)

    n_w = min(WINDOW, t_len)
    kv_p = rows_p.reshape(1, t_len, 4, NSA_KV_HEADS, HEAD_DIM)
    kv_s = rows_s[:n_s].reshape(nb, t_new, 4, NSA_KV_HEADS, HEAD_DIM)
    win_p = win_p3[t_len - n_w:].reshape(1, n_w, 2, NSA_KV_HEADS, HEAD_DIM)
    win_new = win_s3[:n_s].reshape(nb, t_new, 2, NSA_KV_HEADS, HEAD_DIM)
    win_s = jnp.concatenate([s_win, win_new], axis=1)[:, -n_buf:]
    return y_p, y_s.reshape(nb, t_new, d), kv_p, kv_s, win_p, win_s, st_p, st_s


def kernel(x_prompt, x_sample, cache_kv, state_win, state_hgrn, page_table, c_prompt, c_sample, norm_g, w_ada,
           b_ada, w_in, hgrn_lb, hgrn_out_g, q_norm_g, k_norm_g, cmp_pe, cmp_w1, cmp_w2, nsa_out_g, w_out):
    depth = w_in.shape[0]
    assert depth == 1 and x_prompt.shape[0] == 1
    nb = x_sample.shape[0]
    c_all = jnp.concatenate([c_prompt, c_sample], axis=0)
    c_all = jnp.pad(c_all, ((0, (-c_all.shape[0]) % 8), (0, 0)))
    l = 0
    outs = _layer(x_prompt[0], x_sample, c_all, cache_kv[l], state_win[l], state_hgrn[l], page_table,
                  hgrn_lb, l, norm_g[l], w_ada[l], b_ada[l], w_in[l], hgrn_out_g[l], q_norm_g[l], k_norm_g[l],
                  cmp_pe[l], cmp_w1[l], cmp_w2[l], nsa_out_g[l], w_out[l])
    y_p, y_s, kv_p, kv_s, win_p, win_s, st_p, st_s = outs
    return (y_p[None], y_s, kv_p[None], kv_s[None], win_p[None], win_s[None], st_p[None], st_s[None])
```

```python
import functools

import jax
import jax.numpy as jnp
from jax import lax
from jax.experimental import pallas as pl
from jax.experimental.pallas import tpu as pltpu

F32 = jnp.float32
BF16 = jnp.bfloat16

D_MODEL = 4096
HEAD_DIM = 128
HGRN_WIDTH = 2048
NSA_WIDTH = 2048
HGRN_HEADS = 16
NSA_HEADS = 16
NSA_KV_HEADS = 4
NSA_GROUP = 4
KV_WIDTH = 512
CMP_LEN = 32
CMP_STRIDE = 16
CMP_HIDDEN = 256
SEL_BLOCK = 64
SEL_TOP = 16
WINDOW = 512
Q_BLOCK = 128
NORM_EPS = 1e-6
PAGE = 128

COL_HQ, COL_HF, COL_HI, COL_HZ, COL_NQ, COL_ROWS, COL_NZ = (i * 2048 for i in range(7))
COL_WIN = 14336
COL_NG = 15360
N_PROJ = 15872
SAMPLE_ROWS = 256
ROW_TILE = 256
PAGES_PER_TILE = 8
KV_TILE = 512
NEG = -1e30
LOG2E = 1.4426950408889634
VMEM_LIMIT = 56 * 1024 * 1024


def _cparams(sem):
    return pltpu.CompilerParams(dimension_semantics=sem, vmem_limit_bytes=VMEM_LIMIT)


def _dot(a, b):
    return jnp.dot(a.astype(BF16), b.astype(BF16), preferred_element_type=F32)


def _dot_nt(a, b):
    return lax.dot_general(a.astype(BF16), b.astype(BF16), (((1,), (1,)), ((), ())),
                           preferred_element_type=F32)


def _dot_tn(a, b):
    return lax.dot_general(a.astype(BF16), b.astype(BF16), (((0,), (0,)), ((), ())),
                           preferred_element_type=F32)


def _silu(x):
    return x * jax.nn.sigmoid(x)


def _ada_kernel(c_ref, w_ref, b_ref, o_ref):
    c = c_ref[...]
    o_ref[...] = _dot(_silu(c), w_ref[...]) + b_ref[...]


def _ada(c_all, w_ada, b_ada):
    m, d = c_all.shape
    n = w_ada.shape[1]
    tn = 512
    return pl.pallas_call(
        _ada_kernel,
        grid=(n // tn,),
        in_specs=[pl.BlockSpec((m, d), lambda j: (0, 0)),
                  pl.BlockSpec((d, tn), lambda j: (0, j)),
                  pl.BlockSpec((1, tn), lambda j: (0, j))],
        out_specs=pl.BlockSpec((m, tn), lambda j: (0, j)),
        out_shape=jax.ShapeDtypeStruct((m, n), F32),
        compiler_params=_cparams(("arbitrary",)),
        name="ada",
    )(c_all, w_ada, b_ada.reshape(1, n))


def _norm_kernel(xp_ref, xs_ref, g_ref, scp_ref, shp_ref, scs_ref, shs_ref, o_ref, *, n_prompt):
    i = pl.program_id(0)

    def body(x, scale, shift):
        r = x * lax.rsqrt(jnp.mean(x * x, axis=-1, keepdims=True) + NORM_EPS) * g_ref[...]
        o_ref[...] = (r * (1.0 + scale) + shift).astype(o_ref.dtype)

    @pl.when(i < n_prompt)
    def _():
        body(xp_ref[...], scp_ref[...], shp_ref[...])

    @pl.when(i >= n_prompt)
    def _():
        body(xs_ref[...], scs_ref[...], shs_ref[...])


def _norm(xp, xs, g, sc_p, sh_p, sc_s, sh_s):
    t, d = xp.shape
    n_prompt = t // ROW_TILE
    row = lambda i: (0, 0)
    return pl.pallas_call(
        functools.partial(_norm_kernel, n_prompt=n_prompt),
        grid=(n_prompt + 1,),
        in_specs=[pl.BlockSpec((ROW_TILE, d), lambda i: (jnp.minimum(i, n_prompt - 1), 0)),
                  pl.BlockSpec((SAMPLE_ROWS, d), row),
                  pl.BlockSpec((1, d), row), pl.BlockSpec((1, d), row), pl.BlockSpec((1, d), row),
                  pl.BlockSpec((SAMPLE_ROWS, d), row), pl.BlockSpec((SAMPLE_ROWS, d), row)],
        out_specs=pl.BlockSpec((ROW_TILE, d), lambda i: (i, 0)),
        out_shape=jax.ShapeDtypeStruct((t + SAMPLE_ROWS, d), BF16),
        compiler_params=_cparams(("arbitrary",)),
        name="norm",
    )(xp, xs, g, sc_p, sh_p, sc_s, sh_s)


PROJ_TN = 512
_N_MAIN_A, _N_NZ, _N_WIN = 24, 4, 2
_ROW_WIN = _N_MAIN_A * PROJ_TN
_ROW_NG = _ROW_WIN + _N_WIN * PROJ_TN
_ROW_NZ = _ROW_NG + 3 * NSA_HEADS


_ROW_UNIT = 16


def _proj_weight_row(j):
    u = PROJ_TN // _ROW_UNIT
    q = jnp.where(j < _N_MAIN_A, j * u,
                  jnp.where(j < _N_MAIN_A + _N_NZ, _ROW_NZ // _ROW_UNIT + (j - _N_MAIN_A) * u,
                            jnp.where(j < _N_MAIN_A + _N_NZ + _N_WIN,
                                      _ROW_WIN // _ROW_UNIT + (j - _N_MAIN_A - _N_NZ) * u, _ROW_NG // _ROW_UNIT)))
    return q * _ROW_UNIT


def _mm_kernel(a_ref, wt_ref, o_ref, wb_ref):
    @pl.when(pl.program_id(1) == 0)
    def _():
        wb_ref[...] = wt_ref[...].astype(BF16)

    o_ref[...] = lax.dot_general(a_ref[...], wb_ref[...], (((1,), (1,)), ((), ())), preferred_element_type=F32)


def _row_tile(m, cap=1100):
    best = 16
    for tm in range(16, cap + 1, 16):
        if m % tm == 0:
            best = tm
    return best


def _matmul(a, w_t):
    m, k = a.shape
    tn = PROJ_TN
    tm = _row_tile(m)
    return pl.pallas_call(
        _mm_kernel,
        grid=(N_PROJ // tn, m // tm),
        in_specs=[pl.BlockSpec((tm, k), lambda j, i: (i, 0)),
                  pl.BlockSpec((pl.Element(tn), pl.Element(k)), lambda j, i: (_proj_weight_row(j), 0))],
        out_specs=pl.BlockSpec((tm, tn), lambda j, i: (i, j)),
        out_shape=jax.ShapeDtypeStruct((m, N_PROJ), F32),
        scratch_shapes=[pltpu.VMEM((tn, k), BF16)],
        compiler_params=_cparams(("arbitrary", "arbitrary")),
        name="proj",
    )(a, w_t)


def _hgrn_kernel(*refs, chunk, levels, has_s0, layer):
    if has_s0:
        lbp_ref, og_ref, hq_ref, hf_ref, hi_ref, hz_ref, s0_ref, o_ref, sout_ref, st_ref = refs
    else:
        lbp_ref, og_ref, hq_ref, hf_ref, hi_ref, hz_ref, o_ref, sout_ref, st_ref = refs
        s0_ref = None
    C = chunk
    c = pl.program_id(1)
    n_chunks = pl.num_programs(1)

    @pl.when(c == 0)
    def _():
        for h in range(HGRN_HEADS):
            if has_s0:
                st_ref[h] = s0_ref[0, h].T
            else:
                st_ref[h] = jnp.zeros((HEAD_DIM, HEAD_DIM), F32)

    lbp = lbp_ref[...]
    e = jnp.exp(lbp - jnp.max(lbp, axis=0, keepdims=True))
    lb = jnp.sum(e[:layer + 1], axis=0, keepdims=True) / jnp.sum(e, axis=0, keepdims=True)

    f = lb + (1.0 - lb) * jax.nn.sigmoid(hf_ref[...])
    logf = jnp.log(f)
    k = 1.0 - f
    q = hq_ref[...]
    v = hi_ref[...]
    hz = hz_ref[...]
    n = q.shape[1]

    row = lax.broadcasted_iota(jnp.int32, (C, 1), 0)
    cum = logf
    s = 1
    while s < C:
        cum = cum + jnp.where(row >= s, pltpu.roll(cum, s, axis=0), 0.0)
        s *= 2
    last = cum[C - 1:C, :]
    qe = q * jnp.exp(cum)
    kd = k * jnp.exp(last - cum)
    elast = jnp.exp(last)

    lvl = []
    for h in levels:
        cum3 = cum.reshape(C // (2 * h), 2 * h, n)
        cmid = jnp.broadcast_to(cum3[:, h - 1:h, :], cum3.shape).reshape(C, n)
        right = ((row // h) % 2) == 1
        ql = jnp.where(right, q * jnp.exp(jnp.where(right, cum - cmid, 0.0)), 0.0)
        kl = jnp.where(right, 0.0, k * jnp.exp(jnp.where(right, 0.0, cmid - cum)))
        ti = lax.broadcasted_iota(jnp.int32, (C, C), 0)
        si = lax.broadcasted_iota(jnp.int32, (C, C), 1)
        mask = ((ti // (2 * h)) == (si // (2 * h))) & (((ti // h) % 2) == 1) & (((si // h) % 2) == 0)
        lvl.append((ql, kl, mask))

    nb = C // 8
    rin = lax.broadcasted_iota(jnp.int32, (nb, 8, 1), 1)
    og = og_ref[...]
    for h in range(HGRN_HEADS):
        sl = slice(h * HEAD_DIM, (h + 1) * HEAD_DIM)
        st = st_ref[h]
        q_h, k_h, v_h, cum_h = q[:, sl], k[:, sl], v[:, sl], cum[:, sl]
        o_h = _dot_nt(qe[:, sl], st)
        if levels:
            a = jnp.zeros((C, C), F32)
            for ql, kl, mask in lvl:
                a = a + jnp.where(mask, _dot_nt(ql[:, sl], kl[:, sl]), 0.0)
            o_h = o_h + _dot(a, v_h)
        q3 = q_h.reshape(nb, 8, HEAD_DIM)
        k3 = k_h.reshape(nb, 8, HEAD_DIM)
        v3 = v_h.reshape(nb, 8, HEAD_DIM)
        c3 = cum_h.reshape(nb, 8, HEAD_DIM)
        od = jnp.zeros((nb, 8, HEAD_DIM), F32)
        for s in range(8):
            dec = jnp.exp(jnp.where(rin >= s, c3 - c3[:, s:s + 1, :], -jnp.inf))
            w = q3 * k3[:, s:s + 1, :] * dec
            od = od + jnp.sum(w, axis=-1, keepdims=True) * v3[:, s:s + 1, :]
        o_h = o_h + od.reshape(C, HEAD_DIM)
        st_ref[h] = st * elast[:, sl] + _dot_tn(v_h, kd[:, sl])
        o_h = o_h * lax.rsqrt(jnp.mean(o_h * o_h, axis=-1, keepdims=True) + NORM_EPS) * og
        o_ref[:, sl] = (o_h * _silu(hz[:, sl])).astype(o_ref.dtype)

    @pl.when(c == n_chunks - 1)
    def _():
        for h in range(HGRN_HEADS):
            sout_ref[0, h] = st_ref[h].T


def _hgrn(proj, lb_param, out_g, s0, *, row0, n_batch, t_len, chunk, out_dtype, layer):
    n_chunks = t_len // chunk
    levels = []
    h = chunk // 2
    while h >= 8:
        levels.append(h)
        h //= 2
    blk0 = row0 // chunk
    rmap = lambda col: (lambda b, c: (blk0 + b * n_chunks + c, col))
    in_specs = [pl.BlockSpec(lb_param.shape, lambda b, c: (0, 0)),
                pl.BlockSpec((1, HEAD_DIM), lambda b, c: (0, 0)),
                pl.BlockSpec((chunk, HGRN_WIDTH), rmap(0)),
                pl.BlockSpec((chunk, HGRN_WIDTH), rmap(1)),
                pl.BlockSpec((chunk, HGRN_WIDTH), rmap(2)),
                pl.BlockSpec((chunk, HGRN_WIDTH), rmap(3))]
    args = [lb_param, out_g.reshape(1, HEAD_DIM), proj, proj, proj, proj]
    if s0 is not None:
        in_specs.append(pl.BlockSpec((1, HGRN_HEADS, HEAD_DIM, HEAD_DIM), lambda b, c: (b, 0, 0, 0)))
        args.append(s0)
    return pl.pallas_call(
        functools.partial(_hgrn_kernel, chunk=chunk, levels=tuple(levels), has_s0=s0 is not None, layer=layer),
        grid=(n_batch, n_chunks),
        in_specs=in_specs,
        out_specs=[pl.BlockSpec((chunk, HGRN_WIDTH), lambda b, c: (b * n_chunks + c, 0)),
                   pl.BlockSpec((1, HGRN_HEADS, HEAD_DIM, HEAD_DIM), lambda b, c: (b, 0, 0, 0))],
        out_shape=[jax.ShapeDtypeStruct((n_batch * t_len, HGRN_WIDTH), out_dtype),
                   jax.ShapeDtypeStruct((n_batch, HGRN_HEADS, HEAD_DIM, HEAD_DIM), F32)],
        scratch_shapes=[pltpu.VMEM((HGRN_HEADS, HEAD_DIM, HEAD_DIM), F32)],
        compiler_params=_cparams(("arbitrary", "arbitrary")),
        name="hgrn",
    )(*args)


def _head_rms(x, g, n_heads):
    outs = []
    for h in range(n_heads):
        xh = x[:, h * HEAD_DIM:(h + 1) * HEAD_DIM]
        outs.append(xh * lax.rsqrt(jnp.mean(xh * xh, axis=-1, keepdims=True) + NORM_EPS) * g)
    return jnp.concatenate(outs, axis=1)


def _prep_kernel(nq_ref, rows_ref, win_ref, ng_ref, qg_ref, kg_ref,
                 q_ref, rows_o, win_o, selk_o, selvt_o, wink_o, winvt_o, gate_o):
    kg = kg_ref[...]
    q_ref[...] = _head_rms(nq_ref[...], qg_ref[...], NSA_HEADS) * (HEAD_DIM ** -0.5 * LOG2E)
    rows = rows_ref[...]
    ks = _head_rms(rows[:, 2 * KV_WIDTH:3 * KV_WIDTH], kg[1:2], NSA_KV_HEADS)
    for cg in range(4 * NSA_KV_HEADS):
        sl = slice(cg * HEAD_DIM, (cg + 1) * HEAD_DIM)
        if 2 * NSA_KV_HEADS <= cg < 3 * NSA_KV_HEADS:
            rows_o[:, cg, :] = ks[:, (cg - 2 * NSA_KV_HEADS) * HEAD_DIM:(cg - 2 * NSA_KV_HEADS + 1) * HEAD_DIM]
        else:
            rows_o[:, cg, :] = rows[:, sl]
    selk_o[...] = ks.astype(BF16)
    selvt_o[...] = rows[:, 3 * KV_WIDTH:].T.astype(BF16)
    win = win_ref[...]
    kw = _head_rms(win[:, :KV_WIDTH], kg[2:3], NSA_KV_HEADS)
    for cg in range(2 * NSA_KV_HEADS):
        sl = slice(cg * HEAD_DIM, (cg + 1) * HEAD_DIM)
        win_o[:, cg, :] = kw[:, sl] if cg < NSA_KV_HEADS else win[:, sl]
    wink_o[...] = kw.astype(BF16)
    winvt_o[...] = win[:, KV_WIDTH:].T.astype(BF16)
    gate_o[...] = jax.nn.sigmoid(ng_ref[...])


def _prep(proj, q_g, k_g, row0, n_rows):
    t = min(ROW_TILE, n_rows)
    r0 = row0 // t
    return pl.pallas_call(
        _prep_kernel,
        grid=(n_rows // t,),
        in_specs=[pl.BlockSpec((t, NSA_WIDTH), lambda i: (r0 + i, COL_NQ // 2048)),
                  pl.BlockSpec((t, 2048), lambda i: (r0 + i, COL_ROWS // 2048)),
                  pl.BlockSpec((t, 1024), lambda i: (r0 + i, COL_WIN // 1024)),
                  pl.BlockSpec((t, 128), lambda i: (r0 + i, COL_NG // 128)),
                  pl.BlockSpec((1, HEAD_DIM), lambda i: (0, 0)),
                  pl.BlockSpec((3, HEAD_DIM), lambda i: (0, 0))],
        out_specs=[pl.BlockSpec((t, 2048), lambda i: (i, 0)),
                   pl.BlockSpec((t, 4 * NSA_KV_HEADS, HEAD_DIM), lambda i: (i, 0, 0)),
                   pl.BlockSpec((t, 2 * NSA_KV_HEADS, HEAD_DIM), lambda i: (i, 0, 0)),
                   pl.BlockSpec((t, KV_WIDTH), lambda i: (i, 0)),
                   pl.BlockSpec((KV_WIDTH, t), lambda i: (0, i)),
                   pl.BlockSpec((t, KV_WIDTH), lambda i: (i, 0)),
                   pl.BlockSpec((KV_WIDTH, t), lambda i: (0, i)),
                   pl.BlockSpec((t, 128), lambda i: (i, 0))],
        out_shape=[jax.ShapeDtypeStruct((n_rows, 2048), F32),
                   jax.ShapeDtypeStruct((n_rows, 4 * NSA_KV_HEADS, HEAD_DIM), F32),
                   jax.ShapeDtypeStruct((n_rows, 2 * NSA_KV_HEADS, HEAD_DIM), F32),
                   jax.ShapeDtypeStruct((n_rows, KV_WIDTH), BF16),
                   jax.ShapeDtypeStruct((KV_WIDTH, n_rows), BF16),
                   jax.ShapeDtypeStruct((n_rows, KV_WIDTH), BF16),
                   jax.ShapeDtypeStruct((KV_WIDTH, n_rows), BF16),
                   jax.ShapeDtypeStruct((n_rows, 128), F32)],
        compiler_params=_cparams(("arbitrary",)),
        name="prep",
    )(proj, proj, proj, proj, q_g.reshape(1, HEAD_DIM), k_g)


def _compress_kernel(pt_ref, *refs):
    pages = refs[:PAGES_PER_TILE]
    wc_ref, w1f_ref, pe_ref, w2_ref, kcg_ref, kc_ref, vc_ref, cvec_ref, carry_ref, out_scr = refs[PAGES_PER_TILE:]
    b = pl.program_id(0)
    i = pl.program_id(1)
    nsub = PAGE // CMP_STRIDE
    ntile = nsub * PAGES_PER_TILE
    G = NSA_KV_HEADS
    rows = ntile * G

    @pl.when((b == 0) & (i == 0))
    def _():
        for ch in range(2):
            pe = jnp.broadcast_to(pe_ref[ch], (8, CMP_LEN * HEAD_DIM))
            cvec_ref[ch] = _dot(pe, w1f_ref[ch])

    @pl.when(i == 0)
    def _():
        carry_ref[...] = jnp.zeros(carry_ref.shape, F32)

    def gathered(p):
        lo, hi = [], []
        for r in range(PAGES_PER_TILE):
            a = pages[r][0, pl.ds(p, nsub // 2, stride=2 * CMP_STRIDE), :, :]
            bb = pages[r][0, pl.ds(p + CMP_STRIDE, nsub // 2, stride=2 * CMP_STRIDE), :, :]
            lo.append(jnp.concatenate([a[:, :G], bb[:, :G]], axis=1).reshape(nsub * G, HEAD_DIM))
            hi.append(jnp.concatenate([a[:, G:], bb[:, G:]], axis=1).reshape(nsub * G, HEAD_DIM))
        return jnp.concatenate(lo, axis=0).astype(BF16), jnp.concatenate(hi, axis=0).astype(BF16)

    xs = [gathered(p) for p in range(CMP_STRIDE)]
    u = [jnp.dot(jnp.concatenate([x[ch] for x in xs], axis=1), wc_ref[ch], preferred_element_type=F32)
         for ch in range(2)]

    row8 = lax.broadcasted_iota(jnp.int32, (8, 1), 0)
    for ch in range(2):
        u0, u1 = u[ch][:, :CMP_HIDDEN], u[ch][:, CMP_HIDDEN:]
        rolled = pltpu.roll(u0, G, axis=0)
        head = jnp.where(row8 < G, carry_ref[ch], rolled[0:8])
        carry_ref[ch] = rolled[0:8]
        prev = jnp.concatenate([head, rolled[8:]], axis=0)
        pre = prev + u1 + cvec_ref[ch][0:1, :]
        out = _dot(_silu(pre), w2_ref[ch])
        if ch == 0:
            out = out * lax.rsqrt(jnp.mean(out * out, axis=-1, keepdims=True) + NORM_EPS) * kcg_ref[...]
        out_scr[...] = out
        dst = kc_ref if ch == 0 else vc_ref
        for g in range(G):
            dst[0, g] = out_scr[pl.ds(g, ntile, stride=G), :].astype(dst.dtype)


def _compress(src, page_table, wc, w1f, pe_flat, w2, kc_g):
    nb, n_pages = page_table.shape
    n_tiles = n_pages // PAGES_PER_TILE
    nsub = PAGE // CMP_STRIDE
    ntile = nsub * PAGES_PER_TILE

    def page_spec(r):
        return pl.BlockSpec((1, PAGE, 2 * NSA_KV_HEADS, HEAD_DIM),
                            lambda b, i, pt: (pt[b, i * PAGES_PER_TILE + r], 0, 0, 0))

    const = lambda *shape: pl.BlockSpec(shape, lambda b, i, pt: (0,) * len(shape))
    out_spec = pl.BlockSpec((1, NSA_KV_HEADS, ntile, HEAD_DIM), lambda b, i, pt: (b, 0, i, 0))
    grid_spec = pltpu.PrefetchScalarGridSpec(
        num_scalar_prefetch=1,
        grid=(nb, n_tiles),
        in_specs=[page_spec(r) for r in range(PAGES_PER_TILE)] + [
            const(2, CMP_STRIDE * HEAD_DIM, 2 * CMP_HIDDEN),
            const(2, CMP_LEN * HEAD_DIM, CMP_HIDDEN),
            const(2, 1, CMP_LEN * HEAD_DIM),
            const(2, CMP_HIDDEN, HEAD_DIM),
            const(1, HEAD_DIM)],
        out_specs=[out_spec, out_spec],
        scratch_shapes=[pltpu.VMEM((2, 8, CMP_HIDDEN), F32),
                        pltpu.VMEM((2, 8, CMP_HIDDEN), F32),
                        pltpu.VMEM((NSA_KV_HEADS * ntile, HEAD_DIM), F32)],
    )
    shape = jax.ShapeDtypeStruct((nb, NSA_KV_HEADS, n_tiles * ntile, HEAD_DIM), BF16)
    return pl.pallas_call(
        _compress_kernel,
        grid_spec=grid_spec,
        out_shape=[shape, shape],
        compiler_params=_cparams(("arbitrary", "arbitrary")),
        name="compress",
    )(page_table, *([src] * PAGES_PER_TILE), wc, w1f, pe_flat, w2, kc_g.reshape(1, HEAD_DIM))


def _split3(x):
    hi = x.astype(BF16)
    r = x - hi.astype(F32)
    mid = r.astype(BF16)
    lo = (r - mid.astype(F32)).astype(BF16)
    return hi, mid, lo


def _cmp_branch(qg, kc, vc, tpos4, n_sub):
    s = _dot_nt(qg, kc)
    npr = lax.broadcasted_iota(jnp.int32, (1, n_sub), 1)
    valid = (npr >= 1) & (npr * CMP_STRIDE + (CMP_STRIDE - 1) <= tpos4)
    s = jnp.where(valid, s, NEG)
    m = jnp.max(s, axis=-1, keepdims=True)
    p = jnp.where(valid, jnp.exp2(s - m), 0.0)
    l = jnp.sum(p, axis=-1, keepdims=True)
    p = p * jnp.where(l > 0.0, 1.0 / l, 0.0)
    return _dot(p, vc), p


def _softmax_cols(s, mask):
    s = jnp.where(mask, s, NEG)
    m = jnp.maximum(jnp.max(s, axis=0, keepdims=True), 0.1 * NEG)
    p = jnp.exp2(s - m)
    l = jnp.sum(p, axis=0, keepdims=True)
    return p, jnp.where(l > 0.0, 1.0 / l, 0.0)


def _select_t(imp, tpos, nsp):
    sb = lax.broadcasted_iota(jnp.int32, (nsp, 1), 0)
    cur = tpos // SEL_BLOCK
    causal = sb * SEL_BLOCK <= tpos
    forced = (sb == 0) | (sb == cur) | (sb == cur - 1)
    score = jnp.where(causal, jnp.where(forced, -NEG, imp), NEG)
    sel = jnp.zeros(score.shape, jnp.bool_)
    sbf = sb.astype(F32)
    for _ in range(SEL_TOP):
        m = jnp.max(score, axis=0, keepdims=True)
        idx = jnp.min(jnp.where(score == m, sbf, float(nsp)), axis=0, keepdims=True)
        hit = sbf == idx
        sel = sel | hit
        score = jnp.where(hit, NEG, score)
    return sel & causal


def _importance(psum, n_sub, nsp):
    npr = lax.broadcasted_iota(jnp.int32, (n_sub, nsp), 0)
    sb = lax.broadcasted_iota(jnp.int32, (n_sub, nsp), 1)
    r = SEL_BLOCK // CMP_STRIDE
    ov = ((npr >= r * sb) & (npr <= r * sb + r) & (npr >= 1)).astype(BF16)
    hi, mid, lo = _split3(psum)
    dot = lambda a: jnp.dot(a, ov, preferred_element_type=F32)
    return dot(hi) + dot(mid) + dot(lo)


def _select(imp, tpos, nsp):
    sb = lax.broadcasted_iota(jnp.int32, (1, nsp), 1)
    cur = tpos // SEL_BLOCK
    causal = sb * SEL_BLOCK <= tpos
    forced = (sb == 0) | (sb == cur) | (sb == cur - 1)
    score = jnp.where(causal, jnp.where(forced, -NEG, imp), NEG)
    sel = jnp.zeros(score.shape, jnp.bool_)
    sbf = sb.astype(F32)
    for _ in range(SEL_TOP):
        m = jnp.max(score, axis=-1, keepdims=True)
        idx = jnp.min(jnp.where(score == m, sbf, float(nsp)), axis=-1, keepdims=True)
        hit = sbf == idx
        sel = sel | hit
        score = jnp.where(hit, NEG, score)
    return sel & causal


def _online_step(carry, s, vt):
    m, l, acc = carry
    m_new = jnp.maximum(m, jnp.max(s, axis=-1, keepdims=True))
    alpha = jnp.exp2(m - m_new)
    p = jnp.exp2(s - m_new)
    l = alpha * l + jnp.sum(p, axis=-1, keepdims=True)
    acc = alpha * acc + _dot(p, vt)
    return m_new, l, acc


def _block_onehot(key0, n_keys, nsp):
    shift = SEL_BLOCK.bit_length() - 1
    kb = jnp.right_shift(lax.broadcasted_iota(jnp.int32, (n_keys, 1), 0), shift)
    sb = lax.broadcasted_iota(jnp.int32, (1, nsp), 1) - jnp.right_shift(key0, shift)
    return (kb == sb).astype(BF16)


def _window_branch(qg, kw, vw, kpos, tpos4):
    s = _dot_nt(qg, kw)
    mask = (kpos <= tpos4) & (kpos >= tpos4 - WINDOW) & (kpos >= 0)
    s = jnp.where(mask, s, NEG)
    m = jnp.max(s, axis=-1, keepdims=True)
    p = jnp.where(mask, jnp.exp2(s - m), 0.0)
    l = jnp.sum(p, axis=-1, keepdims=True)
    return _dot(p, vw) * jnp.where(l > 0.0, 1.0 / l, 0.0)


def _gate_col(gates, g, br, tq):
    return jnp.concatenate(
        [gates[:, 3 * (NSA_GROUP * g + j) + br:3 * (NSA_GROUP * g + j) + br + 1] for j in range(NSA_GROUP)], axis=0)


def _finish(o_ref, oacc_ref, nz, og):
    o = oacc_ref[...]
    o = o * lax.rsqrt(jnp.mean(o * o, axis=-1, keepdims=True) + NORM_EPS) * og
    o_ref[...] = (o * _silu(nz)).astype(o_ref.dtype)


def _nsa_prompt_kernel(q_ref, gate_ref, nz_ref, kc_ref, vc_ref, selk_ref, selvt_ref, *rest, n_sub, nsp):
    nwin = WINDOW // Q_BLOCK + 1
    wink_refs = rest[:nwin]
    winvt_refs = rest[nwin:2 * nwin]
    og_ref, o_ref, oacc_ref = rest[2 * nwin:]
    i = pl.program_id(0)
    tq = Q_BLOCK
    lanes = NSA_GROUP * tq
    t0 = i * tq
    tpos = t0 + lax.broadcasted_iota(jnp.int32, (1, tq), 1)
    tpos4 = jnp.concatenate([tpos] * NSA_GROUP, axis=1)
    q = q_ref[...]
    gates_t = gate_ref[...].T

    def gate_row(g, br):
        rows = [3 * (NSA_GROUP * g + j) + br for j in range(NSA_GROUP)]
        return jnp.concatenate([gates_t[r:r + 1, :] for r in rows], axis=1)

    npr = lax.broadcasted_iota(jnp.int32, (n_sub, 1), 0)
    cvalid = (npr >= 1) & (npr * CMP_STRIDE + (CMP_STRIDE - 1) <= tpos4)
    sb_o = lax.broadcasted_iota(jnp.int32, (nsp, n_sub), 0)
    np_o = lax.broadcasted_iota(jnp.int32, (nsp, n_sub), 1)
    r = SEL_BLOCK // CMP_STRIDE
    ovt = ((np_o >= r * sb_o) & (np_o <= r * sb_o + r) & (np_o >= 1)).astype(BF16)

    groups = range(NSA_KV_HEADS)
    qts = [jnp.concatenate([q[:, (NSA_GROUP * g + j) * HEAD_DIM:(NSA_GROUP * g + j + 1) * HEAD_DIM].T
                            for j in range(NSA_GROUP)], axis=1).astype(BF16) for g in groups]

    cs = [jnp.dot(kc_ref[0, g], qts[g], preferred_element_type=F32) for g in groups]
    cps = [p * inv for p, inv in (_softmax_cols(s, cvalid) for s in cs)]
    octs = [_dot_tn(vc_ref[0, g], cps[g]) for g in groups]
    imps = []
    for g in groups:
        psum = cps[g][:, 0:tq]
        for j in range(1, NSA_GROUP):
            psum = psum + cps[g][:, j * tq:(j + 1) * tq]
        imps.append(sum(jnp.dot(ovt, part, preferred_element_type=F32) for part in _split3(psum)))
    sel = _select_t(jnp.concatenate(imps, axis=1), tpos4, nsp)
    bias_t = jnp.where(sel, 0.0, NEG).astype(BF16)

    kpos = t0 - WINDOW + lax.broadcasted_iota(jnp.int32, (WINDOW + tq, 1), 0)
    wmask = (kpos <= tpos4) & (kpos >= tpos4 - WINDOW) & (kpos >= 0)
    ws = [jnp.dot(jnp.concatenate([w[:, g * HEAD_DIM:(g + 1) * HEAD_DIM] for w in wink_refs], axis=0), qts[g],
                  preferred_element_type=F32) for g in groups]
    wps = [_softmax_cols(s, wmask) for s in ws]
    owts = [jnp.dot(jnp.concatenate([w[g * HEAD_DIM:(g + 1) * HEAD_DIM, :] for w in winvt_refs], axis=1),
                    wps[g][0].astype(BF16), preferred_element_type=F32) * wps[g][1] for g in groups]
    n_full = t0 // KV_TILE
    krow = lax.broadcasted_iota(jnp.int32, (KV_TILE, 1), 0)
    qaugs = []
    for g in range(NSA_KV_HEADS):
        bg = bias_t[:, g * tq:(g + 1) * tq]
        qaugs.append(jnp.concatenate([qts[g], jnp.concatenate([bg] * NSA_GROUP, axis=1)], axis=0))

    def tile_step(c, carries, diagonal):
        k0 = pl.multiple_of(c * KV_TILE, KV_TILE)
        onehot = _block_onehot(k0, KV_TILE, nsp)
        scores = []
        for g in range(NSA_KV_HEADS):
            kt = selk_ref[pl.ds(k0, KV_TILE), pl.ds(g * HEAD_DIM, HEAD_DIM)]
            s = jnp.dot(jnp.concatenate([kt, onehot], axis=1), qaugs[g], preferred_element_type=F32)
            if diagonal:
                s = jnp.where(k0 + krow <= tpos4, s, NEG)
            scores.append(s)
        stats = []
        for g in range(NSA_KV_HEADS):
            m, l, _ = carries[g]
            m_new = jnp.maximum(m, jnp.max(scores[g], axis=0, keepdims=True))
            alpha = jnp.exp2(m - m_new)
            p = jnp.exp2(scores[g] - m_new)
            stats.append((m_new, alpha * l + jnp.sum(p, axis=0, keepdims=True), alpha, p.astype(BF16)))
        out = []
        for g in range(NSA_KV_HEADS):
            m_new, l, alpha, p = stats[g]
            vtt = selvt_ref[pl.ds(g * HEAD_DIM, HEAD_DIM), pl.ds(k0, KV_TILE)]
            out.append((m_new, l, alpha * carries[g][2] + jnp.dot(vtt, p, preferred_element_type=F32)))
        return tuple(out)

    init = tuple((jnp.full((1, lanes), NEG, F32), jnp.zeros((1, lanes), F32), jnp.zeros((HEAD_DIM, lanes), F32))
                 for _ in range(NSA_KV_HEADS))
    carries = lax.fori_loop(0, n_full, lambda c, cr: tile_step(c, cr, False), init)
    carries = tile_step(n_full, carries, True)

    for g in groups:
        m, l, acc = carries[g]
        o_s = acc * (1.0 / l)
        o_g = gate_row(g, 0) * octs[g] + gate_row(g, 1) * o_s + gate_row(g, 2) * owts[g]
        for j in range(NSA_GROUP):
            hd = NSA_GROUP * g + j
            oacc_ref[:, hd * HEAD_DIM:(hd + 1) * HEAD_DIM] = o_g[:, j * tq:(j + 1) * tq].T

    _finish(o_ref, oacc_ref, nz_ref[...], og_ref[...])


def _nsa_prompt(qn, gates, proj, kc, vc, selk, selvt, wink, winvt, nsa_g, t_len):
    n_sub = kc.shape[2]
    n_sel = t_len // SEL_BLOCK
    nsp = -(-n_sel // 128) * 128
    nq = t_len // Q_BLOCK
    nwin = WINDOW // Q_BLOCK + 1
    wblk = lambda r: (lambda i: jnp.maximum(i - (nwin - 1) + r, 0))
    win_specs = ([pl.BlockSpec((Q_BLOCK, KV_WIDTH), (lambda f: (lambda i: (f(i), 0)))(wblk(r))) for r in range(nwin)]
                 + [pl.BlockSpec((KV_WIDTH, Q_BLOCK), (lambda f: (lambda i: (0, f(i))))(wblk(r))) for r in range(nwin)])
    return pl.pallas_call(
        functools.partial(_nsa_prompt_kernel, n_sub=n_sub, nsp=nsp),
        grid=(nq,),
        in_specs=[pl.BlockSpec((Q_BLOCK, NSA_WIDTH), lambda i: (i, 0)),
                  pl.BlockSpec((Q_BLOCK, 128), lambda i: (i, 0)),
                  pl.BlockSpec((Q_BLOCK, NSA_WIDTH), lambda i: (i, COL_NZ // 2048)),
                  pl.BlockSpec((1, NSA_KV_HEADS, n_sub, HEAD_DIM), lambda i: (0, 0, 0, 0)),
                  pl.BlockSpec((1, NSA_KV_HEADS, n_sub, HEAD_DIM), lambda i: (0, 0, 0, 0)),
                  pl.BlockSpec((t_len, KV_WIDTH), lambda i: (0, 0), pipeline_mode=pl.Buffered(1)),
                  pl.BlockSpec((KV_WIDTH, t_len), lambda i: (0, 0), pipeline_mode=pl.Buffered(1))]
        + win_specs + [pl.BlockSpec((1, NSA_WIDTH), lambda i: (0, 0))],
        out_specs=pl.BlockSpec((Q_BLOCK, NSA_WIDTH), lambda i: (i, 0)),
        out_shape=jax.ShapeDtypeStruct((t_len, NSA_WIDTH), BF16),
        scratch_shapes=[pltpu.VMEM((Q_BLOCK, NSA_WIDTH), F32)],
        compiler_params=_cparams(("arbitrary",)),
        name="nsa_prompt",
    )(qn, gates, proj, kc, vc, selk, selvt, *([wink] * nwin), *([winvt] * nwin), nsa_g.reshape(1, NSA_WIDTH))


def _nsa_sample_kernel(pt_ref, *refs, n_sub, nsp, past_len, t_new):
    pages = refs[:PAGES_PER_TILE]
    (q_ref, gate_ref, nz_ref, kc_ref, vc_ref, newsel_ref, swin_ref, newwin_ref, og_ref,
     o_ref, qbd_ref, bias_ref, oc_ref, m_ref, l_ref, acc_ref, oacc_ref) = refs[PAGES_PER_TILE:]
    tau = pl.program_id(1)
    n_tau = pl.num_programs(1)
    tq = t_new
    G = NSA_KV_HEADS
    rows = NSA_GROUP * tq
    lanes = G * rows
    tpos = past_len + lax.broadcasted_iota(jnp.int32, (tq, 1), 0)
    tpos4 = jnp.concatenate([tpos] * NSA_GROUP, axis=0)
    lane_i = lax.broadcasted_iota(jnp.int32, (1, lanes), 1)

    def group_q():
        q = q_ref[...]
        return [jnp.concatenate([q[:, (NSA_GROUP * g + j) * HEAD_DIM:(NSA_GROUP * g + j + 1) * HEAD_DIM]
                                 for j in range(NSA_GROUP)], axis=0) for g in range(G)]

    def to_col(row_vec):
        eye = lax.broadcasted_iota(jnp.int32, (lanes, lanes), 0) == lax.broadcasted_iota(jnp.int32, (lanes, lanes), 1)
        return jnp.sum(jnp.where(eye, jnp.broadcast_to(row_vec, (lanes, lanes)), 0.0), axis=1, keepdims=True)

    def online(s, v_all, m, l, acc):
        m_new = jnp.maximum(m, jnp.max(s, axis=0, keepdims=True))
        alpha = jnp.exp2(m - m_new)
        p = jnp.exp2(s - m_new)
        l_new = alpha * l + jnp.sum(p, axis=0, keepdims=True)
        pv = _dot_tn(p, v_all)
        upd = jnp.concatenate([pv[g * rows:(g + 1) * rows, g * HEAD_DIM:(g + 1) * HEAD_DIM] for g in range(G)], axis=0)
        return m_new, l_new, to_col(alpha) * acc + upd

    @pl.when(tau == 0)
    def _():
        qgs = group_q()
        imps = []
        for g in range(G):
            oc, p = _cmp_branch(qgs[g].astype(BF16), kc_ref[0, g], vc_ref[0, g], tpos4, n_sub)
            oc_ref[g] = oc
            psum = p[0:tq]
            for j in range(1, NSA_GROUP):
                psum = psum + p[j * tq:(j + 1) * tq]
            imps.append(_importance(psum, n_sub, nsp))
        sel = _select(jnp.concatenate(imps, axis=0), tpos4, nsp)
        q_t = jnp.concatenate(qgs, axis=0).T
        qbd_ref[...] = jnp.concatenate(
            [jnp.where(lane_i // rows == g, q_t, 0.0) for g in range(G)], axis=0).astype(BF16)
        bias = jnp.where(sel, 0.0, NEG)
        bias = jnp.concatenate([bias[g * tq:(g + 1) * tq] for g in range(G) for _ in range(NSA_GROUP)], axis=0)
        for c in range(nsp // lanes):
            bias_ref[c * lanes:(c + 1) * lanes, :] = bias[:, c * lanes:(c + 1) * lanes].T
        m_ref[...] = jnp.full((1, lanes), NEG, F32)
        l_ref[...] = jnp.zeros((1, lanes), F32)
        acc_ref[...] = jnp.zeros((lanes, HEAD_DIM), F32)

    planes = [jnp.swapaxes(p[0], 0, 1) for p in pages]
    k_all = jnp.concatenate([jnp.concatenate([pl_[g] for g in range(G)], axis=1) for pl_ in planes], axis=0)
    v_all = jnp.concatenate([jnp.concatenate([pl_[G + g] for g in range(G)], axis=1) for pl_ in planes], axis=0)
    n_keys = PAGES_PER_TILE * PAGE
    n_blk = n_keys // SEL_BLOCK
    b_rows = bias_ref[pl.ds(pl.multiple_of(tau * n_blk, n_blk), n_blk), :]
    bias_keys = jnp.broadcast_to(b_rows[:, None, :], (n_blk, SEL_BLOCK, lanes)).reshape(n_keys, lanes)
    s = jnp.dot(k_all.astype(BF16), qbd_ref[...], preferred_element_type=F32) + bias_keys
    m, l, acc = online(s, v_all, m_ref[...], l_ref[...], acc_ref[...])
    m_ref[...] = m
    l_ref[...] = l
    acc_ref[...] = acc

    @pl.when(tau == n_tau - 1)
    def _():
        qgs = group_q()
        gates = gate_ref[...]
        n_buf = swin_ref.shape[1]
        kpos = past_len - n_buf + lax.broadcasted_iota(jnp.int32, (1, n_buf + tq), 1)
        last_blk = past_len // SEL_BLOCK
        pad = 16
        new = newsel_ref[...]
        zpad = jnp.zeros((pad - tq, G * HEAD_DIM), F32)
        kn = jnp.concatenate([jnp.concatenate([new[:, g, :] for g in range(G)], axis=1), zpad], axis=0)
        vn = jnp.concatenate([jnp.concatenate([new[:, G + g, :] for g in range(G)], axis=1), zpad], axis=0)
        krow = lax.broadcasted_iota(jnp.int32, (pad, 1), 0)
        s_new = jnp.dot(kn.astype(BF16), qbd_ref[...], preferred_element_type=F32) + bias_ref[last_blk:last_blk + 1, :]
        s_new = jnp.where((krow <= lane_i % tq) & (krow < tq), s_new, NEG)
        m, l, acc = online(s_new, vn, m_ref[...], l_ref[...], acc_ref[...])
        o_sel = acc * to_col(1.0 / l)
        swin = jnp.swapaxes(swin_ref[0], 0, 1)
        nwin = newwin_ref[...]
        for g in range(G):
            kw = jnp.concatenate([swin[g], nwin[:, g, :]], axis=0)
            vw = jnp.concatenate([swin[G + g], nwin[:, G + g, :]], axis=0)
            o_w = _window_branch(qgs[g], kw, vw, kpos, tpos4)
            o_g = (_gate_col(gates, g, 0, tq) * oc_ref[g] + _gate_col(gates, g, 1, tq) * o_sel[g * rows:(g + 1) * rows]
                   + _gate_col(gates, g, 2, tq) * o_w)
            for j in range(NSA_GROUP):
                hd = NSA_GROUP * g + j
                oacc_ref[:, hd * HEAD_DIM:(hd + 1) * HEAD_DIM] = o_g[j * tq:(j + 1) * tq]
        _finish(o_ref, oacc_ref, nz_ref[...], og_ref[...])


def _nsa_sample(cache, page_table, qn, gates, proj, kc, vc, rows_s, state_win, win_s, nsa_g, row0, t_new):
    nb, n_pages = page_table.shape
    past_len = n_pages * PAGE
    n_sub = kc.shape[2]
    n_sel = past_len // SEL_BLOCK + 1
    nsp = -(-n_sel // 128) * 128
    n_tau = n_pages // PAGES_PER_TILE
    rb = row0 // t_new
    n_buf = state_win.shape[1]
    rows = NSA_GROUP * t_new
    lanes = NSA_KV_HEADS * rows
    assert lanes == 128 and nsp % lanes == 0, "the sample kernel packs all query rows of a batch into one lane tile"
    cg = 2 * NSA_KV_HEADS

    def page_spec(r):
        return pl.BlockSpec((1, PAGE, cg, HEAD_DIM),
                            lambda b, i, pt: (pt[b, i * PAGES_PER_TILE + r], 0, 1, 0))

    grid_spec = pltpu.PrefetchScalarGridSpec(
        num_scalar_prefetch=1,
        grid=(nb, n_tau),
        in_specs=[page_spec(r) for r in range(PAGES_PER_TILE)] + [
            pl.BlockSpec((t_new, NSA_WIDTH), lambda b, i, pt: (b, 0)),
            pl.BlockSpec((t_new, 128), lambda b, i, pt: (b, 0)),
            pl.BlockSpec((t_new, NSA_WIDTH), lambda b, i, pt: (rb + b, COL_NZ // 2048)),
            pl.BlockSpec((1, NSA_KV_HEADS, n_sub, HEAD_DIM), lambda b, i, pt: (b, 0, 0, 0)),
            pl.BlockSpec((1, NSA_KV_HEADS, n_sub, HEAD_DIM), lambda b, i, pt: (b, 0, 0, 0)),
            pl.BlockSpec((t_new, cg, HEAD_DIM), lambda b, i, pt: (b, 1, 0)),
            pl.BlockSpec((1, n_buf, cg, HEAD_DIM), lambda b, i, pt: (b, 0, 0, 0)),
            pl.BlockSpec((t_new, cg, HEAD_DIM), lambda b, i, pt: (b, 0, 0)),
            pl.BlockSpec((1, NSA_WIDTH), lambda b, i, pt: (0, 0))],
        out_specs=pl.BlockSpec((t_new, NSA_WIDTH), lambda b, i, pt: (b, 0)),
        scratch_shapes=[pltpu.VMEM((NSA_KV_HEADS * HEAD_DIM, lanes), BF16),
                        pltpu.VMEM((nsp, lanes), F32),
                        pltpu.VMEM((NSA_KV_HEADS, rows, HEAD_DIM), F32),
                        pltpu.VMEM((1, lanes), F32),
                        pltpu.VMEM((1, lanes), F32),
                        pltpu.VMEM((lanes, HEAD_DIM), F32),
                        pltpu.VMEM((t_new, NSA_WIDTH), F32)],
    )
    return pl.pallas_call(
        functools.partial(_nsa_sample_kernel, n_sub=n_sub, nsp=nsp, past_len=past_len, t_new=t_new),
        grid_spec=grid_spec,
        out_shape=jax.ShapeDtypeStruct((nb * t_new, NSA_WIDTH), F32),
        compiler_params=_cparams(("arbitrary", "arbitrary")),
        name="nsa_sample",
    )(page_table, *([cache] * PAGES_PER_TILE), qn, gates, proj, kc, vc, rows_s, state_win, win_s,
      nsa_g.reshape(1, NSA_WIDTH))


def _out_kernel(oh_ref, on_ref, wh_ref, wn_ref, x_ref, gate_ref, y_ref):
    acc = _dot(oh_ref[...], wh_ref[...]) + _dot(on_ref[...], wn_ref[...])
    y_ref[...] = x_ref[...] + gate_ref[...] * acc


def _out_proj(o_h, o_n, w_out, x, gate, tm, tn=512):
    m, d = x.shape
    half = o_h.shape[1]
    grow = gate.shape[0]
    gmap = (lambda i, j: (0, j)) if grow == 1 else (lambda i, j: (i, j))
    return pl.pallas_call(
        _out_kernel,
        grid=(m // tm, d // tn),
        in_specs=[pl.BlockSpec((tm, half), lambda i, j: (i, 0)),
                  pl.BlockSpec((tm, half), lambda i, j: (i, 0)),
                  pl.BlockSpec((half, tn), lambda i, j: (0, j)),
                  pl.BlockSpec((half, tn), lambda i, j: (1, j)),
                  pl.BlockSpec((tm, tn), lambda i, j: (i, j)),
                  pl.BlockSpec((1 if grow == 1 else tm, tn), gmap)],
        out_specs=pl.BlockSpec((tm, tn), lambda i, j: (i, j)),
        out_shape=jax.ShapeDtypeStruct((m, d), F32),
        compiler_params=_cparams(("arbitrary", "arbitrary")),
        name="out_proj",
    )(o_h, o_n, w_out, w_out, x, gate)


def _layer(xp, xs, c_all, cache, s_win, s_hgrn, page_table, lb_param, layer, norm_g, w_ada, b_ada, w_in,
           hgrn_out_g, q_g, k_g, pe, w1, w2, nsa_out_g, w_out):
    t_len = xp.shape[0]
    nb, t_new, _ = xs.shape
    n_s = nb * t_new
    d = D_MODEL

    mod = _ada(c_all, w_ada, b_ada)
    shift, scale, gate = mod[:, :d], mod[:, d:2 * d], mod[:, 2 * d:]
    rep = lambda a: jnp.pad(jnp.repeat(a[1:1 + nb], t_new, axis=0), ((0, SAMPLE_ROWS - n_s), (0, 0)))
    xs_pad = jnp.pad(xs.reshape(n_s, d), ((0, SAMPLE_ROWS - n_s), (0, 0)))
    h_all = _norm(xp, xs_pad, norm_g.reshape(1, d), scale[0:1], shift[0:1], rep(scale), rep(shift))

    proj = _matmul(h_all, jnp.swapaxes(w_in, 0, 1))

    oh_p, st_p = _hgrn(proj, lb_param, hgrn_out_g, None, row0=0, n_batch=1, t_len=t_len,
                       chunk=min(64, t_len), out_dtype=BF16, layer=layer)
    oh_s, st_s = _hgrn(proj, lb_param, hgrn_out_g, s_hgrn, row0=t_len, n_batch=nb, t_len=t_new,
                       chunk=t_new, out_dtype=F32, layer=layer)

    qn_p, rows_p, win_p3, selk_p, selvt_p, wink_p, winvt_p, gates_p = _prep(proj, q_g, k_g, 0, t_len)
    qn_s, rows_s, win_s3, _, _, _, _, gates_s = _prep(proj, q_g, k_g, t_len, SAMPLE_ROWS)

    w1r = w1.reshape(2, 2, CMP_STRIDE * HEAD_DIM, CMP_HIDDEN)
    wc = jnp.concatenate([w1r[:, 0], w1r[:, 1]], axis=-1).astype(BF16)
    w1f = w1.reshape(2, CMP_LEN * HEAD_DIM, CMP_HIDDEN).astype(BF16)
    pe_flat = pe.reshape(2, 1, CMP_LEN * HEAD_DIM)
    w2b = w2.astype(BF16)

    n_pp = t_len // PAGE
    kc_p, vc_p = _compress(rows_p.reshape(n_pp, PAGE, 4 * NSA_KV_HEADS, HEAD_DIM),
                           jnp.arange(n_pp, dtype=jnp.int32).reshape(1, n_pp), wc, w1f, pe_flat, w2b, k_g[0])
    cache4 = cache.reshape(cache.shape[0], PAGE, 4 * NSA_KV_HEADS, HEAD_DIM)
    kc_s, vc_s = _compress(cache4, page_table, wc, w1f, pe_flat, w2b, k_g[0])

    on_p = _nsa_prompt(qn_p, gates_p, proj, kc_p, vc_p, selk_p, selvt_p, wink_p, winvt_p, nsa_out_g, t_len)
    n_buf = s_win.shape[1]
    s_win4 = s_win.reshape(nb, n_buf, 2 * NSA_KV_HEADS, HEAD_DIM)
    on_s = _nsa_sample(cache4, page_table, qn_s, gates_s, proj, kc_s, vc_s, rows_s, s_win4, win_s3,
                       nsa_out_g, t_len, t_new)

    y_p = _out_proj(oh_p, on_p, w_out, xp, gate[0:1], tm=min(1024, t_len))
    y_s = _out_proj(oh_s, on_s, w_out, xs.reshape(n_s, d), jnp.repeat(gate[1:1 + nb], t_new, axis=0), tm=n_s)

    n_w = min(WINDOW, t_len)
    kv_p = rows_p.reshape(1, t_len, 4, NSA_KV_HEADS, HEAD_DIM)
    kv_s = rows_s[:n_s].reshape(nb, t_new, 4, NSA_KV_HEADS, HEAD_DIM)
    win_p = win_p3[t_len - n_w:].reshape(1, n_w, 2, NSA_KV_HEADS, HEAD_DIM)
    win_new = win_s3[:n_s].reshape(nb, t_new, 2, NSA_KV_HEADS, HEAD_DIM)
    win_s = jnp.concatenate([s_win, win_new], axis=1)[:, -n_buf:]
    return y_p, y_s.reshape(nb, t_new, d), kv_p, kv_s, win_p, win_s, st_p, st_s


def kernel(x_prompt, x_sample, cache_kv, state_win, state_hgrn, page_table, c_prompt, c_sample, norm_g, w_ada,
           b_ada, w_in, hgrn_lb, hgrn_out_g, q_norm_g, k_norm_g, cmp_pe, cmp_w1, cmp_w2, nsa_out_g, w_out):
    depth = w_in.shape[0]
    assert depth == 1 and x_prompt.shape[0] == 1
    nb = x_sample.shape[0]
    c_all = jnp.concatenate([c_prompt, c_sample], axis=0)
    c_all = jnp.pad(c_all, ((0, (-c_all.shape[0]) % 8), (0, 0)))
    l = 0
    outs = _layer(x_prompt[0], x_sample, c_all, cache_kv[l], state_win[l], state_hgrn[l], page_table,
                  hgrn_lb, l, norm_g[l], w_ada[l], b_ada[l], w_in[l], hgrn_out_g[l], q_norm_g[l], k_norm_g[l],
                  cmp_pe[l], cmp_w1[l], cmp_w2[l], nsa_out_g[l], w_out[l])
    y_p, y_s, kv_p, kv_s, win_p, win_s, st_p, st_s = outs
    return (y_p[None], y_s, kv_p[None], kv_s[None], win_p[None], win_s[None], st_p[None], st_s[None])
```

```python
import functools

import jax
import jax.numpy as jnp
from jax import lax
from jax.experimental import pallas as pl
from jax.experimental.pallas import tpu as pltpu

F32 = jnp.float32
BF16 = jnp.bfloat16

D_MODEL = 4096
HEAD_DIM = 128
HGRN_WIDTH = 2048
NSA_WIDTH = 2048
HGRN_HEADS = 16
NSA_HEADS = 16
NSA_KV_HEADS = 4
NSA_GROUP = 4
KV_WIDTH = 512
CMP_LEN = 32
CMP_STRIDE = 16
CMP_HIDDEN = 256
SEL_BLOCK = 64
SEL_TOP = 16
WINDOW = 512
Q_BLOCK = 128
NORM_EPS = 1e-6
PAGE = 128

COL_HQ, COL_HF, COL_HI, COL_HZ, COL_NQ, COL_ROWS, COL_NZ = (i * 2048 for i in range(7))
COL_WIN = 14336
COL_NG = 15360
N_PROJ = 15872
SAMPLE_ROWS = 256
ROW_TILE = 256
PAGES_PER_TILE = 8
CMP_PAGES = 32
KV_TILE = 512
NEG = -1e30
LOG2E = 1.4426950408889634
VMEM_LIMIT = 56 * 1024 * 1024


def _cparams(sem):
    return pltpu.CompilerParams(dimension_semantics=sem, vmem_limit_bytes=VMEM_LIMIT)


def _dot(a, b):
    return jnp.dot(a.astype(BF16), b.astype(BF16), preferred_element_type=F32)


def _dot_nt(a, b):
    return lax.dot_general(a.astype(BF16), b.astype(BF16), (((1,), (1,)), ((), ())),
                           preferred_element_type=F32)


def _dot_tn(a, b):
    return lax.dot_general(a.astype(BF16), b.astype(BF16), (((0,), (0,)), ((), ())),
                           preferred_element_type=F32)


def _silu(x):
    return x * jax.nn.sigmoid(x)


def _ada_kernel(c_ref, w_ref, b_ref, o_ref):
    c = c_ref[...]
    o_ref[...] = _dot(_silu(c), w_ref[...]) + b_ref[...]


def _ada(c_all, w_ada, b_ada):
    m, d = c_all.shape
    n = w_ada.shape[1]
    tn = 512
    return pl.pallas_call(
        _ada_kernel,
        grid=(n // tn,),
        in_specs=[pl.BlockSpec((m, d), lambda j: (0, 0)),
                  pl.BlockSpec((d, tn), lambda j: (0, j)),
                  pl.BlockSpec((1, tn), lambda j: (0, j))],
        out_specs=pl.BlockSpec((m, tn), lambda j: (0, j)),
        out_shape=jax.ShapeDtypeStruct((m, n), F32),
        compiler_params=_cparams(("arbitrary",)),
        name="ada",
    )(c_all, w_ada, b_ada.reshape(1, n))


def _norm_kernel(xp_ref, xs_ref, g_ref, scp_ref, shp_ref, scs_ref, shs_ref, o_ref, *, n_prompt):
    i = pl.program_id(0)

    def body(x, scale, shift):
        r = x * lax.rsqrt(jnp.mean(x * x, axis=-1, keepdims=True) + NORM_EPS) * g_ref[...]
        o_ref[...] = (r * (1.0 + scale) + shift).astype(o_ref.dtype)

    @pl.when(i < n_prompt)
    def _():
        body(xp_ref[...], scp_ref[...], shp_ref[...])

    @pl.when(i >= n_prompt)
    def _():
        body(xs_ref[...], scs_ref[...], shs_ref[...])


def _norm(xp, xs, g, sc_p, sh_p, sc_s, sh_s):
    t, d = xp.shape
    n_prompt = t // ROW_TILE
    row = lambda i: (0, 0)
    return pl.pallas_call(
        functools.partial(_norm_kernel, n_prompt=n_prompt),
        grid=(n_prompt + 1,),
        in_specs=[pl.BlockSpec((ROW_TILE, d), lambda i: (jnp.minimum(i, n_prompt - 1), 0)),
                  pl.BlockSpec((SAMPLE_ROWS, d), row),
                  pl.BlockSpec((1, d), row), pl.BlockSpec((1, d), row), pl.BlockSpec((1, d), row),
                  pl.BlockSpec((SAMPLE_ROWS, d), row), pl.BlockSpec((SAMPLE_ROWS, d), row)],
        out_specs=pl.BlockSpec((ROW_TILE, d), lambda i: (i, 0)),
        out_shape=jax.ShapeDtypeStruct((t + SAMPLE_ROWS, d), BF16),
        compiler_params=_cparams(("arbitrary",)),
        name="norm",
    )(xp, xs, g, sc_p, sh_p, sc_s, sh_s)


PROJ_TN = 512
_N_MAIN_A, _N_NZ, _N_WIN = 24, 4, 2
_ROW_WIN = _N_MAIN_A * PROJ_TN
_ROW_NG = _ROW_WIN + _N_WIN * PROJ_TN
_ROW_NZ = _ROW_NG + 3 * NSA_HEADS


_ROW_UNIT = 16


def _proj_weight_row(j):
    u = PROJ_TN // _ROW_UNIT
    q = jnp.where(j < _N_MAIN_A, j * u,
                  jnp.where(j < _N_MAIN_A + _N_NZ, _ROW_NZ // _ROW_UNIT + (j - _N_MAIN_A) * u,
                            jnp.where(j < _N_MAIN_A + _N_NZ + _N_WIN,
                                      _ROW_WIN // _ROW_UNIT + (j - _N_MAIN_A - _N_NZ) * u, _ROW_NG // _ROW_UNIT)))
    return q * _ROW_UNIT


def _mm_kernel(a_ref, wt_ref, o_ref, wb_ref):
    @pl.when(pl.program_id(1) == 0)
    def _():
        wb_ref[...] = wt_ref[...].astype(BF16)

    o_ref[...] = lax.dot_general(a_ref[...], wb_ref[...], (((1,), (1,)), ((), ())), preferred_element_type=F32)


def _row_tile(m, cap=1100):
    best = 16
    for tm in range(16, cap + 1, 16):
        if m % tm == 0:
            best = tm
    return best


def _matmul(a, w_t):
    m, k = a.shape
    tn = PROJ_TN
    tm = _row_tile(m)
    return pl.pallas_call(
        _mm_kernel,
        grid=(N_PROJ // tn, m // tm),
        in_specs=[pl.BlockSpec((tm, k), lambda j, i: (i, 0)),
                  pl.BlockSpec((pl.Element(tn), pl.Element(k)), lambda j, i: (_proj_weight_row(j), 0))],
        out_specs=pl.BlockSpec((tm, tn), lambda j, i: (i, j)),
        out_shape=jax.ShapeDtypeStruct((m, N_PROJ), F32),
        scratch_shapes=[pltpu.VMEM((tn, k), BF16)],
        compiler_params=_cparams(("arbitrary", "arbitrary")),
        name="proj",
    )(a, w_t)


def _hgrn_kernel(*refs, chunk, levels, has_s0, layer):
    if has_s0:
        lbp_ref, og_ref, hq_ref, hf_ref, hi_ref, hz_ref, s0_ref, o_ref, sout_ref, st_ref = refs
    else:
        lbp_ref, og_ref, hq_ref, hf_ref, hi_ref, hz_ref, o_ref, sout_ref, st_ref = refs
        s0_ref = None
    C = chunk
    c = pl.program_id(1)
    n_chunks = pl.num_programs(1)

    @pl.when(c == 0)
    def _():
        for h in range(HGRN_HEADS):
            if has_s0:
                st_ref[h] = s0_ref[0, h].T
            else:
                st_ref[h] = jnp.zeros((HEAD_DIM, HEAD_DIM), F32)

    lbp = lbp_ref[...]
    e = jnp.exp(lbp - jnp.max(lbp, axis=0, keepdims=True))
    lb = jnp.sum(e[:layer + 1], axis=0, keepdims=True) / jnp.sum(e, axis=0, keepdims=True)

    f = lb + (1.0 - lb) * jax.nn.sigmoid(hf_ref[...])
    logf = jnp.log(f)
    k = 1.0 - f
    q = hq_ref[...]
    v = hi_ref[...]
    hz = hz_ref[...]
    n = q.shape[1]

    row = lax.broadcasted_iota(jnp.int32, (C, 1), 0)
    cum = logf
    s = 1
    while s < C:
        cum = cum + jnp.where(row >= s, pltpu.roll(cum, s, axis=0), 0.0)
        s *= 2
    cum = cum * LOG2E
    last = cum[C - 1:C, :]
    qe = q * jnp.exp2(cum)
    kd = k * jnp.exp2(last - cum)
    elast = jnp.exp2(last)

    lvl = []
    for h in levels:
        np_ = C // (2 * h)
        c4 = cum.reshape(np_, 2, h, n)
        cmid = c4[:, 0, h - 1:h, :]
        zero = jnp.zeros((np_, 1, h, n), F32)
        qr = q.reshape(np_, 2, h, n)[:, 1] * jnp.exp2(c4[:, 1] - cmid)
        kl_ = k.reshape(np_, 2, h, n)[:, 0] * jnp.exp2(cmid - c4[:, 0])
        ql = jnp.concatenate([zero, qr[:, None]], axis=1).reshape(C, n)
        kl = jnp.concatenate([kl_[:, None], zero], axis=1).reshape(C, n)
        ti = lax.broadcasted_iota(jnp.int32, (C, C), 0)
        si = lax.broadcasted_iota(jnp.int32, (C, C), 1)
        mask = ((ti // (2 * h)) == (si // (2 * h))) & (((ti // h) % 2) == 1) & (((si // h) % 2) == 0)
        lvl.append((ql, kl, mask))

    nb = C // 8
    rin = lax.broadcasted_iota(jnp.int32, (nb, 8, 1), 1)
    og = og_ref[...]
    for h in range(HGRN_HEADS):
        sl = slice(h * HEAD_DIM, (h + 1) * HEAD_DIM)
        st = st_ref[h]
        q_h, k_h, v_h, cum_h = q[:, sl], k[:, sl], v[:, sl], cum[:, sl]
        o_h = _dot_nt(qe[:, sl], st)
        if levels:
            a = jnp.zeros((C, C), F32)
            for ql, kl, mask in lvl:
                a = a + jnp.where(mask, _dot_nt(ql[:, sl], kl[:, sl]), 0.0)
            o_h = o_h + _dot(a, v_h)
        q3 = q_h.reshape(nb, 8, HEAD_DIM)
        k3 = k_h.reshape(nb, 8, HEAD_DIM)
        v3 = v_h.reshape(nb, 8, HEAD_DIM)
        c3 = cum_h.reshape(nb, 8, HEAD_DIM)
        od = jnp.zeros((nb, 8, HEAD_DIM), F32)
        for s in range(8):
            dec = jnp.exp2(jnp.where(rin >= s, c3 - c3[:, s:s + 1, :], -jnp.inf))
            w = q3 * k3[:, s:s + 1, :] * dec
            od = od + jnp.sum(w, axis=-1, keepdims=True) * v3[:, s:s + 1, :]
        o_h = o_h + od.reshape(C, HEAD_DIM)
        st_ref[h] = st * elast[:, sl] + _dot_tn(v_h, kd[:, sl])
        o_h = o_h * lax.rsqrt(jnp.mean(o_h * o_h, axis=-1, keepdims=True) + NORM_EPS) * og
        o_ref[:, sl] = (o_h * _silu(hz[:, sl])).astype(o_ref.dtype)

    @pl.when(c == n_chunks - 1)
    def _():
        for h in range(HGRN_HEADS):
            sout_ref[0, h] = st_ref[h].T


def _hgrn(proj, lb_param, out_g, s0, *, row0, n_batch, t_len, chunk, out_dtype, layer):
    n_chunks = t_len // chunk
    levels = []
    h = chunk // 2
    while h >= 8:
        levels.append(h)
        h //= 2
    blk0 = row0 // chunk
    rmap = lambda col: (lambda b, c: (blk0 + b * n_chunks + c, col))
    in_specs = [pl.BlockSpec(lb_param.shape, lambda b, c: (0, 0)),
                pl.BlockSpec((1, HEAD_DIM), lambda b, c: (0, 0)),
                pl.BlockSpec((chunk, HGRN_WIDTH), rmap(0)),
                pl.BlockSpec((chunk, HGRN_WIDTH), rmap(1)),
                pl.BlockSpec((chunk, HGRN_WIDTH), rmap(2)),
                pl.BlockSpec((chunk, HGRN_WIDTH), rmap(3))]
    args = [lb_param, out_g.reshape(1, HEAD_DIM), proj, proj, proj, proj]
    if s0 is not None:
        in_specs.append(pl.BlockSpec((1, HGRN_HEADS, HEAD_DIM, HEAD_DIM), lambda b, c: (b, 0, 0, 0)))
        args.append(s0)
    return pl.pallas_call(
        functools.partial(_hgrn_kernel, chunk=chunk, levels=tuple(levels), has_s0=s0 is not None, layer=layer),
        grid=(n_batch, n_chunks),
        in_specs=in_specs,
        out_specs=[pl.BlockSpec((chunk, HGRN_WIDTH), lambda b, c: (b * n_chunks + c, 0)),
                   pl.BlockSpec((1, HGRN_HEADS, HEAD_DIM, HEAD_DIM), lambda b, c: (b, 0, 0, 0))],
        out_shape=[jax.ShapeDtypeStruct((n_batch * t_len, HGRN_WIDTH), out_dtype),
                   jax.ShapeDtypeStruct((n_batch, HGRN_HEADS, HEAD_DIM, HEAD_DIM), F32)],
        scratch_shapes=[pltpu.VMEM((HGRN_HEADS, HEAD_DIM, HEAD_DIM), F32)],
        compiler_params=_cparams(("arbitrary", "arbitrary")),
        name="hgrn",
    )(*args)


def _head_rms(x, g, n_heads):
    outs = []
    for h in range(n_heads):
        xh = x[:, h * HEAD_DIM:(h + 1) * HEAD_DIM]
        outs.append(xh * lax.rsqrt(jnp.mean(xh * xh, axis=-1, keepdims=True) + NORM_EPS) * g)
    return jnp.concatenate(outs, axis=1)


def _prep_kernel(nq_ref, rows_ref, win_ref, ng_ref, qg_ref, kg_ref,
                 q_ref, rows_o, win_o, selk_o, selvt_o, wink_o, winvt_o, gate_o):
    kg = kg_ref[...]
    q_ref[...] = _head_rms(nq_ref[...], qg_ref[...], NSA_HEADS) * (HEAD_DIM ** -0.5 * LOG2E)
    rows = rows_ref[...]
    ks = _head_rms(rows[:, 2 * KV_WIDTH:3 * KV_WIDTH], kg[1:2], NSA_KV_HEADS)
    for cg in range(4 * NSA_KV_HEADS):
        sl = slice(cg * HEAD_DIM, (cg + 1) * HEAD_DIM)
        if 2 * NSA_KV_HEADS <= cg < 3 * NSA_KV_HEADS:
            rows_o[:, cg, :] = ks[:, (cg - 2 * NSA_KV_HEADS) * HEAD_DIM:(cg - 2 * NSA_KV_HEADS + 1) * HEAD_DIM]
        else:
            rows_o[:, cg, :] = rows[:, sl]
    selk_o[...] = ks.astype(BF16)
    selvt_o[...] = rows[:, 3 * KV_WIDTH:].T.astype(BF16)
    win = win_ref[...]
    kw = _head_rms(win[:, :KV_WIDTH], kg[2:3], NSA_KV_HEADS)
    for cg in range(2 * NSA_KV_HEADS):
        sl = slice(cg * HEAD_DIM, (cg + 1) * HEAD_DIM)
        win_o[:, cg, :] = kw[:, sl] if cg < NSA_KV_HEADS else win[:, sl]
    wink_o[...] = kw.astype(BF16)
    winvt_o[...] = win[:, KV_WIDTH:].T.astype(BF16)
    gate_o[...] = jax.nn.sigmoid(ng_ref[...])


def _prep(proj, q_g, k_g, row0, n_rows):
    t = min(ROW_TILE, n_rows)
    r0 = row0 // t
    return pl.pallas_call(
        _prep_kernel,
        grid=(n_rows // t,),
        in_specs=[pl.BlockSpec((t, NSA_WIDTH), lambda i: (r0 + i, COL_NQ // 2048)),
                  pl.BlockSpec((t, 2048), lambda i: (r0 + i, COL_ROWS // 2048)),
                  pl.BlockSpec((t, 1024), lambda i: (r0 + i, COL_WIN // 1024)),
                  pl.BlockSpec((t, 128), lambda i: (r0 + i, COL_NG // 128)),
                  pl.BlockSpec((1, HEAD_DIM), lambda i: (0, 0)),
                  pl.BlockSpec((3, HEAD_DIM), lambda i: (0, 0))],
        out_specs=[pl.BlockSpec((t, 2048), lambda i: (i, 0)),
                   pl.BlockSpec((t, 4 * NSA_KV_HEADS, HEAD_DIM), lambda i: (i, 0, 0)),
                   pl.BlockSpec((t, 2 * NSA_KV_HEADS, HEAD_DIM), lambda i: (i, 0, 0)),
                   pl.BlockSpec((t, KV_WIDTH), lambda i: (i, 0)),
                   pl.BlockSpec((KV_WIDTH, t), lambda i: (0, i)),
                   pl.BlockSpec((t, KV_WIDTH), lambda i: (i, 0)),
                   pl.BlockSpec((KV_WIDTH, t), lambda i: (0, i)),
                   pl.BlockSpec((t, 128), lambda i: (i, 0))],
        out_shape=[jax.ShapeDtypeStruct((n_rows, 2048), F32),
                   jax.ShapeDtypeStruct((n_rows, 4 * NSA_KV_HEADS, HEAD_DIM), F32),
                   jax.ShapeDtypeStruct((n_rows, 2 * NSA_KV_HEADS, HEAD_DIM), F32),
                   jax.ShapeDtypeStruct((n_rows, KV_WIDTH), BF16),
                   jax.ShapeDtypeStruct((KV_WIDTH, n_rows), BF16),
                   jax.ShapeDtypeStruct((n_rows, KV_WIDTH), BF16),
                   jax.ShapeDtypeStruct((KV_WIDTH, n_rows), BF16),
                   jax.ShapeDtypeStruct((n_rows, 128), F32)],
        compiler_params=_cparams(("arbitrary",)),
        name="prep",
    )(proj, proj, proj, proj, q_g.reshape(1, HEAD_DIM), k_g)


def _compress_kernel(pt_ref, *refs, n_pg):
    pages = refs[:n_pg]
    wc_ref, pe_ref, w2_ref, kcg_ref, kc_ref, vc_ref, cvec_ref, carry_ref, out_scr = refs[n_pg:]
    b = pl.program_id(0)
    i = pl.program_id(1)
    nsub = PAGE // CMP_STRIDE
    ntile = nsub * n_pg
    G = NSA_KV_HEADS
    rows = ntile * G

    @pl.when((b == 0) & (i == 0))
    def _():
        for ch in range(2):
            lo = jnp.broadcast_to(pe_ref[ch, 0], (8, CMP_STRIDE * HEAD_DIM))
            hi = jnp.broadcast_to(pe_ref[ch, 1], (8, CMP_STRIDE * HEAD_DIM))
            cvec_ref[ch] = _dot(lo, wc_ref[ch][:, :CMP_HIDDEN]) + _dot(hi, wc_ref[ch][:, CMP_HIDDEN:])

    @pl.when(i == 0)
    def _():
        carry_ref[...] = jnp.zeros(carry_ref.shape, F32)

    def gathered(p):
        lo, hi = [], []
        for r in range(n_pg):
            a = pages[r][0, pl.ds(p, nsub // 2, stride=2 * CMP_STRIDE), :, :]
            bb = pages[r][0, pl.ds(p + CMP_STRIDE, nsub // 2, stride=2 * CMP_STRIDE), :, :]
            lo.append(jnp.concatenate([a[:, :G], bb[:, :G]], axis=1).reshape(nsub * G, HEAD_DIM))
            hi.append(jnp.concatenate([a[:, G:], bb[:, G:]], axis=1).reshape(nsub * G, HEAD_DIM))
        return jnp.concatenate(lo, axis=0).astype(BF16), jnp.concatenate(hi, axis=0).astype(BF16)

    xs = [gathered(p) for p in range(CMP_STRIDE)]
    u = [jnp.dot(jnp.concatenate([x[ch] for x in xs], axis=1), wc_ref[ch], preferred_element_type=F32)
         for ch in range(2)]

    row8 = lax.broadcasted_iota(jnp.int32, (8, 1), 0)
    for ch in range(2):
        u0, u1 = u[ch][:, :CMP_HIDDEN], u[ch][:, CMP_HIDDEN:]
        rolled = pltpu.roll(u0, G, axis=0)
        head = jnp.where(row8 < G, carry_ref[ch], rolled[0:8])
        carry_ref[ch] = rolled[0:8]
        prev = jnp.concatenate([head, rolled[8:]], axis=0)
        pre = prev + u1 + cvec_ref[ch][0:1, :]
        out = _dot(_silu(pre), w2_ref[ch])
        if ch == 0:
            out = out * lax.rsqrt(jnp.mean(out * out, axis=-1, keepdims=True) + NORM_EPS) * kcg_ref[...]
        out_scr[...] = out
        dst = kc_ref if ch == 0 else vc_ref
        for g in range(G):
            dst[0, g] = out_scr[pl.ds(g, ntile, stride=G), :].astype(dst.dtype)


def _compress(src, page_table, wc, pe2, w2, kc_g):
    nb, n_pages = page_table.shape
    n_pg = CMP_PAGES if n_pages % CMP_PAGES == 0 else PAGES_PER_TILE
    n_tiles = n_pages // n_pg
    nsub = PAGE // CMP_STRIDE
    ntile = nsub * n_pg

    def page_spec(r):
        return pl.BlockSpec((1, PAGE, 2 * NSA_KV_HEADS, HEAD_DIM),
                            lambda b, i, pt: (pt[b, i * n_pg + r], 0, 0, 0))

    const = lambda *shape: pl.BlockSpec(shape, lambda b, i, pt: (0,) * len(shape))
    out_spec = pl.BlockSpec((1, NSA_KV_HEADS, ntile, HEAD_DIM), lambda b, i, pt: (b, 0, i, 0))
    grid_spec = pltpu.PrefetchScalarGridSpec(
        num_scalar_prefetch=1,
        grid=(nb, n_tiles),
        in_specs=[page_spec(r) for r in range(n_pg)] + [
            const(2, CMP_STRIDE * HEAD_DIM, 2 * CMP_HIDDEN),
            const(2, 2, 1, CMP_STRIDE * HEAD_DIM),
            const(2, CMP_HIDDEN, HEAD_DIM),
            const(1, HEAD_DIM)],
        out_specs=[out_spec, out_spec],
        scratch_shapes=[pltpu.VMEM((2, 8, CMP_HIDDEN), F32),
                        pltpu.VMEM((2, 8, CMP_HIDDEN), F32),
                        pltpu.VMEM((NSA_KV_HEADS * ntile, HEAD_DIM), F32)],
    )
    shape = jax.ShapeDtypeStruct((nb, NSA_KV_HEADS, n_tiles * ntile, HEAD_DIM), BF16)
    return pl.pallas_call(
        functools.partial(_compress_kernel, n_pg=n_pg),
        grid_spec=grid_spec,
        out_shape=[shape, shape],
        compiler_params=_cparams(("arbitrary", "arbitrary")),
        name="compress",
    )(page_table, *([src] * n_pg), wc, pe2, w2, kc_g.reshape(1, HEAD_DIM))


def _split3(x):
    hi = x.astype(BF16)
    r = x - hi.astype(F32)
    mid = r.astype(BF16)
    lo = (r - mid.astype(F32)).astype(BF16)
    return hi, mid, lo


def _cmp_branch(qg, kc, vc, tpos4, n_sub):
    s = _dot_nt(qg, kc)
    npr = lax.broadcasted_iota(jnp.int32, (1, n_sub), 1)
    valid = (npr >= 1) & (npr * CMP_STRIDE + (CMP_STRIDE - 1) <= tpos4)
    s = jnp.where(valid, s, NEG)
    m = jnp.max(s, axis=-1, keepdims=True)
    p = jnp.where(valid, jnp.exp2(s - m), 0.0)
    l = jnp.sum(p, axis=-1, keepdims=True)
    p = p * jnp.where(l > 0.0, 1.0 / l, 0.0)
    return _dot(p, vc), p


def _softmax_cols(s, mask):
    s = jnp.where(mask, s, NEG)
    m = jnp.maximum(jnp.max(s, axis=0, keepdims=True), 0.1 * NEG)
    p = jnp.exp2(s - m)
    l = jnp.sum(p, axis=0, keepdims=True)
    return p, jnp.where(l > 0.0, 1.0 / l, 0.0)


def _select_t(imp, tpos, nsp):
    sb = lax.broadcasted_iota(jnp.int32, (nsp, 1), 0)
    cur = tpos // SEL_BLOCK
    causal = sb * SEL_BLOCK <= tpos
    forced = (sb == 0) | (sb == cur) | (sb == cur - 1)
    score = jnp.where(causal, jnp.where(forced, -NEG, imp), NEG)
    sel = jnp.zeros(score.shape, jnp.bool_)
    sbf = sb.astype(F32)
    for _ in range(SEL_TOP):
        m = jnp.max(score, axis=0, keepdims=True)
        idx = jnp.min(jnp.where(score == m, sbf, float(nsp)), axis=0, keepdims=True)
        hit = sbf == idx
        sel = sel | hit
        score = jnp.where(hit, NEG, score)
    return sel & causal


def _importance(psum, n_sub, nsp):
    npr = lax.broadcasted_iota(jnp.int32, (n_sub, nsp), 0)
    sb = lax.broadcasted_iota(jnp.int32, (n_sub, nsp), 1)
    r = SEL_BLOCK // CMP_STRIDE
    ov = ((npr >= r * sb) & (npr <= r * sb + r) & (npr >= 1)).astype(BF16)
    hi, mid, lo = _split3(psum)
    dot = lambda a: jnp.dot(a, ov, preferred_element_type=F32)
    return dot(hi) + dot(mid) + dot(lo)


def _select(imp, tpos, nsp):
    sb = lax.broadcasted_iota(jnp.int32, (1, nsp), 1)
    cur = tpos // SEL_BLOCK
    causal = sb * SEL_BLOCK <= tpos
    forced = (sb == 0) | (sb == cur) | (sb == cur - 1)
    score = jnp.where(causal, jnp.where(forced, -NEG, imp), NEG)
    sel = jnp.zeros(score.shape, jnp.bool_)
    sbf = sb.astype(F32)
    for _ in range(SEL_TOP):
        m = jnp.max(score, axis=-1, keepdims=True)
        idx = jnp.min(jnp.where(score == m, sbf, float(nsp)), axis=-1, keepdims=True)
        hit = sbf == idx
        sel = sel | hit
        score = jnp.where(hit, NEG, score)
    return sel & causal


def _online_step(carry, s, vt):
    m, l, acc = carry
    m_new = jnp.maximum(m, jnp.max(s, axis=-1, keepdims=True))
    alpha = jnp.exp2(m - m_new)
    p = jnp.exp2(s - m_new)
    l = alpha * l + jnp.sum(p, axis=-1, keepdims=True)
    acc = alpha * acc + _dot(p, vt)
    return m_new, l, acc


def _block_onehot(key0, n_keys, nsp):
    shift = SEL_BLOCK.bit_length() - 1
    kb = jnp.right_shift(lax.broadcasted_iota(jnp.int32, (n_keys, 1), 0), shift)
    sb = lax.broadcasted_iota(jnp.int32, (1, nsp), 1) - jnp.right_shift(key0, shift)
    return (kb == sb).astype(BF16)


def _window_branch(qg, kw, vw, kpos, tpos4):
    s = _dot_nt(qg, kw)
    mask = (kpos <= tpos4) & (kpos >= tpos4 - WINDOW) & (kpos >= 0)
    s = jnp.where(mask, s, NEG)
    m = jnp.max(s, axis=-1, keepdims=True)
    p = jnp.where(mask, jnp.exp2(s - m), 0.0)
    l = jnp.sum(p, axis=-1, keepdims=True)
    return _dot(p, vw) * jnp.where(l > 0.0, 1.0 / l, 0.0)


def _gate_col(gates, g, br, tq):
    return jnp.concatenate(
        [gates[:, 3 * (NSA_GROUP * g + j) + br:3 * (NSA_GROUP * g + j) + br + 1] for j in range(NSA_GROUP)], axis=0)


def _finish(o_ref, oacc_ref, nz, og):
    o = oacc_ref[...]
    o = o * lax.rsqrt(jnp.mean(o * o, axis=-1, keepdims=True) + NORM_EPS) * og
    o_ref[...] = (o * _silu(nz)).astype(o_ref.dtype)


def _nsa_prompt_kernel(q_ref, gate_ref, nz_ref, kc_ref, vc_ref, selk_ref, selvt_ref, *rest, n_sub, nsp):
    nwin = WINDOW // Q_BLOCK + 1
    wink_refs = rest[:nwin]
    winvt_refs = rest[nwin:2 * nwin]
    og_ref, o_ref, oacc_ref = rest[2 * nwin:]
    i = pl.program_id(0)
    tq = Q_BLOCK
    lanes = NSA_GROUP * tq
    t0 = i * tq
    tpos = t0 + lax.broadcasted_iota(jnp.int32, (1, tq), 1)
    tpos4 = jnp.concatenate([tpos] * NSA_GROUP, axis=1)
    q = q_ref[...]
    gates_t = gate_ref[...].T

    def gate_row(g, br):
        rows = [3 * (NSA_GROUP * g + j) + br for j in range(NSA_GROUP)]
        return jnp.concatenate([gates_t[r:r + 1, :] for r in rows], axis=1)

    npr = lax.broadcasted_iota(jnp.int32, (n_sub, 1), 0)
    cvalid = (npr >= 1) & (npr * CMP_STRIDE + (CMP_STRIDE - 1) <= tpos4)
    sb_o = lax.broadcasted_iota(jnp.int32, (nsp, n_sub), 0)
    np_o = lax.broadcasted_iota(jnp.int32, (nsp, n_sub), 1)
    r = SEL_BLOCK // CMP_STRIDE
    ovt = ((np_o >= r * sb_o) & (np_o <= r * sb_o + r) & (np_o >= 1)).astype(BF16)

    groups = range(NSA_KV_HEADS)
    qts = [jnp.concatenate([q[:, (NSA_GROUP * g + j) * HEAD_DIM:(NSA_GROUP * g + j + 1) * HEAD_DIM].T
                            for j in range(NSA_GROUP)], axis=1).astype(BF16) for g in groups]

    cs = [jnp.dot(kc_ref[0, g], qts[g], preferred_element_type=F32) for g in groups]
    cps = [p * inv for p, inv in (_softmax_cols(s, cvalid) for s in cs)]
    octs = [_dot_tn(vc_ref[0, g], cps[g]) for g in groups]
    imps = []
    for g in groups:
        psum = cps[g][:, 0:tq]
        for j in range(1, NSA_GROUP):
            psum = psum + cps[g][:, j * tq:(j + 1) * tq]
        imps.append(sum(jnp.dot(ovt, part, preferred_element_type=F32) for part in _split3(psum)))
    sel = _select_t(jnp.concatenate(imps, axis=1), tpos4, nsp)
    bias_t = jnp.where(sel, 0.0, NEG).astype(BF16)

    kpos = t0 - WINDOW + lax.broadcasted_iota(jnp.int32, (WINDOW + tq, 1), 0)
    wmask = (kpos <= tpos4) & (kpos >= tpos4 - WINDOW) & (kpos >= 0)
    ws = [jnp.dot(jnp.concatenate([w[:, g * HEAD_DIM:(g + 1) * HEAD_DIM] for w in wink_refs], axis=0), qts[g],
                  preferred_element_type=F32) for g in groups]
    wps = [_softmax_cols(s, wmask) for s in ws]
    owts = [jnp.dot(jnp.concatenate([w[g * HEAD_DIM:(g + 1) * HEAD_DIM, :] for w in winvt_refs], axis=1),
                    wps[g][0].astype(BF16), preferred_element_type=F32) * wps[g][1] for g in groups]
    n_full = t0 // KV_TILE
    krow = lax.broadcasted_iota(jnp.int32, (KV_TILE, 1), 0)
    qaugs = []
    for g in range(NSA_KV_HEADS):
        bg = bias_t[:, g * tq:(g + 1) * tq]
        qaugs.append(jnp.concatenate([qts[g], jnp.concatenate([bg] * NSA_GROUP, axis=1)], axis=0))

    def tile_step(c, carries, diagonal):
        k0 = pl.multiple_of(c * KV_TILE, KV_TILE)
        onehot = _block_onehot(k0, KV_TILE, nsp)
        scores = []
        for g in range(NSA_KV_HEADS):
            kt = selk_ref[pl.ds(k0, KV_TILE), pl.ds(g * HEAD_DIM, HEAD_DIM)]
            s = jnp.dot(jnp.concatenate([kt, onehot], axis=1), qaugs[g], preferred_element_type=F32)
            if diagonal:
                s = jnp.where(k0 + krow <= tpos4, s, NEG)
            scores.append(s.astype(BF16))
        stats = []
        for g in range(NSA_KV_HEADS):
            m = carries[g][0]
            m_new = jnp.maximum(m, jnp.max(scores[g], axis=0, keepdims=True).astype(F32))
            stats.append((m_new, jnp.exp2(m - m_new), jnp.exp2(scores[g] - m_new.astype(BF16))))
        out = []
        ones = jnp.ones((16, KV_TILE), BF16)
        for g in range(NSA_KV_HEADS):
            m_new, alpha, p = stats[g]
            vtt = selvt_ref[pl.ds(g * HEAD_DIM, HEAD_DIM), pl.ds(k0, KV_TILE)]
            pv = jnp.dot(jnp.concatenate([vtt, ones], axis=0), p, preferred_element_type=F32)
            out.append((m_new, alpha * carries[g][1] + pv[HEAD_DIM:HEAD_DIM + 1], alpha * carries[g][2] + pv[:HEAD_DIM]))
        return tuple(out)

    init = tuple((jnp.full((1, lanes), NEG, F32), jnp.zeros((1, lanes), F32), jnp.zeros((HEAD_DIM, lanes), F32))
                 for _ in range(NSA_KV_HEADS))
    carries = lax.fori_loop(0, n_full, lambda c, cr: tile_step(c, cr, False), init)
    carries = tile_step(n_full, carries, True)

    for g in groups:
        m, l, acc = carries[g]
        o_s = acc * (1.0 / l)
        o_g = gate_row(g, 0) * octs[g] + gate_row(g, 1) * o_s + gate_row(g, 2) * owts[g]
        for j in range(NSA_GROUP):
            hd = NSA_GROUP * g + j
            oacc_ref[:, hd * HEAD_DIM:(hd + 1) * HEAD_DIM] = o_g[:, j * tq:(j + 1) * tq].T

    _finish(o_ref, oacc_ref, nz_ref[...], og_ref[...])


def _nsa_prompt(qn, gates, proj, kc, vc, selk, selvt, wink, winvt, nsa_g, t_len):
    n_sub = kc.shape[2]
    n_sel = t_len // SEL_BLOCK
    nsp = -(-n_sel // 128) * 128
    nq = t_len // Q_BLOCK
    nwin = WINDOW // Q_BLOCK + 1
    wblk = lambda r: (lambda i: jnp.maximum(i - (nwin - 1) + r, 0))
    win_specs = ([pl.BlockSpec((Q_BLOCK, KV_WIDTH), (lambda f: (lambda i: (f(i), 0)))(wblk(r))) for r in range(nwin)]
                 + [pl.BlockSpec((KV_WIDTH, Q_BLOCK), (lambda f: (lambda i: (0, f(i))))(wblk(r))) for r in range(nwin)])
    return pl.pallas_call(
        functools.partial(_nsa_prompt_kernel, n_sub=n_sub, nsp=nsp),
        grid=(nq,),
        in_specs=[pl.BlockSpec((Q_BLOCK, NSA_WIDTH), lambda i: (i, 0)),
                  pl.BlockSpec((Q_BLOCK, 128), lambda i: (i, 0)),
                  pl.BlockSpec((Q_BLOCK, NSA_WIDTH), lambda i: (i, COL_NZ // 2048)),
                  pl.BlockSpec((1, NSA_KV_HEADS, n_sub, HEAD_DIM), lambda i: (0, 0, 0, 0)),
                  pl.BlockSpec((1, NSA_KV_HEADS, n_sub, HEAD_DIM), lambda i: (0, 0, 0, 0)),
                  pl.BlockSpec((t_len, KV_WIDTH), lambda i: (0, 0), pipeline_mode=pl.Buffered(1)),
                  pl.BlockSpec((KV_WIDTH, t_len), lambda i: (0, 0), pipeline_mode=pl.Buffered(1))]
        + win_specs + [pl.BlockSpec((1, NSA_WIDTH), lambda i: (0, 0))],
        out_specs=pl.BlockSpec((Q_BLOCK, NSA_WIDTH), lambda i: (i, 0)),
        out_shape=jax.ShapeDtypeStruct((t_len, NSA_WIDTH), BF16),
        scratch_shapes=[pltpu.VMEM((Q_BLOCK, NSA_WIDTH), F32)],
        compiler_params=_cparams(("arbitrary",)),
        name="nsa_prompt",
    )(qn, gates, proj, kc, vc, selk, selvt, *([wink] * nwin), *([winvt] * nwin), nsa_g.reshape(1, NSA_WIDTH))


def _nsa_sample_kernel(pt_ref, *refs, n_sub, nsp, past_len, t_new):
    pages = refs[:PAGES_PER_TILE]
    (q_ref, gate_ref, nz_ref, kc_ref, vc_ref, newsel_ref, swin_ref, newwin_ref, og_ref,
     o_ref, qbd_ref, bias_ref, oc_ref, m_ref, l_ref, acc_ref, oacc_ref) = refs[PAGES_PER_TILE:]
    tau = pl.program_id(1)
    n_tau = pl.num_programs(1)
    tq = t_new
    G = NSA_KV_HEADS
    rows = NSA_GROUP * tq
    lanes = G * rows
    tpos = past_len + lax.broadcasted_iota(jnp.int32, (tq, 1), 0)
    tpos4 = jnp.concatenate([tpos] * NSA_GROUP, axis=0)
    lane_i = lax.broadcasted_iota(jnp.int32, (1, lanes), 1)

    def group_q():
        q = q_ref[...]
        return [jnp.concatenate([q[:, (NSA_GROUP * g + j) * HEAD_DIM:(NSA_GROUP * g + j + 1) * HEAD_DIM]
                                 for j in range(NSA_GROUP)], axis=0) for g in range(G)]

    def to_col(row_vec):
        eye = lax.broadcasted_iota(jnp.int32, (lanes, lanes), 0) == lax.broadcasted_iota(jnp.int32, (lanes, lanes), 1)
        return jnp.sum(jnp.where(eye, jnp.broadcast_to(row_vec, (lanes, lanes)), 0.0), axis=1, keepdims=True)

    def online(s, v_all, m, l, acc):
        m_new = jnp.maximum(m, jnp.max(s, axis=0, keepdims=True))
        alpha = jnp.exp2(m - m_new)
        p = jnp.exp2(s - m_new)
        l_new = alpha * l + jnp.sum(p, axis=0, keepdims=True)
        pv = _dot_tn(p, v_all)
        upd = jnp.concatenate([pv[g * rows:(g + 1) * rows, g * HEAD_DIM:(g + 1) * HEAD_DIM] for g in range(G)], axis=0)
        return m_new, l_new, to_col(alpha) * acc + upd

    @pl.when(tau == 0)
    def _():
        qgs = group_q()
        imps = []
        for g in range(G):
            oc, p = _cmp_branch(qgs[g].astype(BF16), kc_ref[0, g], vc_ref[0, g], tpos4, n_sub)
            oc_ref[g] = oc
            psum = p[0:tq]
            for j in range(1, NSA_GROUP):
                psum = psum + p[j * tq:(j + 1) * tq]
            imps.append(_importance(psum, n_sub, nsp))
        sel = _select(jnp.concatenate(imps, axis=0), tpos4, nsp)
        q_t = jnp.concatenate(qgs, axis=0).T
        qbd_ref[...] = jnp.concatenate(
            [jnp.where(lane_i // rows == g, q_t, 0.0) for g in range(G)], axis=0).astype(BF16)
        bias = jnp.where(sel, 0.0, NEG)
        bias = jnp.concatenate([bias[g * tq:(g + 1) * tq] for g in range(G) for _ in range(NSA_GROUP)], axis=0)
        for c in range(nsp // lanes):
            bias_ref[c * lanes:(c + 1) * lanes, :] = bias[:, c * lanes:(c + 1) * lanes].T
        m_ref[...] = jnp.full((1, lanes), NEG, F32)
        l_ref[...] = jnp.zeros((1, lanes), F32)
        acc_ref[...] = jnp.zeros((lanes, HEAD_DIM), F32)

    planes = [jnp.swapaxes(p[0], 0, 1) for p in pages]
    k_all = jnp.concatenate([jnp.concatenate([pl_[g] for g in range(G)], axis=1) for pl_ in planes], axis=0)
    v_all = jnp.concatenate([jnp.concatenate([pl_[G + g] for g in range(G)], axis=1) for pl_ in planes], axis=0)
    n_keys = PAGES_PER_TILE * PAGE
    n_blk = n_keys // SEL_BLOCK
    b_rows = bias_ref[pl.ds(pl.multiple_of(tau * n_blk, n_blk), n_blk), :]
    bias_keys = jnp.broadcast_to(b_rows[:, None, :], (n_blk, SEL_BLOCK, lanes)).reshape(n_keys, lanes)
    s = jnp.dot(k_all.astype(BF16), qbd_ref[...], preferred_element_type=F32) + bias_keys
    m, l, acc = online(s, v_all, m_ref[...], l_ref[...], acc_ref[...])
    m_ref[...] = m
    l_ref[...] = l
    acc_ref[...] = acc

    @pl.when(tau == n_tau - 1)
    def _():
        qgs = group_q()
        gates = gate_ref[...]
        n_buf = swin_ref.shape[1]
        kpos = past_len - n_buf + lax.broadcasted_iota(jnp.int32, (1, n_buf + tq), 1)
        last_blk = past_len // SEL_BLOCK
        pad = 16
        new = newsel_ref[...]
        zpad = jnp.zeros((pad - tq, G * HEAD_DIM), F32)
        kn = jnp.concatenate([jnp.concatenate([new[:, g, :] for g in range(G)], axis=1), zpad], axis=0)
        vn = jnp.concatenate([jnp.concatenate([new[:, G + g, :] for g in range(G)], axis=1), zpad], axis=0)
        krow = lax.broadcasted_iota(jnp.int32, (pad, 1), 0)
        s_new = jnp.dot(kn.astype(BF16), qbd_ref[...], preferred_element_type=F32) + bias_ref[last_blk:last_blk + 1, :]
        s_new = jnp.where((krow <= lane_i % tq) & (krow < tq), s_new, NEG)
        m, l, acc = online(s_new, vn, m_ref[...], l_ref[...], acc_ref[...])
        o_sel = acc * to_col(1.0 / l)
        swin = jnp.swapaxes(swin_ref[0], 0, 1)
        nwin = newwin_ref[...]
        for g in range(G):
            kw = jnp.concatenate([swin[g], nwin[:, g, :]], axis=0)
            vw = jnp.concatenate([swin[G + g], nwin[:, G + g, :]], axis=0)
            o_w = _window_branch(qgs[g], kw, vw, kpos, tpos4)
            o_g = (_gate_col(gates, g, 0, tq) * oc_ref[g] + _gate_col(gates, g, 1, tq) * o_sel[g * rows:(g + 1) * rows]
                   + _gate_col(gates, g, 2, tq) * o_w)
            for j in range(NSA_GROUP):
                hd = NSA_GROUP * g + j
                oacc_ref[:, hd * HEAD_DIM:(hd + 1) * HEAD_DIM] = o_g[j * tq:(j + 1) * tq]
        _finish(o_ref, oacc_ref, nz_ref[...], og_ref[...])


def _nsa_sample(cache, page_table, qn, gates, proj, kc, vc, rows_s, state_win, win_s, nsa_g, row0, t_new):
    nb, n_pages = page_table.shape
    past_len = n_pages * PAGE
    n_sub = kc.shape[2]
    n_sel = past_len // SEL_BLOCK + 1
    nsp = -(-n_sel // 128) * 128
    n_tau = n_pages // PAGES_PER_TILE
    rb = row0 // t_new
    n_buf = state_win.shape[1]
    rows = NSA_GROUP * t_new
    lanes = NSA_KV_HEADS * rows
    assert lanes == 128 and nsp % lanes == 0, "the sample kernel packs all query rows of a batch into one lane tile"
    cg = 2 * NSA_KV_HEADS

    def page_spec(r):
        return pl.BlockSpec((1, PAGE, cg, HEAD_DIM),
                            lambda b, i, pt: (pt[b, i * PAGES_PER_TILE + r], 0, 1, 0))

    grid_spec = pltpu.PrefetchScalarGridSpec(
        num_scalar_prefetch=1,
        grid=(nb, n_tau),
        in_specs=[page_spec(r) for r in range(PAGES_PER_TILE)] + [
            pl.BlockSpec((t_new, NSA_WIDTH), lambda b, i, pt: (b, 0)),
            pl.BlockSpec((t_new, 128), lambda b, i, pt: (b, 0)),
            pl.BlockSpec((t_new, NSA_WIDTH), lambda b, i, pt: (rb + b, COL_NZ // 2048)),
            pl.BlockSpec((1, NSA_KV_HEADS, n_sub, HEAD_DIM), lambda b, i, pt: (b, 0, 0, 0)),
            pl.BlockSpec((1, NSA_KV_HEADS, n_sub, HEAD_DIM), lambda b, i, pt: (b, 0, 0, 0)),
            pl.BlockSpec((t_new, cg, HEAD_DIM), lambda b, i, pt: (b, 1, 0)),
            pl.BlockSpec((1, n_buf, cg, HEAD_DIM), lambda b, i, pt: (b, 0, 0, 0)),
            pl.BlockSpec((t_new, cg, HEAD_DIM), lambda b, i, pt: (b, 0, 0)),
            pl.BlockSpec((1, NSA_WIDTH), lambda b, i, pt: (0, 0))],
        out_specs=pl.BlockSpec((t_new, NSA_WIDTH), lambda b, i, pt: (b, 0)),
        scratch_shapes=[pltpu.VMEM((NSA_KV_HEADS * HEAD_DIM, lanes), BF16),
                        pltpu.VMEM((nsp, lanes), F32),
                        pltpu.VMEM((NSA_KV_HEADS, rows, HEAD_DIM), F32),
                        pltpu.VMEM((1, lanes), F32),
                        pltpu.VMEM((1, lanes), F32),
                        pltpu.VMEM((lanes, HEAD_DIM), F32),
                        pltpu.VMEM((t_new, NSA_WIDTH), F32)],
    )
    return pl.pallas_call(
        functools.partial(_nsa_sample_kernel, n_sub=n_sub, nsp=nsp, past_len=past_len, t_new=t_new),
        grid_spec=grid_spec,
        out_shape=jax.ShapeDtypeStruct((nb * t_new, NSA_WIDTH), F32),
        compiler_params=_cparams(("arbitrary", "arbitrary")),
        name="nsa_sample",
    )(page_table, *([cache] * PAGES_PER_TILE), qn, gates, proj, kc, vc, rows_s, state_win, win_s,
      nsa_g.reshape(1, NSA_WIDTH))


def _out_kernel(oh_ref, on_ref, wh_ref, wn_ref, x_ref, gate_ref, y_ref):
    acc = _dot(oh_ref[...], wh_ref[...]) + _dot(on_ref[...], wn_ref[...])
    y_ref[...] = x_ref[...] + gate_ref[...] * acc


def _out_proj(o_h, o_n, w_out, x, gate, tm, tn=512):
    m, d = x.shape
    half = o_h.shape[1]
    grow = gate.shape[0]
    gmap = (lambda i, j: (0, j)) if grow == 1 else (lambda i, j: (i, j))
    return pl.pallas_call(
        _out_kernel,
        grid=(m // tm, d // tn),
        in_specs=[pl.BlockSpec((tm, half), lambda i, j: (i, 0)),
                  pl.BlockSpec((tm, half), lambda i, j: (i, 0)),
                  pl.BlockSpec((half, tn), lambda i, j: (0, j)),
                  pl.BlockSpec((half, tn), lambda i, j: (1, j)),
                  pl.BlockSpec((tm, tn), lambda i, j: (i, j)),
                  pl.BlockSpec((1 if grow == 1 else tm, tn), gmap)],
        out_specs=pl.BlockSpec((tm, tn), lambda i, j: (i, j)),
        out_shape=jax.ShapeDtypeStruct((m, d), F32),
        compiler_params=_cparams(("arbitrary", "arbitrary")),
        name="out_proj",
    )(o_h, o_n, w_out, w_out, x, gate)


def _layer(xp, xs, c_all, cache, s_win, s_hgrn, page_table, lb_param, layer, norm_g, w_ada, b_ada, w_in,
           hgrn_out_g, q_g, k_g, pe, w1, w2, nsa_out_g, w_out):
    t_len = xp.shape[0]
    nb, t_new, _ = xs.shape
    n_s = nb * t_new
    d = D_MODEL

    mod = _ada(c_all, w_ada, b_ada)
    shift, scale, gate = mod[:, :d], mod[:, d:2 * d], mod[:, 2 * d:]
    rep = lambda a: jnp.pad(jnp.repeat(a[1:1 + nb], t_new, axis=0), ((0, SAMPLE_ROWS - n_s), (0, 0)))
    xs_pad = jnp.pad(xs.reshape(n_s, d), ((0, SAMPLE_ROWS - n_s), (0, 0)))
    h_all = _norm(xp, xs_pad, norm_g.reshape(1, d), scale[0:1], shift[0:1], rep(scale), rep(shift))

    proj = _matmul(h_all, jnp.swapaxes(w_in, 0, 1))

    oh_p, st_p = _hgrn(proj, lb_param, hgrn_out_g, None, row0=0, n_batch=1, t_len=t_len,
                       chunk=min(64, t_len), out_dtype=BF16, layer=layer)
    oh_s, st_s = _hgrn(proj, lb_param, hgrn_out_g, s_hgrn, row0=t_len, n_batch=nb, t_len=t_new,
                       chunk=t_new, out_dtype=F32, layer=layer)

    qn_p, rows_p, win_p3, selk_p, selvt_p, wink_p, winvt_p, gates_p = _prep(proj, q_g, k_g, 0, t_len)
    qn_s, rows_s, win_s3, _, _, _, _, gates_s = _prep(proj, q_g, k_g, t_len, SAMPLE_ROWS)

    w1r = w1.reshape(2, 2, CMP_STRIDE * HEAD_DIM, CMP_HIDDEN)
    wc = jnp.concatenate([w1r[:, 0], w1r[:, 1]], axis=-1).astype(BF16)
    pe2 = pe.reshape(2, 2, 1, CMP_STRIDE * HEAD_DIM)
    w2b = w2.astype(BF16)

    n_pp = t_len // PAGE
    kc_p, vc_p = _compress(rows_p.reshape(n_pp, PAGE, 4 * NSA_KV_HEADS, HEAD_DIM),
                           jnp.arange(n_pp, dtype=jnp.int32).reshape(1, n_pp), wc, pe2, w2b, k_g[0])
    cache4 = cache.reshape(cache.shape[0], PAGE, 4 * NSA_KV_HEADS, HEAD_DIM)
    kc_s, vc_s = _compress(cache4, page_table, wc, pe2, w2b, k_g[0])

    on_p = _nsa_prompt(qn_p, gates_p, proj, kc_p, vc_p, selk_p, selvt_p, wink_p, winvt_p, nsa_out_g, t_len)
    n_buf = s_win.shape[1]
    s_win4 = s_win.reshape(nb, n_buf, 2 * NSA_KV_HEADS, HEAD_DIM)
    on_s = _nsa_sample(cache4, page_table, qn_s, gates_s, proj, kc_s, vc_s, rows_s, s_win4, win_s3,
                       nsa_out_g, t_len, t_new)

    y_p = _out_proj(oh_p, on_p, w_out, xp, gate[0:1], tm=min(1024, t_len))
    y_s = _out_proj(oh_s, on_s, w_out, xs.reshape(n_s, d), jnp.repeat(gate[1:1 + nb], t_new, axis=0), tm=n_s)

    n_w = min(WINDOW, t_len)
    kv_p = rows_p.reshape(1, t_len, 4, NSA_KV_HEADS, HEAD_DIM)
    kv_s = rows_s[:n_s].reshape(nb, t_new, 4, NSA_KV_HEADS, HEAD_DIM)
    win_p = win_p3[t_len - n_w:].reshape(1, n_w, 2, NSA_KV_HEADS, HEAD_DIM)
    win_new = win_s3[:n_s].reshape(nb, t_new, 2, NSA_KV_HEADS, HEAD_DIM)
    win_s = jnp.concatenate([s_win, win_new], axis=1)[:, -n_buf:]
    return y_p, y_s.reshape(nb, t_new, d), kv_p, kv_s, win_p, win_s, st_p, st_s


def kernel(x_prompt, x_sample, cache_kv, state_win, state_hgrn, page_table, c_prompt, c_sample, norm_g, w_ada,
           b_ada, w_in, hgrn_lb, hgrn_out_g, q_norm_g, k_norm_g, cmp_pe, cmp_w1, cmp_w2, nsa_out_g, w_out):
    depth = w_in.shape[0]
    assert depth == 1 and x_prompt.shape[0] == 1
    nb = x_sample.shape[0]
    c_all = jnp.concatenate([c_prompt, c_sample], axis=0)
    c_all = jnp.pad(c_all, ((0, (-c_all.shape[0]) % 8), (0, 0)))
    l = 0
    outs = _layer(x_prompt[0], x_sample, c_all, cache_kv[l], state_win[l], state_hgrn[l], page_table,
                  hgrn_lb, l, norm_g[l], w_ada[l], b_ada[l], w_in[l], hgrn_out_g[l], q_norm_g[l], k_norm_g[l],
                  cmp_pe[l], cmp_w1[l], cmp_w2[l], nsa_out_g[l], w_out[l])
    y_p, y_s, kv_p, kv_s, win_p, win_s, st_p, st_s = outs
    return (y_p[None], y_s, kv_p[None], kv_s[None], win_p[None], win_s[None], st_p[None], st_s[None])
```

```python
import functools

import jax
import jax.numpy as jnp
from jax import lax
from jax.experimental import pallas as pl
from jax.experimental.pallas import tpu as pltpu

F32 = jnp.float32
BF16 = jnp.bfloat16

D_MODEL = 4096
HEAD_DIM = 128
HGRN_WIDTH = 2048
NSA_WIDTH = 2048
HGRN_HEADS = 16
NSA_HEADS = 16
NSA_KV_HEADS = 4
NSA_GROUP = 4
KV_WIDTH = 512
CMP_LEN = 32
CMP_STRIDE = 16
CMP_HIDDEN = 256
SEL_BLOCK = 64
SEL_TOP = 16
WINDOW = 512
Q_BLOCK = 128
NORM_EPS = 1e-6
PAGE = 128

COL_HQ, COL_HF, COL_HI, COL_HZ, COL_NQ, COL_ROWS, COL_NZ = (i * 2048 for i in range(7))
COL_WIN = 14336
COL_NG = 15360
N_PROJ = 15872
SAMPLE_ROWS = 256
ROW_TILE = 256
PAGES_PER_TILE = 8
CMP_PAGES = 32
KV_TILE = 512
NEG = -1e30
LOG2E = 1.4426950408889634
VMEM_LIMIT = 56 * 1024 * 1024


def _cparams(sem):
    return pltpu.CompilerParams(dimension_semantics=sem, vmem_limit_bytes=VMEM_LIMIT)


def _dot(a, b):
    return jnp.dot(a.astype(BF16), b.astype(BF16), preferred_element_type=F32)


def _dot_nt(a, b):
    return lax.dot_general(a.astype(BF16), b.astype(BF16), (((1,), (1,)), ((), ())),
                           preferred_element_type=F32)


def _dot_tn(a, b):
    return lax.dot_general(a.astype(BF16), b.astype(BF16), (((0,), (0,)), ((), ())),
                           preferred_element_type=F32)


def _silu(x):
    return x * jax.nn.sigmoid(x)


def _ada_kernel(c_ref, w_ref, b_ref, o_ref):
    c = c_ref[...]
    o_ref[...] = _dot(_silu(c), w_ref[...]) + b_ref[...]


def _ada(c_all, w_ada, b_ada):
    m, d = c_all.shape
    n = w_ada.shape[1]
    tn = 512
    return pl.pallas_call(
        _ada_kernel,
        grid=(n // tn,),
        in_specs=[pl.BlockSpec((m, d), lambda j: (0, 0)),
                  pl.BlockSpec((d, tn), lambda j: (0, j)),
                  pl.BlockSpec((1, tn), lambda j: (0, j))],
        out_specs=pl.BlockSpec((m, tn), lambda j: (0, j)),
        out_shape=jax.ShapeDtypeStruct((m, n), F32),
        compiler_params=_cparams(("arbitrary",)),
        name="ada",
    )(c_all, w_ada, b_ada.reshape(1, n))


def _norm_kernel(xp_ref, xs_ref, g_ref, scp_ref, shp_ref, scs_ref, shs_ref, o_ref, *, n_prompt):
    i = pl.program_id(0)

    def body(x, scale, shift):
        r = x * lax.rsqrt(jnp.mean(x * x, axis=-1, keepdims=True) + NORM_EPS) * g_ref[...]
        o_ref[...] = (r * (1.0 + scale) + shift).astype(o_ref.dtype)

    @pl.when(i < n_prompt)
    def _():
        body(xp_ref[...], scp_ref[...], shp_ref[...])

    @pl.when(i >= n_prompt)
    def _():
        body(xs_ref[...], scs_ref[...], shs_ref[...])


def _norm(xp, xs, g, sc_p, sh_p, sc_s, sh_s):
    t, d = xp.shape
    n_prompt = t // ROW_TILE
    row = lambda i: (0, 0)
    return pl.pallas_call(
        functools.partial(_norm_kernel, n_prompt=n_prompt),
        grid=(n_prompt + 1,),
        in_specs=[pl.BlockSpec((ROW_TILE, d), lambda i: (jnp.minimum(i, n_prompt - 1), 0)),
                  pl.BlockSpec((SAMPLE_ROWS, d), row),
                  pl.BlockSpec((1, d), row), pl.BlockSpec((1, d), row), pl.BlockSpec((1, d), row),
                  pl.BlockSpec((SAMPLE_ROWS, d), row), pl.BlockSpec((SAMPLE_ROWS, d), row)],
        out_specs=pl.BlockSpec((ROW_TILE, d), lambda i: (i, 0)),
        out_shape=jax.ShapeDtypeStruct((t + SAMPLE_ROWS, d), BF16),
        compiler_params=_cparams(("arbitrary",)),
        name="norm",
    )(xp, xs, g, sc_p, sh_p, sc_s, sh_s)


PROJ_TN = 512
_N_MAIN_A, _N_NZ, _N_WIN = 24, 4, 2
_ROW_WIN = _N_MAIN_A * PROJ_TN
_ROW_NG = _ROW_WIN + _N_WIN * PROJ_TN
_ROW_NZ = _ROW_NG + 3 * NSA_HEADS


_ROW_UNIT = 16


def _proj_weight_row(j):
    u = PROJ_TN // _ROW_UNIT
    q = jnp.where(j < _N_MAIN_A, j * u,
                  jnp.where(j < _N_MAIN_A + _N_NZ, _ROW_NZ // _ROW_UNIT + (j - _N_MAIN_A) * u,
                            jnp.where(j < _N_MAIN_A + _N_NZ + _N_WIN,
                                      _ROW_WIN // _ROW_UNIT + (j - _N_MAIN_A - _N_NZ) * u, _ROW_NG // _ROW_UNIT)))
    return q * _ROW_UNIT


def _mm_kernel(a_ref, wt_ref, o_ref, wb_ref):
    @pl.when(pl.program_id(1) == 0)
    def _():
        wb_ref[...] = wt_ref[...].astype(BF16)

    o_ref[...] = lax.dot_general(a_ref[...], wb_ref[...], (((1,), (1,)), ((), ())), preferred_element_type=F32)


def _row_tile(m, cap=1100):
    best = 16
    for tm in range(16, cap + 1, 16):
        if m % tm == 0:
            best = tm
    return best


def _matmul(a, w_t):
    m, k = a.shape
    tn = PROJ_TN
    tm = _row_tile(m)
    return pl.pallas_call(
        _mm_kernel,
        grid=(N_PROJ // tn, m // tm),
        in_specs=[pl.BlockSpec((tm, k), lambda j, i: (i, 0)),
                  pl.BlockSpec((pl.Element(tn), pl.Element(k)), lambda j, i: (_proj_weight_row(j), 0))],
        out_specs=pl.BlockSpec((tm, tn), lambda j, i: (i, j)),
        out_shape=jax.ShapeDtypeStruct((m, N_PROJ), F32),
        scratch_shapes=[pltpu.VMEM((tn, k), BF16)],
        compiler_params=_cparams(("arbitrary", "arbitrary")),
        name="proj",
    )(a, w_t)


def _hgrn_kernel(*refs, chunk, levels, has_s0, layer):
    if has_s0:
        lbp_ref, og_ref, hq_ref, hf_ref, hi_ref, hz_ref, s0_ref, o_ref, sout_ref, st_ref = refs
    else:
        lbp_ref, og_ref, hq_ref, hf_ref, hi_ref, hz_ref, o_ref, sout_ref, st_ref = refs
        s0_ref = None
    C = chunk
    c = pl.program_id(1)
    n_chunks = pl.num_programs(1)

    @pl.when(c == 0)
    def _():
        for h in range(HGRN_HEADS):
            if has_s0:
                st_ref[h] = s0_ref[0, h].T
            else:
                st_ref[h] = jnp.zeros((HEAD_DIM, HEAD_DIM), F32)

    lbp = lbp_ref[...]
    e = jnp.exp(lbp - jnp.max(lbp, axis=0, keepdims=True))
    lb = jnp.sum(e[:layer + 1], axis=0, keepdims=True) / jnp.sum(e, axis=0, keepdims=True)

    f = lb + (1.0 - lb) * jax.nn.sigmoid(hf_ref[...])
    logf = jnp.log(f)
    k = 1.0 - f
    q = hq_ref[...]
    v = hi_ref[...]
    hz = hz_ref[...]
    n = q.shape[1]

    row = lax.broadcasted_iota(jnp.int32, (C, 1), 0)
    cum = logf
    s = 1
    while s < C:
        cum = cum + jnp.where(row >= s, pltpu.roll(cum, s, axis=0), 0.0)
        s *= 2
    cum = cum * LOG2E
    last = cum[C - 1:C, :]
    qe = q * jnp.exp2(cum)
    kd = k * jnp.exp2(last - cum)
    elast = jnp.exp2(last)

    lvl = []
    for h in levels:
        np_ = C // (2 * h)
        c4 = cum.reshape(np_, 2, h, n)
        cmid = c4[:, 0, h - 1:h, :]
        zero = jnp.zeros((np_, 1, h, n), F32)
        qr = q.reshape(np_, 2, h, n)[:, 1] * jnp.exp2(c4[:, 1] - cmid)
        kl_ = k.reshape(np_, 2, h, n)[:, 0] * jnp.exp2(cmid - c4[:, 0])
        ql = jnp.concatenate([zero, qr[:, None]], axis=1).reshape(C, n)
        kl = jnp.concatenate([kl_[:, None], zero], axis=1).reshape(C, n)
        ti = lax.broadcasted_iota(jnp.int32, (C, C), 0)
        si = lax.broadcasted_iota(jnp.int32, (C, C), 1)
        mask = ((ti // (2 * h)) == (si // (2 * h))) & (((ti // h) % 2) == 1) & (((si // h) % 2) == 0)
        lvl.append((ql, kl, mask))

    nb = C // 8
    rin = lax.broadcasted_iota(jnp.int32, (nb, 8, 1), 1)
    og = og_ref[...]
    for h in range(HGRN_HEADS):
        sl = slice(h * HEAD_DIM, (h + 1) * HEAD_DIM)
        st = st_ref[h]
        q_h, k_h, v_h, cum_h = q[:, sl], k[:, sl], v[:, sl], cum[:, sl]
        o_h = _dot_nt(qe[:, sl], st)
        if levels:
            a = jnp.zeros((C, C), F32)
            for ql, kl, mask in lvl:
                a = a + jnp.where(mask, _dot_nt(ql[:, sl], kl[:, sl]), 0.0)
            o_h = o_h + _dot(a, v_h)
        q3 = q_h.reshape(nb, 8, HEAD_DIM)
        k3 = k_h.reshape(nb, 8, HEAD_DIM)
        v3 = v_h.reshape(nb, 8, HEAD_DIM)
        c3 = cum_h.reshape(nb, 8, HEAD_DIM)
        od = jnp.zeros((nb, 8, HEAD_DIM), F32)
        for s in range(8):
            dec = jnp.exp2(jnp.where(rin >= s, c3 - c3[:, s:s + 1, :], -jnp.inf))
            w = q3 * k3[:, s:s + 1, :] * dec
            od = od + jnp.sum(w, axis=-1, keepdims=True) * v3[:, s:s + 1, :]
        o_h = o_h + od.reshape(C, HEAD_DIM)
        st_ref[h] = st * elast[:, sl] + _dot_tn(v_h, kd[:, sl])
        o_h = o_h * lax.rsqrt(jnp.mean(o_h * o_h, axis=-1, keepdims=True) + NORM_EPS) * og
        o_ref[:, sl] = (o_h * _silu(hz[:, sl])).astype(o_ref.dtype)

    @pl.when(c == n_chunks - 1)
    def _():
        for h in range(HGRN_HEADS):
            sout_ref[0, h] = st_ref[h].T


def _hgrn(proj, lb_param, out_g, s0, *, row0, n_batch, t_len, chunk, out_dtype, layer):
    n_chunks = t_len // chunk
    levels = []
    h = chunk // 2
    while h >= 8:
        levels.append(h)
        h //= 2
    blk0 = row0 // chunk
    rmap = lambda col: (lambda b, c: (blk0 + b * n_chunks + c, col))
    in_specs = [pl.BlockSpec(lb_param.shape, lambda b, c: (0, 0)),
                pl.BlockSpec((1, HEAD_DIM), lambda b, c: (0, 0)),
                pl.BlockSpec((chunk, HGRN_WIDTH), rmap(0)),
                pl.BlockSpec((chunk, HGRN_WIDTH), rmap(1)),
                pl.BlockSpec((chunk, HGRN_WIDTH), rmap(2)),
                pl.BlockSpec((chunk, HGRN_WIDTH), rmap(3))]
    args = [lb_param, out_g.reshape(1, HEAD_DIM), proj, proj, proj, proj]
    if s0 is not None:
        in_specs.append(pl.BlockSpec((1, HGRN_HEADS, HEAD_DIM, HEAD_DIM), lambda b, c: (b, 0, 0, 0)))
        args.append(s0)
    return pl.pallas_call(
        functools.partial(_hgrn_kernel, chunk=chunk, levels=tuple(levels), has_s0=s0 is not None, layer=layer),
        grid=(n_batch, n_chunks),
        in_specs=in_specs,
        out_specs=[pl.BlockSpec((chunk, HGRN_WIDTH), lambda b, c: (b * n_chunks + c, 0)),
                   pl.BlockSpec((1, HGRN_HEADS, HEAD_DIM, HEAD_DIM), lambda b, c: (b, 0, 0, 0))],
        out_shape=[jax.ShapeDtypeStruct((n_batch * t_len, HGRN_WIDTH), out_dtype),
                   jax.ShapeDtypeStruct((n_batch, HGRN_HEADS, HEAD_DIM, HEAD_DIM), F32)],
        scratch_shapes=[pltpu.VMEM((HGRN_HEADS, HEAD_DIM, HEAD_DIM), F32)],
        compiler_params=_cparams(("arbitrary", "arbitrary")),
        name="hgrn",
    )(*args)


def _head_rms(x, g, n_heads):
    outs = []
    for h in range(n_heads):
        xh = x[:, h * HEAD_DIM:(h + 1) * HEAD_DIM]
        outs.append(xh * lax.rsqrt(jnp.mean(xh * xh, axis=-1, keepdims=True) + NORM_EPS) * g)
    return jnp.concatenate(outs, axis=1)


def _prep_kernel(nq_ref, rows_ref, win_ref, ng_ref, qg_ref, kg_ref,
                 q_ref, rows_o, win_o, selk_o, selvt_o, wink_o, winvt_o, gate_o):
    kg = kg_ref[...]
    q_ref[...] = _head_rms(nq_ref[...], qg_ref[...], NSA_HEADS) * (HEAD_DIM ** -0.5 * LOG2E)
    rows = rows_ref[...]
    ks = _head_rms(rows[:, 2 * KV_WIDTH:3 * KV_WIDTH], kg[1:2], NSA_KV_HEADS)
    for cg in range(4 * NSA_KV_HEADS):
        sl = slice(cg * HEAD_DIM, (cg + 1) * HEAD_DIM)
        if 2 * NSA_KV_HEADS <= cg < 3 * NSA_KV_HEADS:
            rows_o[:, cg, :] = ks[:, (cg - 2 * NSA_KV_HEADS) * HEAD_DIM:(cg - 2 * NSA_KV_HEADS + 1) * HEAD_DIM]
        else:
            rows_o[:, cg, :] = rows[:, sl]
    selk_o[...] = ks.astype(BF16)
    selvt_o[...] = rows[:, 3 * KV_WIDTH:].T.astype(BF16)
    win = win_ref[...]
    kw = _head_rms(win[:, :KV_WIDTH], kg[2:3], NSA_KV_HEADS)
    for cg in range(2 * NSA_KV_HEADS):
        sl = slice(cg * HEAD_DIM, (cg + 1) * HEAD_DIM)
        win_o[:, cg, :] = kw[:, sl] if cg < NSA_KV_HEADS else win[:, sl]
    wink_o[...] = kw.astype(BF16)
    winvt_o[...] = win[:, KV_WIDTH:].T.astype(BF16)
    gate_o[...] = jax.nn.sigmoid(ng_ref[...])


def _prep(proj, q_g, k_g, row0, n_rows):
    t = min(ROW_TILE, n_rows)
    r0 = row0 // t
    return pl.pallas_call(
        _prep_kernel,
        grid=(n_rows // t,),
        in_specs=[pl.BlockSpec((t, NSA_WIDTH), lambda i: (r0 + i, COL_NQ // 2048)),
                  pl.BlockSpec((t, 2048), lambda i: (r0 + i, COL_ROWS // 2048)),
                  pl.BlockSpec((t, 1024), lambda i: (r0 + i, COL_WIN // 1024)),
                  pl.BlockSpec((t, 128), lambda i: (r0 + i, COL_NG // 128)),
                  pl.BlockSpec((1, HEAD_DIM), lambda i: (0, 0)),
                  pl.BlockSpec((3, HEAD_DIM), lambda i: (0, 0))],
        out_specs=[pl.BlockSpec((t, 2048), lambda i: (i, 0)),
                   pl.BlockSpec((t, 4 * NSA_KV_HEADS, HEAD_DIM), lambda i: (i, 0, 0)),
                   pl.BlockSpec((t, 2 * NSA_KV_HEADS, HEAD_DIM), lambda i: (i, 0, 0)),
                   pl.BlockSpec((t, KV_WIDTH), lambda i: (i, 0)),
                   pl.BlockSpec((KV_WIDTH, t), lambda i: (0, i)),
                   pl.BlockSpec((t, KV_WIDTH), lambda i: (i, 0)),
                   pl.BlockSpec((KV_WIDTH, t), lambda i: (0, i)),
                   pl.BlockSpec((t, 128), lambda i: (i, 0))],
        out_shape=[jax.ShapeDtypeStruct((n_rows, 2048), F32),
                   jax.ShapeDtypeStruct((n_rows, 4 * NSA_KV_HEADS, HEAD_DIM), F32),
                   jax.ShapeDtypeStruct((n_rows, 2 * NSA_KV_HEADS, HEAD_DIM), F32),
                   jax.ShapeDtypeStruct((n_rows, KV_WIDTH), BF16),
                   jax.ShapeDtypeStruct((KV_WIDTH, n_rows), BF16),
                   jax.ShapeDtypeStruct((n_rows, KV_WIDTH), BF16),
                   jax.ShapeDtypeStruct((KV_WIDTH, n_rows), BF16),
                   jax.ShapeDtypeStruct((n_rows, 128), F32)],
        compiler_params=_cparams(("arbitrary",)),
        name="prep",
    )(proj, proj, proj, proj, q_g.reshape(1, HEAD_DIM), k_g)


def _compress_kernel(pt_ref, *refs, n_pg):
    pages = refs[:n_pg]
    wc_ref, pe_ref, w2_ref, kcg_ref, kc_ref, vc_ref, cvec_ref, carry_ref, out_scr = refs[n_pg:]
    b = pl.program_id(0)
    i = pl.program_id(1)
    nsub = PAGE // CMP_STRIDE
    ntile = nsub * n_pg
    G = NSA_KV_HEADS
    rows = ntile * G

    @pl.when((b == 0) & (i == 0))
    def _():
        for ch in range(2):
            lo = jnp.broadcast_to(pe_ref[ch, 0], (8, CMP_STRIDE * HEAD_DIM))
            hi = jnp.broadcast_to(pe_ref[ch, 1], (8, CMP_STRIDE * HEAD_DIM))
            cvec_ref[ch] = _dot(lo, wc_ref[ch][:, :CMP_HIDDEN]) + _dot(hi, wc_ref[ch][:, CMP_HIDDEN:])

    @pl.when(i == 0)
    def _():
        carry_ref[...] = jnp.zeros(carry_ref.shape, F32)

    def gathered(p):
        lo, hi = [], []
        for r in range(n_pg):
            a = pages[r][0, pl.ds(p, nsub // 2, stride=2 * CMP_STRIDE), :, :]
            bb = pages[r][0, pl.ds(p + CMP_STRIDE, nsub // 2, stride=2 * CMP_STRIDE), :, :]
            lo.append(jnp.concatenate([a[:, :G], bb[:, :G]], axis=1).reshape(nsub * G, HEAD_DIM))
            hi.append(jnp.concatenate([a[:, G:], bb[:, G:]], axis=1).reshape(nsub * G, HEAD_DIM))
        return jnp.concatenate(lo, axis=0).astype(BF16), jnp.concatenate(hi, axis=0).astype(BF16)

    xs = [gathered(p) for p in range(CMP_STRIDE)]
    u = [jnp.dot(jnp.concatenate([x[ch] for x in xs], axis=1), wc_ref[ch], preferred_element_type=F32)
         for ch in range(2)]

    row8 = lax.broadcasted_iota(jnp.int32, (8, 1), 0)
    for ch in range(2):
        u0, u1 = u[ch][:, :CMP_HIDDEN], u[ch][:, CMP_HIDDEN:]
        rolled = pltpu.roll(u0, G, axis=0)
        head = jnp.where(row8 < G, carry_ref[ch], rolled[0:8])
        carry_ref[ch] = rolled[0:8]
        prev = jnp.concatenate([head, rolled[8:]], axis=0)
        pre = prev + u1 + cvec_ref[ch][0:1, :]
        out = _dot(_silu(pre), w2_ref[ch])
        if ch == 0:
            out = out * lax.rsqrt(jnp.mean(out * out, axis=-1, keepdims=True) + NORM_EPS) * kcg_ref[...]
        out_scr[...] = out
        dst = kc_ref if ch == 0 else vc_ref
        for g in range(G):
            dst[0, g] = out_scr[pl.ds(g, ntile, stride=G), :].astype(dst.dtype)


def _compress(src, page_table, wc, pe2, w2, kc_g):
    nb, n_pages = page_table.shape
    n_pg = CMP_PAGES if n_pages % CMP_PAGES == 0 else PAGES_PER_TILE
    n_tiles = n_pages // n_pg
    nsub = PAGE // CMP_STRIDE
    ntile = nsub * n_pg

    def page_spec(r):
        return pl.BlockSpec((1, PAGE, 2 * NSA_KV_HEADS, HEAD_DIM),
                            lambda b, i, pt: (pt[b, i * n_pg + r], 0, 0, 0))

    const = lambda *shape: pl.BlockSpec(shape, lambda b, i, pt: (0,) * len(shape))
    out_spec = pl.BlockSpec((1, NSA_KV_HEADS, ntile, HEAD_DIM), lambda b, i, pt: (b, 0, i, 0))
    grid_spec = pltpu.PrefetchScalarGridSpec(
        num_scalar_prefetch=1,
        grid=(nb, n_tiles),
        in_specs=[page_spec(r) for r in range(n_pg)] + [
            const(2, CMP_STRIDE * HEAD_DIM, 2 * CMP_HIDDEN),
            const(2, 2, 1, CMP_STRIDE * HEAD_DIM),
            const(2, CMP_HIDDEN, HEAD_DIM),
            const(1, HEAD_DIM)],
        out_specs=[out_spec, out_spec],
        scratch_shapes=[pltpu.VMEM((2, 8, CMP_HIDDEN), F32),
                        pltpu.VMEM((2, 8, CMP_HIDDEN), F32),
                        pltpu.VMEM((NSA_KV_HEADS * ntile, HEAD_DIM), F32)],
    )
    shape = jax.ShapeDtypeStruct((nb, NSA_KV_HEADS, n_tiles * ntile, HEAD_DIM), BF16)
    return pl.pallas_call(
        functools.partial(_compress_kernel, n_pg=n_pg),
        grid_spec=grid_spec,
        out_shape=[shape, shape],
        compiler_params=_cparams(("arbitrary", "arbitrary")),
        name="compress",
    )(page_table, *([src] * n_pg), wc, pe2, w2, kc_g.reshape(1, HEAD_DIM))


def _split3(x):
    hi = x.astype(BF16)
    r = x - hi.astype(F32)
    mid = r.astype(BF16)
    lo = (r - mid.astype(F32)).astype(BF16)
    return hi, mid, lo


def _cmp_branch(qg, kc, vc, tpos4, n_sub):
    s = _dot_nt(qg, kc)
    npr = lax.broadcasted_iota(jnp.int32, (1, n_sub), 1)
    valid = (npr >= 1) & (npr * CMP_STRIDE + (CMP_STRIDE - 1) <= tpos4)
    s = jnp.where(valid, s, NEG)
    m = jnp.max(s, axis=-1, keepdims=True)
    p = jnp.where(valid, jnp.exp2(s - m), 0.0)
    l = jnp.sum(p, axis=-1, keepdims=True)
    p = p * jnp.where(l > 0.0, 1.0 / l, 0.0)
    return _dot(p, vc), p


def _softmax_cols(s, mask):
    s = jnp.where(mask, s, NEG)
    m = jnp.maximum(jnp.max(s, axis=0, keepdims=True), 0.1 * NEG)
    p = jnp.exp2(s - m)
    l = jnp.sum(p, axis=0, keepdims=True)
    return p, jnp.where(l > 0.0, 1.0 / l, 0.0)


def _select_t(imp, tpos, nsp):
    sb = lax.broadcasted_iota(jnp.int32, (nsp, 1), 0)
    cur = tpos // SEL_BLOCK
    causal = sb * SEL_BLOCK <= tpos
    forced = (sb == 0) | (sb == cur) | (sb == cur - 1)
    score = jnp.where(causal, jnp.where(forced, -NEG, imp), NEG)
    sel = jnp.zeros(score.shape, jnp.bool_)
    sbf = sb.astype(F32)
    for _ in range(SEL_TOP):
        m = jnp.max(score, axis=0, keepdims=True)
        idx = jnp.min(jnp.where(score == m, sbf, float(nsp)), axis=0, keepdims=True)
        hit = sbf == idx
        sel = sel | hit
        score = jnp.where(hit, NEG, score)
    return sel & causal


def _importance(psum, n_sub, nsp):
    npr = lax.broadcasted_iota(jnp.int32, (n_sub, nsp), 0)
    sb = lax.broadcasted_iota(jnp.int32, (n_sub, nsp), 1)
    r = SEL_BLOCK // CMP_STRIDE
    ov = ((npr >= r * sb) & (npr <= r * sb + r) & (npr >= 1)).astype(BF16)
    hi, mid, lo = _split3(psum)
    dot = lambda a: jnp.dot(a, ov, preferred_element_type=F32)
    return dot(hi) + dot(mid) + dot(lo)


def _select(imp, tpos, nsp):
    sb = lax.broadcasted_iota(jnp.int32, (1, nsp), 1)
    cur = tpos // SEL_BLOCK
    causal = sb * SEL_BLOCK <= tpos
    forced = (sb == 0) | (sb == cur) | (sb == cur - 1)
    score = jnp.where(causal, jnp.where(forced, -NEG, imp), NEG)
    sel = jnp.zeros(score.shape, jnp.bool_)
    sbf = sb.astype(F32)
    for _ in range(SEL_TOP):
        m = jnp.max(score, axis=-1, keepdims=True)
        idx = jnp.min(jnp.where(score == m, sbf, float(nsp)), axis=-1, keepdims=True)
        hit = sbf == idx
        sel = sel | hit
        score = jnp.where(hit, NEG, score)
    return sel & causal


def _online_step(carry, s, vt):
    m, l, acc = carry
    m_new = jnp.maximum(m, jnp.max(s, axis=-1, keepdims=True))
    alpha = jnp.exp2(m - m_new)
    p = jnp.exp2(s - m_new)
    l = alpha * l + jnp.sum(p, axis=-1, keepdims=True)
    acc = alpha * acc + _dot(p, vt)
    return m_new, l, acc


def _block_onehot(key0, n_keys, nsp):
    shift = SEL_BLOCK.bit_length() - 1
    kb = jnp.right_shift(lax.broadcasted_iota(jnp.int32, (n_keys, 1), 0), shift)
    sb = lax.broadcasted_iota(jnp.int32, (1, nsp), 1) - jnp.right_shift(key0, shift)
    return (kb == sb).astype(BF16)


def _window_branch(qg, kw, vw, kpos, tpos4):
    s = _dot_nt(qg, kw)
    mask = (kpos <= tpos4) & (kpos >= tpos4 - WINDOW) & (kpos >= 0)
    s = jnp.where(mask, s, NEG)
    m = jnp.max(s, axis=-1, keepdims=True)
    p = jnp.where(mask, jnp.exp2(s - m), 0.0)
    l = jnp.sum(p, axis=-1, keepdims=True)
    return _dot(p, vw) * jnp.where(l > 0.0, 1.0 / l, 0.0)


def _gate_col(gates, g, br, tq):
    return jnp.concatenate(
        [gates[:, 3 * (NSA_GROUP * g + j) + br:3 * (NSA_GROUP * g + j) + br + 1] for j in range(NSA_GROUP)], axis=0)


def _finish(o_ref, oacc_ref, nz, og):
    o = oacc_ref[...]
    o = o * lax.rsqrt(jnp.mean(o * o, axis=-1, keepdims=True) + NORM_EPS) * og
    o_ref[...] = (o * _silu(nz)).astype(o_ref.dtype)


def _nsa_prompt_kernel(q_ref, gate_ref, nz_ref, kc_ref, vc_ref, selk_ref, selvt_ref, *rest, n_sub, nsp):
    nwin = WINDOW // Q_BLOCK + 1
    wink_refs = rest[:nwin]
    winvt_refs = rest[nwin:2 * nwin]
    og_ref, o_ref, oacc_ref = rest[2 * nwin:]
    i = pl.program_id(0)
    tq = Q_BLOCK
    lanes = NSA_GROUP * tq
    t0 = i * tq
    tpos = t0 + lax.broadcasted_iota(jnp.int32, (1, tq), 1)
    tpos4 = jnp.concatenate([tpos] * NSA_GROUP, axis=1)
    q = q_ref[...]
    gates_t = gate_ref[...].T

    def gate_row(g, br):
        rows = [3 * (NSA_GROUP * g + j) + br for j in range(NSA_GROUP)]
        return jnp.concatenate([gates_t[r:r + 1, :] for r in rows], axis=1)

    npr = lax.broadcasted_iota(jnp.int32, (n_sub, 1), 0)
    cvalid = (npr >= 1) & (npr * CMP_STRIDE + (CMP_STRIDE - 1) <= tpos4)
    sb_o = lax.broadcasted_iota(jnp.int32, (nsp, n_sub), 0)
    np_o = lax.broadcasted_iota(jnp.int32, (nsp, n_sub), 1)
    r = SEL_BLOCK // CMP_STRIDE
    ovt = ((np_o >= r * sb_o) & (np_o <= r * sb_o + r) & (np_o >= 1)).astype(BF16)

    groups = range(NSA_KV_HEADS)
    qts = [jnp.concatenate([q[:, (NSA_GROUP * g + j) * HEAD_DIM:(NSA_GROUP * g + j + 1) * HEAD_DIM].T
                            for j in range(NSA_GROUP)], axis=1).astype(BF16) for g in groups]

    cs = [jnp.dot(kc_ref[0, g], qts[g], preferred_element_type=F32) for g in groups]
    cps = [p * inv for p, inv in (_softmax_cols(s, cvalid) for s in cs)]
    octs = [_dot_tn(vc_ref[0, g], cps[g]) for g in groups]
    imps = []
    for g in groups:
        psum = cps[g][:, 0:tq]
        for j in range(1, NSA_GROUP):
            psum = psum + cps[g][:, j * tq:(j + 1) * tq]
        imps.append(sum(jnp.dot(ovt, part, preferred_element_type=F32) for part in _split3(psum)))
    sel = _select_t(jnp.concatenate(imps, axis=1), tpos4, nsp)
    bias_t = jnp.where(sel, 0.0, NEG).astype(BF16)

    kpos = t0 - WINDOW + lax.broadcasted_iota(jnp.int32, (WINDOW + tq, 1), 0)
    wmask = (kpos <= tpos4) & (kpos >= tpos4 - WINDOW) & (kpos >= 0)
    ws = [jnp.dot(jnp.concatenate([w[:, g * HEAD_DIM:(g + 1) * HEAD_DIM] for w in wink_refs], axis=0), qts[g],
                  preferred_element_type=F32) for g in groups]
    wsb = [jnp.where(wmask, s, NEG).astype(BF16) for s in ws]
    wps = [jnp.exp2(s - jnp.max(s, axis=0, keepdims=True)) for s in wsb]
    wones = jnp.ones((16, WINDOW + tq), BF16)
    owts = []
    for g in groups:
        vwt = jnp.concatenate([w[g * HEAD_DIM:(g + 1) * HEAD_DIM, :] for w in winvt_refs], axis=1)
        pv = jnp.dot(jnp.concatenate([vwt, wones], axis=0), wps[g], preferred_element_type=F32)
        owts.append(pv[:HEAD_DIM] * (1.0 / pv[HEAD_DIM:HEAD_DIM + 1]))
    n_full = t0 // KV_TILE
    krow = lax.broadcasted_iota(jnp.int32, (KV_TILE, 1), 0)
    qaugs = []
    for g in range(NSA_KV_HEADS):
        bg = bias_t[:, g * tq:(g + 1) * tq]
        qaugs.append(jnp.concatenate([qts[g], jnp.concatenate([bg] * NSA_GROUP, axis=1)], axis=0))

    def tile_step(c, carries, diagonal):
        k0 = pl.multiple_of(c * KV_TILE, KV_TILE)
        onehot = _block_onehot(k0, KV_TILE, nsp)
        scores = []
        for g in range(NSA_KV_HEADS):
            kt = selk_ref[pl.ds(k0, KV_TILE), pl.ds(g * HEAD_DIM, HEAD_DIM)]
            s = jnp.dot(jnp.concatenate([kt, onehot], axis=1), qaugs[g], preferred_element_type=F32)
            if diagonal:
                s = jnp.where(k0 + krow <= tpos4, s, NEG)
            scores.append(s.astype(BF16))
        stats = []
        for g in range(NSA_KV_HEADS):
            m = carries[g][0]
            m_new = jnp.maximum(m, jnp.max(scores[g], axis=0, keepdims=True).astype(F32))
            stats.append((m_new, jnp.exp2(m - m_new), jnp.exp2(scores[g] - m_new.astype(BF16))))
        out = []
        ones = jnp.ones((16, KV_TILE), BF16)
        for g in range(NSA_KV_HEADS):
            m_new, alpha, p = stats[g]
            vtt = selvt_ref[pl.ds(g * HEAD_DIM, HEAD_DIM), pl.ds(k0, KV_TILE)]
            pv = jnp.dot(jnp.concatenate([vtt, ones], axis=0), p, preferred_element_type=F32)
            out.append((m_new, alpha * carries[g][1] + pv[HEAD_DIM:HEAD_DIM + 1], alpha * carries[g][2] + pv[:HEAD_DIM]))
        return tuple(out)

    init = tuple((jnp.full((1, lanes), NEG, F32), jnp.zeros((1, lanes), F32), jnp.zeros((HEAD_DIM, lanes), F32))
                 for _ in range(NSA_KV_HEADS))
    carries = lax.fori_loop(0, n_full, lambda c, cr: tile_step(c, cr, False), init)
    carries = tile_step(n_full, carries, True)

    for g in groups:
        m, l, acc = carries[g]
        o_s = acc * (1.0 / l)
        o_g = gate_row(g, 0) * octs[g] + gate_row(g, 1) * o_s + gate_row(g, 2) * owts[g]
        for j in range(NSA_GROUP):
            hd = NSA_GROUP * g + j
            oacc_ref[:, hd * HEAD_DIM:(hd + 1) * HEAD_DIM] = o_g[:, j * tq:(j + 1) * tq].T

    _finish(o_ref, oacc_ref, nz_ref[...], og_ref[...])


def _nsa_prompt(qn, gates, proj, kc, vc, selk, selvt, wink, winvt, nsa_g, t_len):
    n_sub = kc.shape[2]
    n_sel = t_len // SEL_BLOCK
    nsp = -(-n_sel // 128) * 128
    nq = t_len // Q_BLOCK
    nwin = WINDOW // Q_BLOCK + 1
    wblk = lambda r: (lambda i: jnp.maximum(i - (nwin - 1) + r, 0))
    win_specs = ([pl.BlockSpec((Q_BLOCK, KV_WIDTH), (lambda f: (lambda i: (f(i), 0)))(wblk(r))) for r in range(nwin)]
                 + [pl.BlockSpec((KV_WIDTH, Q_BLOCK), (lambda f: (lambda i: (0, f(i))))(wblk(r))) for r in range(nwin)])
    return pl.pallas_call(
        functools.partial(_nsa_prompt_kernel, n_sub=n_sub, nsp=nsp),
        grid=(nq,),
        in_specs=[pl.BlockSpec((Q_BLOCK, NSA_WIDTH), lambda i: (i, 0)),
                  pl.BlockSpec((Q_BLOCK, 128), lambda i: (i, 0)),
                  pl.BlockSpec((Q_BLOCK, NSA_WIDTH), lambda i: (i, COL_NZ // 2048)),
                  pl.BlockSpec((1, NSA_KV_HEADS, n_sub, HEAD_DIM), lambda i: (0, 0, 0, 0)),
                  pl.BlockSpec((1, NSA_KV_HEADS, n_sub, HEAD_DIM), lambda i: (0, 0, 0, 0)),
                  pl.BlockSpec((t_len, KV_WIDTH), lambda i: (0, 0), pipeline_mode=pl.Buffered(1)),
                  pl.BlockSpec((KV_WIDTH, t_len), lambda i: (0, 0), pipeline_mode=pl.Buffered(1))]
        + win_specs + [pl.BlockSpec((1, NSA_WIDTH), lambda i: (0, 0))],
        out_specs=pl.BlockSpec((Q_BLOCK, NSA_WIDTH), lambda i: (i, 0)),
        out_shape=jax.ShapeDtypeStruct((t_len, NSA_WIDTH), BF16),
        scratch_shapes=[pltpu.VMEM((Q_BLOCK, NSA_WIDTH), F32)],
        compiler_params=_cparams(("arbitrary",)),
        name="nsa_prompt",
    )(qn, gates, proj, kc, vc, selk, selvt, *([wink] * nwin), *([winvt] * nwin), nsa_g.reshape(1, NSA_WIDTH))


def _sample_select_kernel(q_ref, kc_ref, vc_ref, bias_ref, oc_ref, *, n_sub, nsp, past_len, t_new, bs):
    tq = t_new
    G = NSA_KV_HEADS
    tpos = past_len + lax.broadcasted_iota(jnp.int32, (tq, 1), 0)
    tpos4 = jnp.concatenate([tpos] * NSA_GROUP, axis=0)
    q = q_ref[...]
    psums = []
    for b in range(bs):
        for g in range(G):
            qg = jnp.concatenate([q[b * tq:(b + 1) * tq, (NSA_GROUP * g + j) * HEAD_DIM:(NSA_GROUP * g + j + 1) * HEAD_DIM]
                                  for j in range(NSA_GROUP)], axis=0)
            oc, p = _cmp_branch(qg.astype(BF16), kc_ref[b, g], vc_ref[b, g], tpos4, n_sub)
            oc_ref[b, g] = oc
            psum = p[0:tq]
            for j in range(1, NSA_GROUP):
                psum = psum + p[j * tq:(j + 1) * tq]
            psums.append(psum)
    imp = _importance(jnp.concatenate(psums, axis=0), n_sub, nsp)
    sel = _select(imp, jnp.concatenate([tpos] * (bs * G), axis=0), nsp)
    bias = jnp.where(sel, 0.0, NEG)
    lanes = G * NSA_GROUP * tq
    for b in range(bs):
        bb = jnp.concatenate([bias[(b * G + g) * tq:(b * G + g + 1) * tq]
                              for g in range(G) for _ in range(NSA_GROUP)], axis=0)
        for c in range(nsp // lanes):
            bias_ref[b, c * lanes:(c + 1) * lanes, :] = bb[:, c * lanes:(c + 1) * lanes].T


def _sample_select(qn, kc, vc, past_len, t_new):
    nb = kc.shape[0]
    n_sub = kc.shape[2]
    n_sel = past_len // SEL_BLOCK + 1
    nsp = -(-n_sel // 128) * 128
    bs = 8 if nb % 8 == 0 else nb
    rows = NSA_GROUP * t_new
    lanes = NSA_KV_HEADS * rows
    assert lanes == 128 and nsp % lanes == 0
    kv_spec = pl.BlockSpec((bs, NSA_KV_HEADS, n_sub, HEAD_DIM), lambda i: (i, 0, 0, 0))
    return pl.pallas_call(
        functools.partial(_sample_select_kernel, n_sub=n_sub, nsp=nsp, past_len=past_len, t_new=t_new, bs=bs),
        grid=(nb // bs,),
        in_specs=[pl.BlockSpec((bs * t_new, NSA_WIDTH), lambda i: (i, 0)), kv_spec, kv_spec],
        out_specs=[pl.BlockSpec((bs, nsp, lanes), lambda i: (i, 0, 0)),
                   pl.BlockSpec((bs, NSA_KV_HEADS, rows, HEAD_DIM), lambda i: (i, 0, 0, 0))],
        out_shape=[jax.ShapeDtypeStruct((nb, nsp, lanes), F32),
                   jax.ShapeDtypeStruct((nb, NSA_KV_HEADS, rows, HEAD_DIM), F32)],
        compiler_params=_cparams(("arbitrary",)),
        name="sample_select",
    )(qn, kc, vc)


def _nsa_sample_kernel(pt_ref, *refs, nsp, past_len, t_new):
    pages = refs[:PAGES_PER_TILE]
    (q_ref, gate_ref, nz_ref, bias_ref, oc_ref, newsel_ref, swin_ref, newwin_ref, og_ref,
     o_ref, qbd_ref, m_ref, l_ref, acc_ref, oacc_ref) = refs[PAGES_PER_TILE:]
    tau = pl.program_id(1)
    n_tau = pl.num_programs(1)
    tq = t_new
    G = NSA_KV_HEADS
    rows = NSA_GROUP * tq
    lanes = G * rows
    tpos = past_len + lax.broadcasted_iota(jnp.int32, (tq, 1), 0)
    tpos4 = jnp.concatenate([tpos] * NSA_GROUP, axis=0)
    lane_i = lax.broadcasted_iota(jnp.int32, (1, lanes), 1)

    def group_q():
        q = q_ref[...]
        return [jnp.concatenate([q[:, (NSA_GROUP * g + j) * HEAD_DIM:(NSA_GROUP * g + j + 1) * HEAD_DIM]
                                 for j in range(NSA_GROUP)], axis=0) for g in range(G)]

    def to_col(row_vec):
        eye = lax.broadcasted_iota(jnp.int32, (lanes, lanes), 0) == lax.broadcasted_iota(jnp.int32, (lanes, lanes), 1)
        return jnp.sum(jnp.where(eye, jnp.broadcast_to(row_vec, (lanes, lanes)), 0.0), axis=1, keepdims=True)

    def online(s, v_all, m, l, acc):
        m_new = jnp.maximum(m, jnp.max(s, axis=0, keepdims=True))
        alpha = jnp.exp2(m - m_new)
        p = jnp.exp2(s - m_new)
        l_new = alpha * l + jnp.sum(p, axis=0, keepdims=True)
        pv = _dot_tn(p, v_all)
        upd = jnp.concatenate([pv[g * rows:(g + 1) * rows, g * HEAD_DIM:(g + 1) * HEAD_DIM] for g in range(G)], axis=0)
        return m_new, l_new, to_col(alpha) * acc + upd

    @pl.when(tau == 0)
    def _():
        q_t = jnp.concatenate(group_q(), axis=0).T
        qbd_ref[...] = jnp.concatenate(
            [jnp.where(lane_i // rows == g, q_t, 0.0) for g in range(G)], axis=0).astype(BF16)
        m_ref[...] = jnp.full((1, lanes), NEG, F32)
        l_ref[...] = jnp.zeros((1, lanes), F32)
        acc_ref[...] = jnp.zeros((lanes, HEAD_DIM), F32)

    planes = [jnp.swapaxes(p[0], 0, 1) for p in pages]
    k_all = jnp.concatenate([jnp.concatenate([pl_[g] for g in range(G)], axis=1) for pl_ in planes], axis=0)
    v_all = jnp.concatenate([jnp.concatenate([pl_[G + g] for g in range(G)], axis=1) for pl_ in planes], axis=0)
    n_keys = PAGES_PER_TILE * PAGE
    n_blk = n_keys // SEL_BLOCK
    b_rows = bias_ref[0, pl.ds(pl.multiple_of(tau * n_blk, n_blk), n_blk), :]
    bias_keys = jnp.broadcast_to(b_rows[:, None, :], (n_blk, SEL_BLOCK, lanes)).reshape(n_keys, lanes)
    s = jnp.dot(k_all.astype(BF16), qbd_ref[...], preferred_element_type=F32) + bias_keys
    m, l, acc = online(s, v_all, m_ref[...], l_ref[...], acc_ref[...])
    m_ref[...] = m
    l_ref[...] = l
    acc_ref[...] = acc

    @pl.when(tau == n_tau - 1)
    def _():
        qgs = group_q()
        gates = gate_ref[...]
        n_buf = swin_ref.shape[1]
        kpos = past_len - n_buf + lax.broadcasted_iota(jnp.int32, (1, n_buf + tq), 1)
        last_blk = past_len // SEL_BLOCK
        pad = 16
        new = newsel_ref[...]
        zpad = jnp.zeros((pad - tq, G * HEAD_DIM), F32)
        kn = jnp.concatenate([jnp.concatenate([new[:, g, :] for g in range(G)], axis=1), zpad], axis=0)
        vn = jnp.concatenate([jnp.concatenate([new[:, G + g, :] for g in range(G)], axis=1), zpad], axis=0)
        krow = lax.broadcasted_iota(jnp.int32, (pad, 1), 0)
        s_new = jnp.dot(kn.astype(BF16), qbd_ref[...], preferred_element_type=F32) + bias_ref[0, last_blk:last_blk + 1, :]
        s_new = jnp.where((krow <= lane_i % tq) & (krow < tq), s_new, NEG)
        m, l, acc = online(s_new, vn, m_ref[...], l_ref[...], acc_ref[...])
        o_sel = acc * to_col(1.0 / l)
        swin = jnp.swapaxes(swin_ref[0], 0, 1)
        nwin = newwin_ref[...]
        for g in range(G):
            kw = jnp.concatenate([swin[g], nwin[:, g, :]], axis=0)
            vw = jnp.concatenate([swin[G + g], nwin[:, G + g, :]], axis=0)
            o_w = _window_branch(qgs[g], kw, vw, kpos, tpos4)
            o_g = (_gate_col(gates, g, 0, tq) * oc_ref[0, g] + _gate_col(gates, g, 1, tq) * o_sel[g * rows:(g + 1) * rows]
                   + _gate_col(gates, g, 2, tq) * o_w)
            for j in range(NSA_GROUP):
                hd = NSA_GROUP * g + j
                oacc_ref[:, hd * HEAD_DIM:(hd + 1) * HEAD_DIM] = o_g[j * tq:(j + 1) * tq]
        _finish(o_ref, oacc_ref, nz_ref[...], og_ref[...])


def _nsa_sample(cache, page_table, qn, gates, proj, bias_t, oc, rows_s, state_win, win_s, nsa_g, row0, t_new):
    nb, n_pages = page_table.shape
    past_len = n_pages * PAGE
    nsp = bias_t.shape[1]
    n_tau = n_pages // PAGES_PER_TILE
    rb = row0 // t_new
    n_buf = state_win.shape[1]
    rows = NSA_GROUP * t_new
    lanes = NSA_KV_HEADS * rows
    assert lanes == 128 and nsp % lanes == 0, "the sample kernel packs all query rows of a batch into one lane tile"
    cg = 2 * NSA_KV_HEADS

    def page_spec(r):
        return pl.BlockSpec((1, PAGE, cg, HEAD_DIM),
                            lambda b, i, pt: (pt[b, i * PAGES_PER_TILE + r], 0, 1, 0))

    grid_spec = pltpu.PrefetchScalarGridSpec(
        num_scalar_prefetch=1,
        grid=(nb, n_tau),
        in_specs=[page_spec(r) for r in range(PAGES_PER_TILE)] + [
            pl.BlockSpec((t_new, NSA_WIDTH), lambda b, i, pt: (b, 0)),
            pl.BlockSpec((t_new, 128), lambda b, i, pt: (b, 0)),
            pl.BlockSpec((t_new, NSA_WIDTH), lambda b, i, pt: (rb + b, COL_NZ // 2048)),
            pl.BlockSpec((1, nsp, lanes), lambda b, i, pt: (b, 0, 0)),
            pl.BlockSpec((1, NSA_KV_HEADS, rows, HEAD_DIM), lambda b, i, pt: (b, 0, 0, 0)),
            pl.BlockSpec((t_new, cg, HEAD_DIM), lambda b, i, pt: (b, 1, 0)),
            pl.BlockSpec((1, n_buf, cg, HEAD_DIM), lambda b, i, pt: (b, 0, 0, 0)),
            pl.BlockSpec((t_new, cg, HEAD_DIM), lambda b, i, pt: (b, 0, 0)),
            pl.BlockSpec((1, NSA_WIDTH), lambda b, i, pt: (0, 0))],
        out_specs=pl.BlockSpec((t_new, NSA_WIDTH), lambda b, i, pt: (b, 0)),
        scratch_shapes=[pltpu.VMEM((NSA_KV_HEADS * HEAD_DIM, lanes), BF16),
                        pltpu.VMEM((1, lanes), F32),
                        pltpu.VMEM((1, lanes), F32),
                        pltpu.VMEM((lanes, HEAD_DIM), F32),
                        pltpu.VMEM((t_new, NSA_WIDTH), F32)],
    )
    return pl.pallas_call(
        functools.partial(_nsa_sample_kernel, nsp=nsp, past_len=past_len, t_new=t_new),
        grid_spec=grid_spec,
        out_shape=jax.ShapeDtypeStruct((nb * t_new, NSA_WIDTH), F32),
        compiler_params=_cparams(("arbitrary", "arbitrary")),
        name="nsa_sample",
    )(page_table, *([cache] * PAGES_PER_TILE), qn, gates, proj, bias_t, oc, rows_s, state_win, win_s,
      nsa_g.reshape(1, NSA_WIDTH))


def _out_kernel(oh_ref, on_ref, wh_ref, wn_ref, x_ref, gate_ref, y_ref):
    acc = _dot(oh_ref[...], wh_ref[...]) + _dot(on_ref[...], wn_ref[...])
    y_ref[...] = x_ref[...] + gate_ref[...] * acc


def _out_proj(o_h, o_n, w_out, x, gate, tm, tn=512):
    m, d = x.shape
    half = o_h.shape[1]
    grow = gate.shape[0]
    gmap = (lambda i, j: (0, j)) if grow == 1 else (lambda i, j: (i, j))
    return pl.pallas_call(
        _out_kernel,
        grid=(m // tm, d // tn),
        in_specs=[pl.BlockSpec((tm, half), lambda i, j: (i, 0)),
                  pl.BlockSpec((tm, half), lambda i, j: (i, 0)),
                  pl.BlockSpec((half, tn), lambda i, j: (0, j)),
                  pl.BlockSpec((half, tn), lambda i, j: (1, j)),
                  pl.BlockSpec((tm, tn), lambda i, j: (i, j)),
                  pl.BlockSpec((1 if grow == 1 else tm, tn), gmap)],
        out_specs=pl.BlockSpec((tm, tn), lambda i, j: (i, j)),
        out_shape=jax.ShapeDtypeStruct((m, d), F32),
        compiler_params=_cparams(("arbitrary", "arbitrary")),
        name="out_proj",
    )(o_h, o_n, w_out, w_out, x, gate)


def _layer(xp, xs, c_all, cache, s_win, s_hgrn, page_table, lb_param, layer, norm_g, w_ada, b_ada, w_in,
           hgrn_out_g, q_g, k_g, pe, w1, w2, nsa_out_g, w_out):
    t_len = xp.shape[0]
    nb, t_new, _ = xs.shape
    n_s = nb * t_new
    d = D_MODEL

    mod = _ada(c_all, w_ada, b_ada)
    shift, scale, gate = mod[:, :d], mod[:, d:2 * d], mod[:, 2 * d:]
    rep = lambda a: jnp.pad(jnp.repeat(a[1:1 + nb], t_new, axis=0), ((0, SAMPLE_ROWS - n_s), (0, 0)))
    xs_pad = jnp.pad(xs.reshape(n_s, d), ((0, SAMPLE_ROWS - n_s), (0, 0)))
    h_all = _norm(xp, xs_pad, norm_g.reshape(1, d), scale[0:1], shift[0:1], rep(scale), rep(shift))

    proj = _matmul(h_all, jnp.swapaxes(w_in, 0, 1))

    oh_p, st_p = _hgrn(proj, lb_param, hgrn_out_g, None, row0=0, n_batch=1, t_len=t_len,
                       chunk=min(64, t_len), out_dtype=BF16, layer=layer)
    oh_s, st_s = _hgrn(proj, lb_param, hgrn_out_g, s_hgrn, row0=t_len, n_batch=nb, t_len=t_new,
                       chunk=t_new, out_dtype=F32, layer=layer)

    qn_p, rows_p, win_p3, selk_p, selvt_p, wink_p, winvt_p, gates_p = _prep(proj, q_g, k_g, 0, t_len)
    qn_s, rows_s, win_s3, _, _, _, _, gates_s = _prep(proj, q_g, k_g, t_len, SAMPLE_ROWS)

    w1r = w1.reshape(2, 2, CMP_STRIDE * HEAD_DIM, CMP_HIDDEN)
    wc = jnp.concatenate([w1r[:, 0], w1r[:, 1]], axis=-1).astype(BF16)
    pe2 = pe.reshape(2, 2, 1, CMP_STRIDE * HEAD_DIM)
    w2b = w2.astype(BF16)

    n_pp = t_len // PAGE
    kc_p, vc_p = _compress(rows_p.reshape(n_pp, PAGE, 4 * NSA_KV_HEADS, HEAD_DIM),
                           jnp.arange(n_pp, dtype=jnp.int32).reshape(1, n_pp), wc, pe2, w2b, k_g[0])
    cache4 = cache.reshape(cache.shape[0], PAGE, 4 * NSA_KV_HEADS, HEAD_DIM)
    kc_s, vc_s = _compress(cache4, page_table, wc, pe2, w2b, k_g[0])

    on_p = _nsa_prompt(qn_p, gates_p, proj, kc_p, vc_p, selk_p, selvt_p, wink_p, winvt_p, nsa_out_g, t_len)
    n_buf = s_win.shape[1]
    s_win4 = s_win.reshape(nb, n_buf, 2 * NSA_KV_HEADS, HEAD_DIM)
    bias_s, oc_s = _sample_select(qn_s, kc_s, vc_s, page_table.shape[1] * PAGE, t_new)
    on_s = _nsa_sample(cache4, page_table, qn_s, gates_s, proj, bias_s, oc_s, rows_s, s_win4, win_s3,
                       nsa_out_g, t_len, t_new)

    y_p = _out_proj(oh_p, on_p, w_out, xp, gate[0:1], tm=min(1024, t_len))
    y_s = _out_proj(oh_s, on_s, w_out, xs.reshape(n_s, d), jnp.repeat(gate[1:1 + nb], t_new, axis=0), tm=n_s)

    n_w = min(WINDOW, t_len)
    kv_p = rows_p.reshape(1, t_len, 4, NSA_KV_HEADS, HEAD_DIM)
    kv_s = rows_s[:n_s].reshape(nb, t_new, 4, NSA_KV_HEADS, HEAD_DIM)
    win_p = win_p3[t_len - n_w:].reshape(1, n_w, 2, NSA_KV_HEADS, HEAD_DIM)
    win_new = win_s3[:n_s].reshape(nb, t_new, 2, NSA_KV_HEADS, HEAD_DIM)
    win_s = jnp.concatenate([s_win, win_new], axis=1)[:, -n_buf:]
    return y_p, y_s.reshape(nb, t_new, d), kv_p, kv_s, win_p, win_s, st_p, st_s


def kernel(x_prompt, x_sample, cache_kv, state_win, state_hgrn, page_table, c_prompt, c_sample, norm_g, w_ada,
           b_ada, w_in, hgrn_lb, hgrn_out_g, q_norm_g, k_norm_g, cmp_pe, cmp_w1, cmp_w2, nsa_out_g, w_out):
    depth = w_in.shape[0]
    assert depth == 1 and x_prompt.shape[0] == 1
    nb = x_sample.shape[0]
    c_all = jnp.concatenate([c_prompt, c_sample], axis=0)
    c_all = jnp.pad(c_all, ((0, (-c_all.shape[0]) % 8), (0, 0)))
    l = 0
    outs = _layer(x_prompt[0], x_sample, c_all, cache_kv[l], state_win[l], state_hgrn[l], page_table,
                  hgrn_lb, l, norm_g[l], w_ada[l], b_ada[l], w_in[l], hgrn_out_g[l], q_norm_g[l], k_norm_g[l],
                  cmp_pe[l], cmp_w1[l], cmp_w2[l], nsa_out_g[l], w_out[l])
    y_p, y_s, kv_p, kv_s, win_p, win_s, st_p, st_s = outs
    return (y_p[None], y_s, kv_p[None], kv_s[None], win_p[None], win_s[None], st_p[None], st_s[None])
```

```python
import functools

import jax
import jax.numpy as jnp
from jax import lax
from jax.experimental import pallas as pl
from jax.experimental.pallas import tpu as pltpu

F32 = jnp.float32
BF16 = jnp.bfloat16

D_MODEL = 4096
HEAD_DIM = 128
HGRN_WIDTH = 2048
NSA_WIDTH = 2048
HGRN_HEADS = 16
NSA_HEADS = 16
NSA_KV_HEADS = 4
NSA_GROUP = 4
KV_WIDTH = 512
CMP_LEN = 32
CMP_STRIDE = 16
CMP_HIDDEN = 256
SEL_BLOCK = 64
SEL_TOP = 16
WINDOW = 512
Q_BLOCK = 128
NORM_EPS = 1e-6
PAGE = 128

COL_HQ, COL_HF, COL_HI, COL_HZ, COL_NQ, COL_ROWS, COL_NZ = (i * 2048 for i in range(7))
COL_WIN = 14336
COL_NG = 15360
N_PROJ = 15872
SAMPLE_ROWS = 256
ROW_TILE = 256
PAGES_PER_TILE = 16
CMP_PAGES = 32
KV_TILE = 512
NEG = -1e30
LOG2E = 1.4426950408889634
VMEM_LIMIT = 56 * 1024 * 1024


def _cparams(sem):
    return pltpu.CompilerParams(dimension_semantics=sem, vmem_limit_bytes=VMEM_LIMIT)


def _dot(a, b):
    return jnp.dot(a.astype(BF16), b.astype(BF16), preferred_element_type=F32)


def _dot_nt(a, b):
    return lax.dot_general(a.astype(BF16), b.astype(BF16), (((1,), (1,)), ((), ())),
                           preferred_element_type=F32)


def _dot_tn(a, b):
    return lax.dot_general(a.astype(BF16), b.astype(BF16), (((0,), (0,)), ((), ())),
                           preferred_element_type=F32)


def _silu(x):
    return x * jax.nn.sigmoid(x)


def _ada_kernel(c_ref, w_ref, b_ref, o_ref):
    c = c_ref[...]
    o_ref[...] = _dot(_silu(c), w_ref[...]) + b_ref[...]


def _ada(c_all, w_ada, b_ada):
    m, d = c_all.shape
    n = w_ada.shape[1]
    tn = 512
    return pl.pallas_call(
        _ada_kernel,
        grid=(n // tn,),
        in_specs=[pl.BlockSpec((m, d), lambda j: (0, 0)),
                  pl.BlockSpec((d, tn), lambda j: (0, j)),
                  pl.BlockSpec((1, tn), lambda j: (0, j))],
        out_specs=pl.BlockSpec((m, tn), lambda j: (0, j)),
        out_shape=jax.ShapeDtypeStruct((m, n), F32),
        compiler_params=_cparams(("arbitrary",)),
        name="ada",
    )(c_all, w_ada, b_ada.reshape(1, n))


def _norm_kernel(xp_ref, xs_ref, g_ref, scp_ref, shp_ref, scs_ref, shs_ref, o_ref, *, n_prompt):
    i = pl.program_id(0)

    def body(x, scale, shift):
        r = x * lax.rsqrt(jnp.mean(x * x, axis=-1, keepdims=True) + NORM_EPS) * g_ref[...]
        o_ref[...] = (r * (1.0 + scale) + shift).astype(o_ref.dtype)

    @pl.when(i < n_prompt)
    def _():
        body(xp_ref[...], scp_ref[...], shp_ref[...])

    @pl.when(i >= n_prompt)
    def _():
        body(xs_ref[...], scs_ref[...], shs_ref[...])


def _norm(xp, xs, g, sc_p, sh_p, sc_s, sh_s):
    t, d = xp.shape
    n_prompt = t // ROW_TILE
    row = lambda i: (0, 0)
    return pl.pallas_call(
        functools.partial(_norm_kernel, n_prompt=n_prompt),
        grid=(n_prompt + 1,),
        in_specs=[pl.BlockSpec((ROW_TILE, d), lambda i: (jnp.minimum(i, n_prompt - 1), 0)),
                  pl.BlockSpec((SAMPLE_ROWS, d), row),
                  pl.BlockSpec((1, d), row), pl.BlockSpec((1, d), row), pl.BlockSpec((1, d), row),
                  pl.BlockSpec((SAMPLE_ROWS, d), row), pl.BlockSpec((SAMPLE_ROWS, d), row)],
        out_specs=pl.BlockSpec((ROW_TILE, d), lambda i: (i, 0)),
        out_shape=jax.ShapeDtypeStruct((t + SAMPLE_ROWS, d), BF16),
        compiler_params=_cparams(("arbitrary",)),
        name="norm",
    )(xp, xs, g, sc_p, sh_p, sc_s, sh_s)


PROJ_TN = 512
_N_MAIN_A, _N_NZ, _N_WIN = 24, 4, 2
_ROW_WIN = _N_MAIN_A * PROJ_TN
_ROW_NG = _ROW_WIN + _N_WIN * PROJ_TN
_ROW_NZ = _ROW_NG + 3 * NSA_HEADS


_ROW_UNIT = 16


def _proj_weight_row(j):
    u = PROJ_TN // _ROW_UNIT
    q = jnp.where(j < _N_MAIN_A, j * u,
                  jnp.where(j < _N_MAIN_A + _N_NZ, _ROW_NZ // _ROW_UNIT + (j - _N_MAIN_A) * u,
                            jnp.where(j < _N_MAIN_A + _N_NZ + _N_WIN,
                                      _ROW_WIN // _ROW_UNIT + (j - _N_MAIN_A - _N_NZ) * u, _ROW_NG // _ROW_UNIT)))
    return q * _ROW_UNIT


def _mm_kernel(a_ref, wt_ref, o_ref, wb_ref):
    @pl.when(pl.program_id(1) == 0)
    def _():
        wb_ref[...] = wt_ref[...].astype(BF16)

    o_ref[...] = lax.dot_general(a_ref[...], wb_ref[...], (((1,), (1,)), ((), ())), preferred_element_type=F32)


def _row_tile(m, cap=1100):
    best = 16
    for tm in range(16, cap + 1, 16):
        if m % tm == 0:
            best = tm
    return best


def _matmul(a, w_t):
    m, k = a.shape
    tn = PROJ_TN
    tm = _row_tile(m)
    return pl.pallas_call(
        _mm_kernel,
        grid=(N_PROJ // tn, m // tm),
        in_specs=[pl.BlockSpec((tm, k), lambda j, i: (i, 0)),
                  pl.BlockSpec((pl.Element(tn), pl.Element(k)), lambda j, i: (_proj_weight_row(j), 0))],
        out_specs=pl.BlockSpec((tm, tn), lambda j, i: (i, j)),
        out_shape=jax.ShapeDtypeStruct((m, N_PROJ), F32),
        scratch_shapes=[pltpu.VMEM((tn, k), BF16)],
        compiler_params=_cparams(("arbitrary", "arbitrary")),
        name="proj",
    )(a, w_t)


def _hgrn_kernel(*refs, chunk, levels, has_s0, layer):
    if has_s0:
        lbp_ref, og_ref, hq_ref, hf_ref, hi_ref, hz_ref, s0_ref, o_ref, sout_ref, st_ref = refs
    else:
        lbp_ref, og_ref, hq_ref, hf_ref, hi_ref, hz_ref, o_ref, sout_ref, st_ref = refs
        s0_ref = None
    C = chunk
    c = pl.program_id(1)
    n_chunks = pl.num_programs(1)

    @pl.when(c == 0)
    def _():
        for h in range(HGRN_HEADS):
            if has_s0:
                st_ref[h] = s0_ref[0, h].T
            else:
                st_ref[h] = jnp.zeros((HEAD_DIM, HEAD_DIM), F32)

    lbp = lbp_ref[...]
    e = jnp.exp(lbp - jnp.max(lbp, axis=0, keepdims=True))
    lb = jnp.sum(e[:layer + 1], axis=0, keepdims=True) / jnp.sum(e, axis=0, keepdims=True)

    f = lb + (1.0 - lb) * jax.nn.sigmoid(hf_ref[...])
    logf = jnp.log(f)
    k = 1.0 - f
    q = hq_ref[...]
    v = hi_ref[...]
    hz = hz_ref[...]
    n = q.shape[1]

    row = lax.broadcasted_iota(jnp.int32, (C, 1), 0)
    cum = logf
    s = 1
    while s < C:
        cum = cum + jnp.where(row >= s, pltpu.roll(cum, s, axis=0), 0.0)
        s *= 2
    cum = cum * LOG2E
    last = cum[C - 1:C, :]
    qe = q * jnp.exp2(cum)
    kd = k * jnp.exp2(last - cum)
    elast = jnp.exp2(last)

    lvl = []
    for h in levels:
        np_ = C // (2 * h)
        c4 = cum.reshape(np_, 2, h, n)
        cmid = c4[:, 0, h - 1:h, :]
        zero = jnp.zeros((np_, 1, h, n), F32)
        qr = q.reshape(np_, 2, h, n)[:, 1] * jnp.exp2(c4[:, 1] - cmid)
        kl_ = k.reshape(np_, 2, h, n)[:, 0] * jnp.exp2(cmid - c4[:, 0])
        ql = jnp.concatenate([zero, qr[:, None]], axis=1).reshape(C, n)
        kl = jnp.concatenate([kl_[:, None], zero], axis=1).reshape(C, n)
        ti = lax.broadcasted_iota(jnp.int32, (C, C), 0)
        si = lax.broadcasted_iota(jnp.int32, (C, C), 1)
        mask = ((ti // (2 * h)) == (si // (2 * h))) & (((ti // h) % 2) == 1) & (((si // h) % 2) == 0)
        lvl.append((ql, kl, mask))

    nb = C // 8
    rin = lax.broadcasted_iota(jnp.int32, (nb, 8, 1), 1)
    og = og_ref[...]
    for h in range(HGRN_HEADS):
        sl = slice(h * HEAD_DIM, (h + 1) * HEAD_DIM)
        st = st_ref[h]
        q_h, k_h, v_h, cum_h = q[:, sl], k[:, sl], v[:, sl], cum[:, sl]
        o_h = _dot_nt(qe[:, sl], st)
        if levels:
            a = jnp.zeros((C, C), F32)
            for ql, kl, mask in lvl:
                a = a + jnp.where(mask, _dot_nt(ql[:, sl], kl[:, sl]), 0.0)
            o_h = o_h + _dot(a, v_h)
        q3 = q_h.reshape(nb, 8, HEAD_DIM)
        k3 = k_h.reshape(nb, 8, HEAD_DIM)
        v3 = v_h.reshape(nb, 8, HEAD_DIM)
        c3 = cum_h.reshape(nb, 8, HEAD_DIM)
        od = jnp.zeros((nb, 8, HEAD_DIM), F32)
        for s in range(8):
            dec = jnp.exp2(jnp.where(rin >= s, c3 - c3[:, s:s + 1, :], -jnp.inf))
            w = q3 * k3[:, s:s + 1, :] * dec
            od = od + jnp.sum(w, axis=-1, keepdims=True) * v3[:, s:s + 1, :]
        o_h = o_h + od.reshape(C, HEAD_DIM)
        st_ref[h] = st * elast[:, sl] + _dot_tn(v_h, kd[:, sl])
        o_h = o_h * lax.rsqrt(jnp.mean(o_h * o_h, axis=-1, keepdims=True) + NORM_EPS) * og
        o_ref[:, sl] = (o_h * _silu(hz[:, sl])).astype(o_ref.dtype)

    @pl.when(c == n_chunks - 1)
    def _():
        for h in range(HGRN_HEADS):
            sout_ref[0, h] = st_ref[h].T


def _hgrn(proj, lb_param, out_g, s0, *, row0, n_batch, t_len, chunk, out_dtype, layer):
    n_chunks = t_len // chunk
    levels = []
    h = chunk // 2
    while h >= 8:
        levels.append(h)
        h //= 2
    blk0 = row0 // chunk
    rmap = lambda col: (lambda b, c: (blk0 + b * n_chunks + c, col))
    in_specs = [pl.BlockSpec(lb_param.shape, lambda b, c: (0, 0)),
                pl.BlockSpec((1, HEAD_DIM), lambda b, c: (0, 0)),
                pl.BlockSpec((chunk, HGRN_WIDTH), rmap(0)),
                pl.BlockSpec((chunk, HGRN_WIDTH), rmap(1)),
                pl.BlockSpec((chunk, HGRN_WIDTH), rmap(2)),
                pl.BlockSpec((chunk, HGRN_WIDTH), rmap(3))]
    args = [lb_param, out_g.reshape(1, HEAD_DIM), proj, proj, proj, proj]
    if s0 is not None:
        in_specs.append(pl.BlockSpec((1, HGRN_HEADS, HEAD_DIM, HEAD_DIM), lambda b, c: (b, 0, 0, 0)))
        args.append(s0)
    return pl.pallas_call(
        functools.partial(_hgrn_kernel, chunk=chunk, levels=tuple(levels), has_s0=s0 is not None, layer=layer),
        grid=(n_batch, n_chunks),
        in_specs=in_specs,
        out_specs=[pl.BlockSpec((chunk, HGRN_WIDTH), lambda b, c: (b * n_chunks + c, 0)),
                   pl.BlockSpec((1, HGRN_HEADS, HEAD_DIM, HEAD_DIM), lambda b, c: (b, 0, 0, 0))],
        out_shape=[jax.ShapeDtypeStruct((n_batch * t_len, HGRN_WIDTH), out_dtype),
                   jax.ShapeDtypeStruct((n_batch, HGRN_HEADS, HEAD_DIM, HEAD_DIM), F32)],
        scratch_shapes=[pltpu.VMEM((HGRN_HEADS, HEAD_DIM, HEAD_DIM), F32)],
        compiler_params=_cparams(("arbitrary", "arbitrary")),
        name="hgrn",
    )(*args)


def _head_rms(x, g, n_heads):
    outs = []
    for h in range(n_heads):
        xh = x[:, h * HEAD_DIM:(h + 1) * HEAD_DIM]
        outs.append(xh * lax.rsqrt(jnp.mean(xh * xh, axis=-1, keepdims=True) + NORM_EPS) * g)
    return jnp.concatenate(outs, axis=1)


def _prep_kernel(nq_ref, rows_ref, win_ref, ng_ref, qg_ref, kg_ref,
                 q_ref, rows_o, win_o, selk_o, selvt_o, wink_o, winvt_o, gate_o, *, q_transposed):
    kg = kg_ref[...]
    qn = _head_rms(nq_ref[...], qg_ref[...], NSA_HEADS) * (HEAD_DIM ** -0.5 * LOG2E)
    if q_transposed:
        q_ref[...] = qn.T.astype(q_ref.dtype)
    else:
        q_ref[...] = qn
    rows = rows_ref[...]
    ks = _head_rms(rows[:, 2 * KV_WIDTH:3 * KV_WIDTH], kg[1:2], NSA_KV_HEADS)
    for cg in range(4 * NSA_KV_HEADS):
        sl = slice(cg * HEAD_DIM, (cg + 1) * HEAD_DIM)
        if 2 * NSA_KV_HEADS <= cg < 3 * NSA_KV_HEADS:
            rows_o[:, cg, :] = ks[:, (cg - 2 * NSA_KV_HEADS) * HEAD_DIM:(cg - 2 * NSA_KV_HEADS + 1) * HEAD_DIM]
        else:
            rows_o[:, cg, :] = rows[:, sl]
    selk_o[...] = ks.astype(BF16)
    selvt_o[...] = rows[:, 3 * KV_WIDTH:].T.astype(BF16)
    win = win_ref[...]
    kw = _head_rms(win[:, :KV_WIDTH], kg[2:3], NSA_KV_HEADS)
    for cg in range(2 * NSA_KV_HEADS):
        sl = slice(cg * HEAD_DIM, (cg + 1) * HEAD_DIM)
        win_o[:, cg, :] = kw[:, sl] if cg < NSA_KV_HEADS else win[:, sl]
    wink_o[...] = kw.astype(BF16)
    winvt_o[...] = win[:, KV_WIDTH:].T.astype(BF16)
    gate_o[...] = jax.nn.sigmoid(ng_ref[...])


def _prep(proj, q_g, k_g, row0, n_rows, q_transposed):
    t = min(ROW_TILE, n_rows)
    r0 = row0 // t
    q_spec = pl.BlockSpec((NSA_WIDTH, t), lambda i: (0, i)) if q_transposed else pl.BlockSpec((t, 2048), lambda i: (i, 0))
    q_shape = (jax.ShapeDtypeStruct((NSA_WIDTH, n_rows), BF16) if q_transposed
               else jax.ShapeDtypeStruct((n_rows, NSA_WIDTH), F32))
    return pl.pallas_call(
        functools.partial(_prep_kernel, q_transposed=q_transposed),
        grid=(n_rows // t,),
        in_specs=[pl.BlockSpec((t, NSA_WIDTH), lambda i: (r0 + i, COL_NQ // 2048)),
                  pl.BlockSpec((t, 2048), lambda i: (r0 + i, COL_ROWS // 2048)),
                  pl.BlockSpec((t, 1024), lambda i: (r0 + i, COL_WIN // 1024)),
                  pl.BlockSpec((t, 128), lambda i: (r0 + i, COL_NG // 128)),
                  pl.BlockSpec((1, HEAD_DIM), lambda i: (0, 0)),
                  pl.BlockSpec((3, HEAD_DIM), lambda i: (0, 0))],
        out_specs=[q_spec,
                   pl.BlockSpec((t, 4 * NSA_KV_HEADS, HEAD_DIM), lambda i: (i, 0, 0)),
                   pl.BlockSpec((t, 2 * NSA_KV_HEADS, HEAD_DIM), lambda i: (i, 0, 0)),
                   pl.BlockSpec((t, KV_WIDTH), lambda i: (i, 0)),
                   pl.BlockSpec((KV_WIDTH, t), lambda i: (0, i)),
                   pl.BlockSpec((t, KV_WIDTH), lambda i: (i, 0)),
                   pl.BlockSpec((KV_WIDTH, t), lambda i: (0, i)),
                   pl.BlockSpec((t, 128), lambda i: (i, 0))],
        out_shape=[q_shape,
                   jax.ShapeDtypeStruct((n_rows, 4 * NSA_KV_HEADS, HEAD_DIM), F32),
                   jax.ShapeDtypeStruct((n_rows, 2 * NSA_KV_HEADS, HEAD_DIM), F32),
                   jax.ShapeDtypeStruct((n_rows, KV_WIDTH), BF16),
                   jax.ShapeDtypeStruct((KV_WIDTH, n_rows), BF16),
                   jax.ShapeDtypeStruct((n_rows, KV_WIDTH), BF16),
                   jax.ShapeDtypeStruct((KV_WIDTH, n_rows), BF16),
                   jax.ShapeDtypeStruct((n_rows, 128), F32)],
        compiler_params=_cparams(("arbitrary",)),
        name="prep",
    )(proj, proj, proj, proj, q_g.reshape(1, HEAD_DIM), k_g)


def _compress_kernel(pt_ref, *refs, n_pg):
    pages = refs[:n_pg]
    wc_ref, pe_ref, w2_ref, kcg_ref, kc_ref, vc_ref, cvec_ref, carry_ref, out_scr = refs[n_pg:]
    b = pl.program_id(0)
    i = pl.program_id(1)
    nsub = PAGE // CMP_STRIDE
    ntile = nsub * n_pg
    G = NSA_KV_HEADS
    rows = ntile * G

    @pl.when((b == 0) & (i == 0))
    def _():
        for ch in range(2):
            lo = jnp.broadcast_to(pe_ref[ch, 0], (8, CMP_STRIDE * HEAD_DIM))
            hi = jnp.broadcast_to(pe_ref[ch, 1], (8, CMP_STRIDE * HEAD_DIM))
            cvec_ref[ch] = _dot(lo, wc_ref[ch][:, :CMP_HIDDEN]) + _dot(hi, wc_ref[ch][:, CMP_HIDDEN:])

    @pl.when(i == 0)
    def _():
        carry_ref[...] = jnp.zeros(carry_ref.shape, F32)

    def gathered(p):
        lo, hi = [], []
        for r in range(n_pg):
            a = pages[r][0, pl.ds(p, nsub // 2, stride=2 * CMP_STRIDE), :, :]
            bb = pages[r][0, pl.ds(p + CMP_STRIDE, nsub // 2, stride=2 * CMP_STRIDE), :, :]
            lo.append(jnp.concatenate([a[:, :G], bb[:, :G]], axis=1).reshape(nsub * G, HEAD_DIM))
            hi.append(jnp.concatenate([a[:, G:], bb[:, G:]], axis=1).reshape(nsub * G, HEAD_DIM))
        return jnp.concatenate(lo, axis=0).astype(BF16), jnp.concatenate(hi, axis=0).astype(BF16)

    xs = [gathered(p) for p in range(CMP_STRIDE)]
    u = [jnp.dot(jnp.concatenate([x[ch] for x in xs], axis=1), wc_ref[ch], preferred_element_type=F32)
         for ch in range(2)]

    row8 = lax.broadcasted_iota(jnp.int32, (8, 1), 0)
    for ch in range(2):
        u0, u1 = u[ch][:, :CMP_HIDDEN], u[ch][:, CMP_HIDDEN:]
        rolled = pltpu.roll(u0, G, axis=0)
        head = jnp.where(row8 < G, carry_ref[ch], rolled[0:8])
        carry_ref[ch] = rolled[0:8]
        prev = jnp.concatenate([head, rolled[8:]], axis=0)
        pre = prev + u1 + cvec_ref[ch][0:1, :]
        out = _dot(_silu(pre), w2_ref[ch])
        if ch == 0:
            out = out * lax.rsqrt(jnp.mean(out * out, axis=-1, keepdims=True) + NORM_EPS) * kcg_ref[...]
        out_scr[...] = out
        dst = kc_ref if ch == 0 else vc_ref
        for g in range(G):
            dst[0, g] = out_scr[pl.ds(g, ntile, stride=G), :].astype(dst.dtype)


def _compress(src, page_table, wc, pe2, w2, kc_g):
    nb, n_pages = page_table.shape
    n_pg = CMP_PAGES if n_pages % CMP_PAGES == 0 else PAGES_PER_TILE
    n_tiles = n_pages // n_pg
    nsub = PAGE // CMP_STRIDE
    ntile = nsub * n_pg

    def page_spec(r):
        return pl.BlockSpec((1, PAGE, 2 * NSA_KV_HEADS, HEAD_DIM),
                            lambda b, i, pt: (pt[b, i * n_pg + r], 0, 0, 0))

    const = lambda *shape: pl.BlockSpec(shape, lambda b, i, pt: (0,) * len(shape))
    out_spec = pl.BlockSpec((1, NSA_KV_HEADS, ntile, HEAD_DIM), lambda b, i, pt: (b, 0, i, 0))
    grid_spec = pltpu.PrefetchScalarGridSpec(
        num_scalar_prefetch=1,
        grid=(nb, n_tiles),
        in_specs=[page_spec(r) for r in range(n_pg)] + [
            const(2, CMP_STRIDE * HEAD_DIM, 2 * CMP_HIDDEN),
            const(2, 2, 1, CMP_STRIDE * HEAD_DIM),
            const(2, CMP_HIDDEN, HEAD_DIM),
            const(1, HEAD_DIM)],
        out_specs=[out_spec, out_spec],
        scratch_shapes=[pltpu.VMEM((2, 8, CMP_HIDDEN), F32),
                        pltpu.VMEM((2, 8, CMP_HIDDEN), F32),
                        pltpu.VMEM((NSA_KV_HEADS * ntile, HEAD_DIM), F32)],
    )
    shape = jax.ShapeDtypeStruct((nb, NSA_KV_HEADS, n_tiles * ntile, HEAD_DIM), BF16)
    return pl.pallas_call(
        functools.partial(_compress_kernel, n_pg=n_pg),
        grid_spec=grid_spec,
        out_shape=[shape, shape],
        compiler_params=_cparams(("arbitrary", "arbitrary")),
        name="compress",
    )(page_table, *([src] * n_pg), wc, pe2, w2, kc_g.reshape(1, HEAD_DIM))


def _split3(x):
    hi = x.astype(BF16)
    r = x - hi.astype(F32)
    mid = r.astype(BF16)
    lo = (r - mid.astype(F32)).astype(BF16)
    return hi, mid, lo


def _cmp_branch(qg, kc, vc, tpos4, n_sub):
    s = _dot_nt(qg, kc)
    npr = lax.broadcasted_iota(jnp.int32, (1, n_sub), 1)
    valid = (npr >= 1) & (npr * CMP_STRIDE + (CMP_STRIDE - 1) <= tpos4)
    s = jnp.where(valid, s, NEG)
    m = jnp.max(s, axis=-1, keepdims=True)
    p = jnp.where(valid, jnp.exp2(s - m), 0.0)
    l = jnp.sum(p, axis=-1, keepdims=True)
    p = p * jnp.where(l > 0.0, 1.0 / l, 0.0)
    return _dot(p, vc), p


def _softmax_cols(s, mask):
    s = jnp.where(mask, s, NEG)
    m = jnp.maximum(jnp.max(s, axis=0, keepdims=True), 0.1 * NEG)
    p = jnp.exp2(s - m)
    l = jnp.sum(p, axis=0, keepdims=True)
    return p, jnp.where(l > 0.0, 1.0 / l, 0.0)


def _select_t(imp, tpos, nsp):
    sb = lax.broadcasted_iota(jnp.int32, (nsp, 1), 0)
    cur = tpos // SEL_BLOCK
    causal = sb * SEL_BLOCK <= tpos
    forced = (sb == 0) | (sb == cur) | (sb == cur - 1)
    score = jnp.where(causal, jnp.where(forced, -NEG, imp), NEG)
    sel = jnp.zeros(score.shape, jnp.bool_)
    sbf = sb.astype(F32)
    for _ in range(SEL_TOP):
        m = jnp.max(score, axis=0, keepdims=True)
        idx = jnp.min(jnp.where(score == m, sbf, float(nsp)), axis=0, keepdims=True)
        hit = sbf == idx
        sel = sel | hit
        score = jnp.where(hit, NEG, score)
    return sel & causal


def _importance(psum, n_sub, nsp):
    npr = lax.broadcasted_iota(jnp.int32, (n_sub, nsp), 0)
    sb = lax.broadcasted_iota(jnp.int32, (n_sub, nsp), 1)
    r = SEL_BLOCK // CMP_STRIDE
    ov = ((npr >= r * sb) & (npr <= r * sb + r) & (npr >= 1)).astype(BF16)
    hi, mid, lo = _split3(psum)
    dot = lambda a: jnp.dot(a, ov, preferred_element_type=F32)
    return dot(hi) + dot(mid) + dot(lo)


def _select(imp, tpos, nsp):
    sb = lax.broadcasted_iota(jnp.int32, (1, nsp), 1)
    cur = tpos // SEL_BLOCK
    causal = sb * SEL_BLOCK <= tpos
    forced = (sb == 0) | (sb == cur) | (sb == cur - 1)
    score = jnp.where(causal, jnp.where(forced, -NEG, imp), NEG)
    sel = jnp.zeros(score.shape, jnp.bool_)
    sbf = sb.astype(F32)
    for _ in range(SEL_TOP):
        m = jnp.max(score, axis=-1, keepdims=True)
        idx = jnp.min(jnp.where(score == m, sbf, float(nsp)), axis=-1, keepdims=True)
        hit = sbf == idx
        sel = sel | hit
        score = jnp.where(hit, NEG, score)
    return sel & causal


def _online_step(carry, s, vt):
    m, l, acc = carry
    m_new = jnp.maximum(m, jnp.max(s, axis=-1, keepdims=True))
    alpha = jnp.exp2(m - m_new)
    p = jnp.exp2(s - m_new)
    l = alpha * l + jnp.sum(p, axis=-1, keepdims=True)
    acc = alpha * acc + _dot(p, vt)
    return m_new, l, acc


def _block_onehot(key0, n_keys, nsp):
    shift = SEL_BLOCK.bit_length() - 1
    kb = jnp.right_shift(lax.broadcasted_iota(jnp.int32, (n_keys, 1), 0), shift)
    sb = lax.broadcasted_iota(jnp.int32, (1, nsp), 1) - jnp.right_shift(key0, shift)
    return (kb == sb).astype(BF16)


def _window_branch(qg, kw, vw, kpos, tpos4):
    s = _dot_nt(qg, kw)
    mask = (kpos <= tpos4) & (kpos >= tpos4 - WINDOW) & (kpos >= 0)
    s = jnp.where(mask, s, NEG)
    m = jnp.max(s, axis=-1, keepdims=True)
    p = jnp.where(mask, jnp.exp2(s - m), 0.0)
    l = jnp.sum(p, axis=-1, keepdims=True)
    return _dot(p, vw) * jnp.where(l > 0.0, 1.0 / l, 0.0)


def _gate_col(gates, g, br, tq):
    return jnp.concatenate(
        [gates[:, 3 * (NSA_GROUP * g + j) + br:3 * (NSA_GROUP * g + j) + br + 1] for j in range(NSA_GROUP)], axis=0)


def _finish(o_ref, oacc_ref, nz, og):
    o = oacc_ref[...]
    o = o * lax.rsqrt(jnp.mean(o * o, axis=-1, keepdims=True) + NORM_EPS) * og
    o_ref[...] = (o * _silu(nz)).astype(o_ref.dtype)


def _nsa_prompt_kernel(q_ref, gate_ref, nz_ref, kc_ref, vc_ref, selk_ref, selvt_ref, *rest, n_sub, nsp):
    nwin = WINDOW // Q_BLOCK + 1
    wink_refs = rest[:nwin]
    winvt_refs = rest[nwin:2 * nwin]
    og_ref, o_ref, oacc_ref = rest[2 * nwin:]
    i = pl.program_id(0)
    tq = Q_BLOCK
    lanes = NSA_GROUP * tq
    t0 = i * tq
    tpos = t0 + lax.broadcasted_iota(jnp.int32, (1, tq), 1)
    tpos4 = jnp.concatenate([tpos] * NSA_GROUP, axis=1)
    q = q_ref[...]
    gates_t = gate_ref[...].T

    def gate_row(g, br):
        rows = [3 * (NSA_GROUP * g + j) + br for j in range(NSA_GROUP)]
        return jnp.concatenate([gates_t[r:r + 1, :] for r in rows], axis=1)

    npr = lax.broadcasted_iota(jnp.int32, (n_sub, 1), 0)
    cvalid = (npr >= 1) & (npr * CMP_STRIDE + (CMP_STRIDE - 1) <= tpos4)
    sb_o = lax.broadcasted_iota(jnp.int32, (nsp, n_sub), 0)
    np_o = lax.broadcasted_iota(jnp.int32, (nsp, n_sub), 1)
    r = SEL_BLOCK // CMP_STRIDE
    ovt = ((np_o >= r * sb_o) & (np_o <= r * sb_o + r) & (np_o >= 1)).astype(BF16)

    groups = range(NSA_KV_HEADS)
    qts = [jnp.concatenate([q[(NSA_GROUP * g + j) * HEAD_DIM:(NSA_GROUP * g + j + 1) * HEAD_DIM, :]
                            for j in range(NSA_GROUP)], axis=1) for g in groups]

    cs = [jnp.dot(kc_ref[0, g], qts[g], preferred_element_type=F32) for g in groups]
    cps = [p * inv for p, inv in (_softmax_cols(s, cvalid) for s in cs)]
    octs = [_dot_tn(vc_ref[0, g], cps[g]) for g in groups]
    imps = []
    for g in groups:
        psum = cps[g][:, 0:tq]
        for j in range(1, NSA_GROUP):
            psum = psum + cps[g][:, j * tq:(j + 1) * tq]
        imps.append(sum(jnp.dot(ovt, part, preferred_element_type=F32) for part in _split3(psum)))
    sel = _select_t(jnp.concatenate(imps, axis=1), tpos4, nsp)
    bias_t = jnp.where(sel, 0.0, NEG).astype(BF16)

    kpos = t0 - WINDOW + lax.broadcasted_iota(jnp.int32, (WINDOW + tq, 1), 0)
    wmask = (kpos <= tpos4) & (kpos >= tpos4 - WINDOW) & (kpos >= 0)
    ws = [jnp.dot(jnp.concatenate([w[:, g * HEAD_DIM:(g + 1) * HEAD_DIM] for w in wink_refs], axis=0), qts[g],
                  preferred_element_type=F32) for g in groups]
    wsb = [jnp.where(wmask, s, NEG).astype(BF16) for s in ws]
    wps = [jnp.exp2(s - jnp.max(s, axis=0, keepdims=True)) for s in wsb]
    wones = jnp.ones((16, WINDOW + tq), BF16)
    owts = []
    for g in groups:
        vwt = jnp.concatenate([w[g * HEAD_DIM:(g + 1) * HEAD_DIM, :] for w in winvt_refs], axis=1)
        pv = jnp.dot(jnp.concatenate([vwt, wones], axis=0), wps[g], preferred_element_type=F32)
        owts.append(pv[:HEAD_DIM] * (1.0 / pv[HEAD_DIM:HEAD_DIM + 1]))
    n_full = t0 // KV_TILE
    krow = lax.broadcasted_iota(jnp.int32, (KV_TILE, 1), 0)
    qaugs = []
    for g in range(NSA_KV_HEADS):
        bg = bias_t[:, g * tq:(g + 1) * tq]
        qaugs.append(jnp.concatenate([qts[g], jnp.concatenate([bg] * NSA_GROUP, axis=1)], axis=0))

    def tile_step(c, carries, diagonal):
        k0 = pl.multiple_of(c * KV_TILE, KV_TILE)
        onehot = _block_onehot(k0, KV_TILE, nsp)
        scores = []
        for g in range(NSA_KV_HEADS):
            kt = selk_ref[pl.ds(k0, KV_TILE), pl.ds(g * HEAD_DIM, HEAD_DIM)]
            s = jnp.dot(jnp.concatenate([kt, onehot], axis=1), qaugs[g], preferred_element_type=F32)
            if diagonal:
                s = jnp.where(k0 + krow <= tpos4, s, NEG)
            scores.append(s.astype(BF16))
        stats = []
        for g in range(NSA_KV_HEADS):
            m = carries[g][0]
            m_new = jnp.maximum(m, jnp.max(scores[g], axis=0, keepdims=True).astype(F32))
            stats.append((m_new, jnp.exp2(m - m_new), jnp.exp2(scores[g] - m_new.astype(BF16))))
        out = []
        ones = jnp.ones((16, KV_TILE), BF16)
        for g in range(NSA_KV_HEADS):
            m_new, alpha, p = stats[g]
            vtt = selvt_ref[pl.ds(g * HEAD_DIM, HEAD_DIM), pl.ds(k0, KV_TILE)]
            pv = jnp.dot(jnp.concatenate([vtt, ones], axis=0), p, preferred_element_type=F32)
            out.append((m_new, alpha * carries[g][1] + pv[HEAD_DIM:HEAD_DIM + 1], alpha * carries[g][2] + pv[:HEAD_DIM]))
        return tuple(out)

    init = tuple((jnp.full((1, lanes), NEG, F32), jnp.zeros((1, lanes), F32), jnp.zeros((HEAD_DIM, lanes), F32))
                 for _ in range(NSA_KV_HEADS))
    carries = lax.fori_loop(0, n_full, lambda c, cr: tile_step(c, cr, False), init)
    carries = tile_step(n_full, carries, True)

    for g in groups:
        m, l, acc = carries[g]
        o_s = acc * (1.0 / l)
        o_g = gate_row(g, 0) * octs[g] + gate_row(g, 1) * o_s + gate_row(g, 2) * owts[g]
        for j in range(NSA_GROUP):
            hd = NSA_GROUP * g + j
            oacc_ref[:, hd * HEAD_DIM:(hd + 1) * HEAD_DIM] = o_g[:, j * tq:(j + 1) * tq].T

    _finish(o_ref, oacc_ref, nz_ref[...], og_ref[...])


def _nsa_prompt(qn, gates, proj, kc, vc, selk, selvt, wink, winvt, nsa_g, t_len):
    n_sub = kc.shape[2]
    n_sel = t_len // SEL_BLOCK
    nsp = -(-n_sel // 128) * 128
    nq = t_len // Q_BLOCK
    nwin = WINDOW // Q_BLOCK + 1
    wblk = lambda r: (lambda i: jnp.maximum(i - (nwin - 1) + r, 0))
    win_specs = ([pl.BlockSpec((Q_BLOCK, KV_WIDTH), (lambda f: (lambda i: (f(i), 0)))(wblk(r))) for r in range(nwin)]
                 + [pl.BlockSpec((KV_WIDTH, Q_BLOCK), (lambda f: (lambda i: (0, f(i))))(wblk(r))) for r in range(nwin)])
    return pl.pallas_call(
        functools.partial(_nsa_prompt_kernel, n_sub=n_sub, nsp=nsp),
        grid=(nq,),
        in_specs=[pl.BlockSpec((NSA_WIDTH, Q_BLOCK), lambda i: (0, i)),
                  pl.BlockSpec((Q_BLOCK, 128), lambda i: (i, 0)),
                  pl.BlockSpec((Q_BLOCK, NSA_WIDTH), lambda i: (i, COL_NZ // 2048)),
                  pl.BlockSpec((1, NSA_KV_HEADS, n_sub, HEAD_DIM), lambda i: (0, 0, 0, 0)),
                  pl.BlockSpec((1, NSA_KV_HEADS, n_sub, HEAD_DIM), lambda i: (0, 0, 0, 0)),
                  pl.BlockSpec((t_len, KV_WIDTH), lambda i: (0, 0), pipeline_mode=pl.Buffered(1)),
                  pl.BlockSpec((KV_WIDTH, t_len), lambda i: (0, 0), pipeline_mode=pl.Buffered(1))]
        + win_specs + [pl.BlockSpec((1, NSA_WIDTH), lambda i: (0, 0))],
        out_specs=pl.BlockSpec((Q_BLOCK, NSA_WIDTH), lambda i: (i, 0)),
        out_shape=jax.ShapeDtypeStruct((t_len, NSA_WIDTH), BF16),
        scratch_shapes=[pltpu.VMEM((Q_BLOCK, NSA_WIDTH), F32)],
        compiler_params=_cparams(("arbitrary",)),
        name="nsa_prompt",
    )(qn, gates, proj, kc, vc, selk, selvt, *([wink] * nwin), *([winvt] * nwin), nsa_g.reshape(1, NSA_WIDTH))


def _sample_select_kernel(q_ref, kc_ref, vc_ref, bias_ref, oc_ref, *, n_sub, nsp, past_len, t_new, bs):
    tq = t_new
    G = NSA_KV_HEADS
    tpos = past_len + lax.broadcasted_iota(jnp.int32, (tq, 1), 0)
    tpos4 = jnp.concatenate([tpos] * NSA_GROUP, axis=0)
    q = q_ref[...]
    psums = []
    for b in range(bs):
        for g in range(G):
            qg = jnp.concatenate([q[b * tq:(b + 1) * tq, (NSA_GROUP * g + j) * HEAD_DIM:(NSA_GROUP * g + j + 1) * HEAD_DIM]
                                  for j in range(NSA_GROUP)], axis=0)
            oc, p = _cmp_branch(qg.astype(BF16), kc_ref[b, g], vc_ref[b, g], tpos4, n_sub)
            oc_ref[b, g] = oc
            psum = p[0:tq]
            for j in range(1, NSA_GROUP):
                psum = psum + p[j * tq:(j + 1) * tq]
            psums.append(psum)
    imp = _importance(jnp.concatenate(psums, axis=0), n_sub, nsp)
    sel = _select(imp, jnp.concatenate([tpos] * (bs * G), axis=0), nsp)
    bias = jnp.where(sel, 0.0, NEG)
    lanes = G * NSA_GROUP * tq
    for b in range(bs):
        bb = jnp.concatenate([bias[(b * G + g) * tq:(b * G + g + 1) * tq]
                              for g in range(G) for _ in range(NSA_GROUP)], axis=0)
        for c in range(nsp // lanes):
            bias_ref[b, c * lanes:(c + 1) * lanes, :] = bb[:, c * lanes:(c + 1) * lanes].T


def _sample_select(qn, kc, vc, past_len, t_new):
    nb = kc.shape[0]
    n_sub = kc.shape[2]
    n_sel = past_len // SEL_BLOCK + 1
    nsp = -(-n_sel // 128) * 128
    bs = 8 if nb % 8 == 0 else nb
    rows = NSA_GROUP * t_new
    lanes = NSA_KV_HEADS * rows
    assert lanes == 128 and nsp % lanes == 0
    kv_spec = pl.BlockSpec((bs, NSA_KV_HEADS, n_sub, HEAD_DIM), lambda i: (i, 0, 0, 0))
    return pl.pallas_call(
        functools.partial(_sample_select_kernel, n_sub=n_sub, nsp=nsp, past_len=past_len, t_new=t_new, bs=bs),
        grid=(nb // bs,),
        in_specs=[pl.BlockSpec((bs * t_new, NSA_WIDTH), lambda i: (i, 0)), kv_spec, kv_spec],
        out_specs=[pl.BlockSpec((bs, nsp, lanes), lambda i: (i, 0, 0)),
                   pl.BlockSpec((bs, NSA_KV_HEADS, rows, HEAD_DIM), lambda i: (i, 0, 0, 0))],
        out_shape=[jax.ShapeDtypeStruct((nb, nsp, lanes), F32),
                   jax.ShapeDtypeStruct((nb, NSA_KV_HEADS, rows, HEAD_DIM), F32)],
        compiler_params=_cparams(("arbitrary",)),
        name="sample_select",
    )(qn, kc, vc)


def _nsa_sample_kernel(pt_ref, *refs, nsp, past_len, t_new):
    pages = refs[:PAGES_PER_TILE]
    (q_ref, gate_ref, nz_ref, bias_ref, oc_ref, newsel_ref, swin_ref, newwin_ref, og_ref,
     o_ref, qbd_ref, m_ref, l_ref, acc_ref, oacc_ref) = refs[PAGES_PER_TILE:]
    tau = pl.program_id(1)
    n_tau = pl.num_programs(1)
    tq = t_new
    G = NSA_KV_HEADS
    rows = NSA_GROUP * tq
    lanes = G * rows
    tpos = past_len + lax.broadcasted_iota(jnp.int32, (tq, 1), 0)
    tpos4 = jnp.concatenate([tpos] * NSA_GROUP, axis=0)
    lane_i = lax.broadcasted_iota(jnp.int32, (1, lanes), 1)

    def group_q():
        q = q_ref[...]
        return [jnp.concatenate([q[:, (NSA_GROUP * g + j) * HEAD_DIM:(NSA_GROUP * g + j + 1) * HEAD_DIM]
                                 for j in range(NSA_GROUP)], axis=0) for g in range(G)]

    def to_col(row_vec):
        eye = lax.broadcasted_iota(jnp.int32, (lanes, lanes), 0) == lax.broadcasted_iota(jnp.int32, (lanes, lanes), 1)
        return jnp.sum(jnp.where(eye, jnp.broadcast_to(row_vec, (lanes, lanes)), 0.0), axis=1, keepdims=True)

    def online(s, v_all, m, l, acc):
        m_new = jnp.maximum(m, jnp.max(s, axis=0, keepdims=True))
        alpha = jnp.exp2(m - m_new)
        p = jnp.exp2(s - m_new)
        l_new = alpha * l + jnp.sum(p, axis=0, keepdims=True)
        pv = _dot_tn(p, v_all)
        upd = jnp.concatenate([pv[g * rows:(g + 1) * rows, g * HEAD_DIM:(g + 1) * HEAD_DIM] for g in range(G)], axis=0)
        return m_new, l_new, to_col(alpha) * acc + upd

    @pl.when(tau == 0)
    def _():
        q_t = jnp.concatenate(group_q(), axis=0).T
        qbd_ref[...] = jnp.concatenate(
            [jnp.where(lane_i // rows == g, q_t, 0.0) for g in range(G)], axis=0).astype(BF16)
        m_ref[...] = jnp.full((1, lanes), NEG, F32)
        l_ref[...] = jnp.zeros((1, lanes), F32)
        acc_ref[...] = jnp.zeros((lanes, HEAD_DIM), F32)

    planes = [jnp.swapaxes(p[0], 0, 1) for p in pages]
    k_all = jnp.concatenate([jnp.concatenate([pl_[g] for g in range(G)], axis=1) for pl_ in planes], axis=0)
    v_all = jnp.concatenate([jnp.concatenate([pl_[G + g] for g in range(G)], axis=1) for pl_ in planes], axis=0)
    n_keys = PAGES_PER_TILE * PAGE
    n_blk = n_keys // SEL_BLOCK
    b_rows = bias_ref[0, pl.ds(pl.multiple_of(tau * n_blk, n_blk), n_blk), :]
    bias_keys = jnp.broadcast_to(b_rows[:, None, :], (n_blk, SEL_BLOCK, lanes)).reshape(n_keys, lanes)
    s = jnp.dot(k_all.astype(BF16), qbd_ref[...], preferred_element_type=F32) + bias_keys
    m, l, acc = online(s, v_all, m_ref[...], l_ref[...], acc_ref[...])
    m_ref[...] = m
    l_ref[...] = l
    acc_ref[...] = acc

    @pl.when(tau == n_tau - 1)
    def _():
        qgs = group_q()
        gates = gate_ref[...]
        n_buf = swin_ref.shape[1]
        kpos = past_len - n_buf + lax.broadcasted_iota(jnp.int32, (1, n_buf + tq), 1)
        last_blk = past_len // SEL_BLOCK
        pad = 16
        new = newsel_ref[...]
        zpad = jnp.zeros((pad - tq, G * HEAD_DIM), F32)
        kn = jnp.concatenate([jnp.concatenate([new[:, g, :] for g in range(G)], axis=1), zpad], axis=0)
        vn = jnp.concatenate([jnp.concatenate([new[:, G + g, :] for g in range(G)], axis=1), zpad], axis=0)
        krow = lax.broadcasted_iota(jnp.int32, (pad, 1), 0)
        s_new = jnp.dot(kn.astype(BF16), qbd_ref[...], preferred_element_type=F32) + bias_ref[0, last_blk:last_blk + 1, :]
        s_new = jnp.where((krow <= lane_i % tq) & (krow < tq), s_new, NEG)
        m, l, acc = online(s_new, vn, m_ref[...], l_ref[...], acc_ref[...])
        o_sel = acc * to_col(1.0 / l)
        swin = jnp.swapaxes(swin_ref[0], 0, 1)
        nwin = newwin_ref[...]
        for g in range(G):
            kw = jnp.concatenate([swin[g], nwin[:, g, :]], axis=0)
            vw = jnp.concatenate([swin[G + g], nwin[:, G + g, :]], axis=0)
            o_w = _window_branch(qgs[g], kw, vw, kpos, tpos4)
            o_g = (_gate_col(gates, g, 0, tq) * oc_ref[0, g] + _gate_col(gates, g, 1, tq) * o_sel[g * rows:(g + 1) * rows]
                   + _gate_col(gates, g, 2, tq) * o_w)
            for j in range(NSA_GROUP):
                hd = NSA_GROUP * g + j
                oacc_ref[:, hd * HEAD_DIM:(hd + 1) * HEAD_DIM] = o_g[j * tq:(j + 1) * tq]
        _finish(o_ref, oacc_ref, nz_ref[...], og_ref[...])


def _nsa_sample(cache, page_table, qn, gates, proj, bias_t, oc, rows_s, state_win, win_s, nsa_g, row0, t_new):
    nb, n_pages = page_table.shape
    past_len = n_pages * PAGE
    nsp = bias_t.shape[1]
    n_tau = n_pages // PAGES_PER_TILE
    rb = row0 // t_new
    n_buf = state_win.shape[1]
    rows = NSA_GROUP * t_new
    lanes = NSA_KV_HEADS * rows
    assert lanes == 128 and nsp % lanes == 0, "the sample kernel packs all query rows of a batch into one lane tile"
    cg = 2 * NSA_KV_HEADS

    def page_spec(r):
        return pl.BlockSpec((1, PAGE, cg, HEAD_DIM),
                            lambda b, i, pt: (pt[b, i * PAGES_PER_TILE + r], 0, 1, 0))

    grid_spec = pltpu.PrefetchScalarGridSpec(
        num_scalar_prefetch=1,
        grid=(nb, n_tau),
        in_specs=[page_spec(r) for r in range(PAGES_PER_TILE)] + [
            pl.BlockSpec((t_new, NSA_WIDTH), lambda b, i, pt: (b, 0)),
            pl.BlockSpec((t_new, 128), lambda b, i, pt: (b, 0)),
            pl.BlockSpec((t_new, NSA_WIDTH), lambda b, i, pt: (rb + b, COL_NZ // 2048)),
            pl.BlockSpec((1, nsp, lanes), lambda b, i, pt: (b, 0, 0)),
            pl.BlockSpec((1, NSA_KV_HEADS, rows, HEAD_DIM), lambda b, i, pt: (b, 0, 0, 0)),
            pl.BlockSpec((t_new, cg, HEAD_DIM), lambda b, i, pt: (b, 1, 0)),
            pl.BlockSpec((1, n_buf, cg, HEAD_DIM), lambda b, i, pt: (b, 0, 0, 0)),
            pl.BlockSpec((t_new, cg, HEAD_DIM), lambda b, i, pt: (b, 0, 0)),
            pl.BlockSpec((1, NSA_WIDTH), lambda b, i, pt: (0, 0))],
        out_specs=pl.BlockSpec((t_new, NSA_WIDTH), lambda b, i, pt: (b, 0)),
        scratch_shapes=[pltpu.VMEM((NSA_KV_HEADS * HEAD_DIM, lanes), BF16),
                        pltpu.VMEM((1, lanes), F32),
                        pltpu.VMEM((1, lanes), F32),
                        pltpu.VMEM((lanes, HEAD_DIM), F32),
                        pltpu.VMEM((t_new, NSA_WIDTH), F32)],
    )
    return pl.pallas_call(
        functools.partial(_nsa_sample_kernel, nsp=nsp, past_len=past_len, t_new=t_new),
        grid_spec=grid_spec,
        out_shape=jax.ShapeDtypeStruct((nb * t_new, NSA_WIDTH), F32),
        compiler_params=_cparams(("arbitrary", "arbitrary")),
        name="nsa_sample",
    )(page_table, *([cache] * PAGES_PER_TILE), qn, gates, proj, bias_t, oc, rows_s, state_win, win_s,
      nsa_g.reshape(1, NSA_WIDTH))


def _out_kernel(oh_ref, on_ref, wh_ref, wn_ref, x_ref, gate_ref, y_ref):
    acc = _dot(oh_ref[...], wh_ref[...]) + _dot(on_ref[...], wn_ref[...])
    y_ref[...] = x_ref[...] + gate_ref[...] * acc


def _out_proj(o_h, o_n, w_out, x, gate, tm, tn=512):
    m, d = x.shape
    half = o_h.shape[1]
    grow = gate.shape[0]
    gmap = (lambda i, j: (0, j)) if grow == 1 else (lambda i, j: (i, j))
    return pl.pallas_call(
        _out_kernel,
        grid=(m // tm, d // tn),
        in_specs=[pl.BlockSpec((tm, half), lambda i, j: (i, 0)),
                  pl.BlockSpec((tm, half), lambda i, j: (i, 0)),
                  pl.BlockSpec((half, tn), lambda i, j: (0, j)),
                  pl.BlockSpec((half, tn), lambda i, j: (1, j)),
                  pl.BlockSpec((tm, tn), lambda i, j: (i, j)),
                  pl.BlockSpec((1 if grow == 1 else tm, tn), gmap)],
        out_specs=pl.BlockSpec((tm, tn), lambda i, j: (i, j)),
        out_shape=jax.ShapeDtypeStruct((m, d), F32),
        compiler_params=_cparams(("arbitrary", "arbitrary")),
        name="out_proj",
    )(o_h, o_n, w_out, w_out, x, gate)


def _layer(xp, xs, c_all, cache, s_win, s_hgrn, page_table, lb_param, layer, norm_g, w_ada, b_ada, w_in,
           hgrn_out_g, q_g, k_g, pe, w1, w2, nsa_out_g, w_out):
    t_len = xp.shape[0]
    nb, t_new, _ = xs.shape
    n_s = nb * t_new
    d = D_MODEL

    mod = _ada(c_all, w_ada, b_ada)
    shift, scale, gate = mod[:, :d], mod[:, d:2 * d], mod[:, 2 * d:]
    rep = lambda a: jnp.pad(jnp.repeat(a[1:1 + nb], t_new, axis=0), ((0, SAMPLE_ROWS - n_s), (0, 0)))
    xs_pad = jnp.pad(xs.reshape(n_s, d), ((0, SAMPLE_ROWS - n_s), (0, 0)))
    h_all = _norm(xp, xs_pad, norm_g.reshape(1, d), scale[0:1], shift[0:1], rep(scale), rep(shift))

    proj = _matmul(h_all, jnp.swapaxes(w_in, 0, 1))

    oh_p, st_p = _hgrn(proj, lb_param, hgrn_out_g, None, row0=0, n_batch=1, t_len=t_len,
                       chunk=min(64, t_len), out_dtype=BF16, layer=layer)
    oh_s, st_s = _hgrn(proj, lb_param, hgrn_out_g, s_hgrn, row0=t_len, n_batch=nb, t_len=t_new,
                       chunk=t_new, out_dtype=F32, layer=layer)

    qn_p, rows_p, win_p3, selk_p, selvt_p, wink_p, winvt_p, gates_p = _prep(proj, q_g, k_g, 0, t_len, True)
    qn_s, rows_s, win_s3, _, _, _, _, gates_s = _prep(proj, q_g, k_g, t_len, SAMPLE_ROWS, False)

    w1r = w1.reshape(2, 2, CMP_STRIDE * HEAD_DIM, CMP_HIDDEN)
    wc = jnp.concatenate([w1r[:, 0], w1r[:, 1]], axis=-1).astype(BF16)
    pe2 = pe.reshape(2, 2, 1, CMP_STRIDE * HEAD_DIM)
    w2b = w2.astype(BF16)

    n_pp = t_len // PAGE
    kc_p, vc_p = _compress(rows_p.reshape(n_pp, PAGE, 4 * NSA_KV_HEADS, HEAD_DIM),
                           jnp.arange(n_pp, dtype=jnp.int32).reshape(1, n_pp), wc, pe2, w2b, k_g[0])
    cache4 = cache.reshape(cache.shape[0], PAGE, 4 * NSA_KV_HEADS, HEAD_DIM)
    kc_s, vc_s = _compress(cache4, page_table, wc, pe2, w2b, k_g[0])

    on_p = _nsa_prompt(qn_p, gates_p, proj, kc_p, vc_p, selk_p, selvt_p, wink_p, winvt_p, nsa_out_g, t_len)
    n_buf = s_win.shape[1]
    s_win4 = s_win.reshape(nb, n_buf, 2 * NSA_KV_HEADS, HEAD_DIM)
    bias_s, oc_s = _sample_select(qn_s, kc_s, vc_s, page_table.shape[1] * PAGE, t_new)
    on_s = _nsa_sample(cache4, page_table, qn_s, gates_s, proj, bias_s, oc_s, rows_s, s_win4, win_s3,
                       nsa_out_g, t_len, t_new)

    y_p = _out_proj(oh_p, on_p, w_out, xp, gate[0:1], tm=min(1024, t_len))
    y_s = _out_proj(oh_s, on_s, w_out, xs.reshape(n_s, d), jnp.repeat(gate[1:1 + nb], t_new, axis=0), tm=n_s)

    n_w = min(WINDOW, t_len)
    kv_p = rows_p.reshape(1, t_len, 4, NSA_KV_HEADS, HEAD_DIM)
    kv_s = rows_s[:n_s].reshape(nb, t_new, 4, NSA_KV_HEADS, HEAD_DIM)
    win_p = win_p3[t_len - n_w:].reshape(1, n_w, 2, NSA_KV_HEADS, HEAD_DIM)
    win_new = win_s3[:n_s].reshape(nb, t_new, 2, NSA_KV_HEADS, HEAD_DIM)
    win_s = jnp.concatenate([s_win, win_new], axis=1)[:, -n_buf:]
    return y_p, y_s.reshape(nb, t_new, d), kv_p, kv_s, win_p, win_s, st_p, st_s


def kernel(x_prompt, x_sample, cache_kv, state_win, state_hgrn, page_table, c_prompt, c_sample, norm_g, w_ada,
           b_ada, w_in, hgrn_lb, hgrn_out_g, q_norm_g, k_norm_g, cmp_pe, cmp_w1, cmp_w2, nsa_out_g, w_out):
    depth = w_in.shape[0]
    assert depth == 1 and x_prompt.shape[0] == 1
    nb = x_sample.shape[0]
    c_all = jnp.concatenate([c_prompt, c_sample], axis=0)
    c_all = jnp.pad(c_all, ((0, (-c_all.shape[0]) % 8), (0, 0)))
    l = 0
    outs = _layer(x_prompt[0], x_sample, c_all, cache_kv[l], state_win[l], state_hgrn[l], page_table,
                  hgrn_lb, l, norm_g[l], w_ada[l], b_ada[l], w_in[l], hgrn_out_g[l], q_norm_g[l], k_norm_g[l],
                  cmp_pe[l], cmp_w1[l], cmp_w2[l], nsa_out_g[l], w_out[l])
    y_p, y_s, kv_p, kv_s, win_p, win_s, st_p, st_s = outs
    return (y_p[None], y_s, kv_p[None], kv_s[None], win_p[None], win_s[None], st_p[None], st_s[None])
```

```python
import functools

import jax
import jax.numpy as jnp
from jax import lax
from jax.experimental import pallas as pl
from jax.experimental.pallas import tpu as pltpu

F32 = jnp.float32
BF16 = jnp.bfloat16

D_MODEL = 4096
HEAD_DIM = 128
HGRN_WIDTH = 2048
NSA_WIDTH = 2048
HGRN_HEADS = 16
NSA_HEADS = 16
NSA_KV_HEADS = 4
NSA_GROUP = 4
KV_WIDTH = 512
CMP_LEN = 32
CMP_STRIDE = 16
CMP_HIDDEN = 256
SEL_BLOCK = 64
SEL_TOP = 16
WINDOW = 512
Q_BLOCK = 128
NORM_EPS = 1e-6
PAGE = 128

COL_HQ, COL_HF, COL_HI, COL_HZ, COL_NQ, COL_ROWS, COL_NZ = (i * 2048 for i in range(7))
COL_WIN = 14336
COL_NG = 15360
N_PROJ = 15872
SAMPLE_ROWS = 256
ROW_TILE = 256
PAGES_PER_TILE = 32
CMP_PAGES = 32
KV_TILE = 512
NEG = -1e30
LOG2E = 1.4426950408889634
VMEM_LIMIT = 56 * 1024 * 1024


def _cparams(sem):
    return pltpu.CompilerParams(dimension_semantics=sem, vmem_limit_bytes=VMEM_LIMIT)


def _dot(a, b):
    return jnp.dot(a.astype(BF16), b.astype(BF16), preferred_element_type=F32)


def _dot_nt(a, b):
    return lax.dot_general(a.astype(BF16), b.astype(BF16), (((1,), (1,)), ((), ())),
                           preferred_element_type=F32)


def _dot_tn(a, b):
    return lax.dot_general(a.astype(BF16), b.astype(BF16), (((0,), (0,)), ((), ())),
                           preferred_element_type=F32)


def _silu(x):
    return x * jax.nn.sigmoid(x)


def _ada_kernel(c_ref, w_ref, b_ref, o_ref):
    c = c_ref[...]
    o_ref[...] = _dot(_silu(c), w_ref[...]) + b_ref[...]


def _ada(c_all, w_ada, b_ada):
    m, d = c_all.shape
    n = w_ada.shape[1]
    tn = 512
    return pl.pallas_call(
        _ada_kernel,
        grid=(n // tn,),
        in_specs=[pl.BlockSpec((m, d), lambda j: (0, 0)),
                  pl.BlockSpec((d, tn), lambda j: (0, j)),
                  pl.BlockSpec((1, tn), lambda j: (0, j))],
        out_specs=pl.BlockSpec((m, tn), lambda j: (0, j)),
        out_shape=jax.ShapeDtypeStruct((m, n), F32),
        compiler_params=_cparams(("arbitrary",)),
        name="ada",
    )(c_all, w_ada, b_ada.reshape(1, n))


def _norm_kernel(xp_ref, xs_ref, g_ref, scp_ref, shp_ref, scs_ref, shs_ref, o_ref, *, n_prompt):
    i = pl.program_id(0)

    def body(x, scale, shift):
        r = x * lax.rsqrt(jnp.mean(x * x, axis=-1, keepdims=True) + NORM_EPS) * g_ref[...]
        o_ref[...] = (r * (1.0 + scale) + shift).astype(o_ref.dtype)

    @pl.when(i < n_prompt)
    def _():
        body(xp_ref[...], scp_ref[...], shp_ref[...])

    @pl.when(i >= n_prompt)
    def _():
        body(xs_ref[...], scs_ref[...], shs_ref[...])


def _norm(xp, xs, g, sc_p, sh_p, sc_s, sh_s):
    t, d = xp.shape
    n_prompt = t // ROW_TILE
    row = lambda i: (0, 0)
    return pl.pallas_call(
        functools.partial(_norm_kernel, n_prompt=n_prompt),
        grid=(n_prompt + 1,),
        in_specs=[pl.BlockSpec((ROW_TILE, d), lambda i: (jnp.minimum(i, n_prompt - 1), 0)),
                  pl.BlockSpec((SAMPLE_ROWS, d), row),
                  pl.BlockSpec((1, d), row), pl.BlockSpec((1, d), row), pl.BlockSpec((1, d), row),
                  pl.BlockSpec((SAMPLE_ROWS, d), row), pl.BlockSpec((SAMPLE_ROWS, d), row)],
        out_specs=pl.BlockSpec((ROW_TILE, d), lambda i: (i, 0)),
        out_shape=jax.ShapeDtypeStruct((t + SAMPLE_ROWS, d), BF16),
        compiler_params=_cparams(("arbitrary",)),
        name="norm",
    )(xp, xs, g, sc_p, sh_p, sc_s, sh_s)


PROJ_TN = 512
_N_MAIN_A, _N_NZ, _N_WIN = 24, 4, 2
_ROW_WIN = _N_MAIN_A * PROJ_TN
_ROW_NG = _ROW_WIN + _N_WIN * PROJ_TN
_ROW_NZ = _ROW_NG + 3 * NSA_HEADS


_ROW_UNIT = 16


def _proj_weight_row(j):
    u = PROJ_TN // _ROW_UNIT
    q = jnp.where(j < _N_MAIN_A, j * u,
                  jnp.where(j < _N_MAIN_A + _N_NZ, _ROW_NZ // _ROW_UNIT + (j - _N_MAIN_A) * u,
                            jnp.where(j < _N_MAIN_A + _N_NZ + _N_WIN,
                                      _ROW_WIN // _ROW_UNIT + (j - _N_MAIN_A - _N_NZ) * u, _ROW_NG // _ROW_UNIT)))
    return q * _ROW_UNIT


def _mm_kernel(a_ref, wt_ref, o_ref, wb_ref):
    @pl.when(pl.program_id(1) == 0)
    def _():
        wb_ref[...] = wt_ref[...].astype(BF16)

    o_ref[...] = lax.dot_general(a_ref[...], wb_ref[...], (((1,), (1,)), ((), ())), preferred_element_type=F32)


def _row_tile(m, cap=1100):
    best = 16
    for tm in range(16, cap + 1, 16):
        if m % tm == 0:
            best = tm
    return best


def _matmul(a, w_t):
    m, k = a.shape
    tn = PROJ_TN
    tm = _row_tile(m)
    return pl.pallas_call(
        _mm_kernel,
        grid=(N_PROJ // tn, m // tm),
        in_specs=[pl.BlockSpec((tm, k), lambda j, i: (i, 0)),
                  pl.BlockSpec((pl.Element(tn), pl.Element(k)), lambda j, i: (_proj_weight_row(j), 0))],
        out_specs=pl.BlockSpec((tm, tn), lambda j, i: (i, j)),
        out_shape=jax.ShapeDtypeStruct((m, N_PROJ), F32),
        scratch_shapes=[pltpu.VMEM((tn, k), BF16)],
        compiler_params=_cparams(("arbitrary", "arbitrary")),
        name="proj",
    )(a, w_t)


def _hgrn_kernel(*refs, chunk, levels, has_s0, layer, single_chunk):
    if has_s0:
        lbp_ref, og_ref, hq_ref, hf_ref, hi_ref, hz_ref, s0_ref, o_ref, sout_ref, st_ref = refs
    else:
        lbp_ref, og_ref, hq_ref, hf_ref, hi_ref, hz_ref, o_ref, sout_ref, st_ref = refs
        s0_ref = None
    C = chunk
    c = pl.program_id(1)
    n_chunks = pl.num_programs(1)

    if not single_chunk:
        @pl.when(c == 0)
        def _():
            for h in range(HGRN_HEADS):
                if has_s0:
                    st_ref[h] = s0_ref[0, h].T
                else:
                    st_ref[h] = jnp.zeros((HEAD_DIM, HEAD_DIM), F32)

    lbp = lbp_ref[...]
    e = jnp.exp(lbp - jnp.max(lbp, axis=0, keepdims=True))
    lb = jnp.sum(e[:layer + 1], axis=0, keepdims=True) / jnp.sum(e, axis=0, keepdims=True)

    f = lb + (1.0 - lb) * jax.nn.sigmoid(hf_ref[...])
    logf = jnp.log(f)
    k = 1.0 - f
    q = hq_ref[...]
    v = hi_ref[...]
    hz = hz_ref[...]
    n = q.shape[1]

    row = lax.broadcasted_iota(jnp.int32, (C, 1), 0)
    cum = logf
    s = 1
    while s < C:
        cum = cum + jnp.where(row >= s, pltpu.roll(cum, s, axis=0), 0.0)
        s *= 2
    cum = cum * LOG2E
    last = cum[C - 1:C, :]
    qe = q * jnp.exp2(cum)
    kd = k * jnp.exp2(last - cum)
    elast = jnp.exp2(last)

    lvl = []
    for h in levels:
        np_ = C // (2 * h)
        c4 = cum.reshape(np_, 2, h, n)
        cmid = c4[:, 0, h - 1:h, :]
        zero = jnp.zeros((np_, 1, h, n), F32)
        qr = q.reshape(np_, 2, h, n)[:, 1] * jnp.exp2(c4[:, 1] - cmid)
        kl_ = k.reshape(np_, 2, h, n)[:, 0] * jnp.exp2(cmid - c4[:, 0])
        ql = jnp.concatenate([zero, qr[:, None]], axis=1).reshape(C, n)
        kl = jnp.concatenate([kl_[:, None], zero], axis=1).reshape(C, n)
        ti = lax.broadcasted_iota(jnp.int32, (C, C), 0)
        si = lax.broadcasted_iota(jnp.int32, (C, C), 1)
        mask = ((ti // (2 * h)) == (si // (2 * h))) & (((ti // h) % 2) == 1) & (((si // h) % 2) == 0)
        lvl.append((ql, kl, mask))

    nb = C // 8
    rin = lax.broadcasted_iota(jnp.int32, (nb, 8, 1), 1)
    og = og_ref[...]
    for h in range(HGRN_HEADS):
        sl = slice(h * HEAD_DIM, (h + 1) * HEAD_DIM)
        q_h, k_h, v_h, cum_h = q[:, sl], k[:, sl], v[:, sl], cum[:, sl]
        if single_chunk:
            st = s0_ref[0, h]
            o_h = _dot(qe[:, sl], st)
        else:
            st = st_ref[h]
            o_h = _dot_nt(qe[:, sl], st)
        if levels:
            a = jnp.zeros((C, C), F32)
            for ql, kl, mask in lvl:
                a = a + jnp.where(mask, _dot_nt(ql[:, sl], kl[:, sl]), 0.0)
            o_h = o_h + _dot(a, v_h)
        q3 = q_h.reshape(nb, 8, HEAD_DIM)
        k3 = k_h.reshape(nb, 8, HEAD_DIM)
        v3 = v_h.reshape(nb, 8, HEAD_DIM)
        c3 = cum_h.reshape(nb, 8, HEAD_DIM)
        od = jnp.zeros((nb, 8, HEAD_DIM), F32)
        for s in range(8):
            dec = jnp.exp2(jnp.where(rin >= s, c3 - c3[:, s:s + 1, :], -jnp.inf))
            w = q3 * k3[:, s:s + 1, :] * dec
            od = od + jnp.sum(w, axis=-1, keepdims=True) * v3[:, s:s + 1, :]
        o_h = o_h + od.reshape(C, HEAD_DIM)
        if single_chunk:
            hi, mid, lo = (p.astype(F32) for p in _split3(elast[:, sl]))
            r16 = lax.broadcasted_iota(jnp.int32, (16, 1), 0)
            erows = jnp.where(r16 == 0, hi, jnp.where(r16 == 1, mid, jnp.where(r16 == 2, lo, 0.0))).astype(BF16)
            emat = lax.dot_general(erows, jnp.ones((16, HEAD_DIM), BF16), (((0,), (0,)), ((), ())),
                                   preferred_element_type=F32)
            sout_ref[0, h] = emat * st + _dot_tn(kd[:, sl], v_h)
        else:
            st_ref[h] = st * elast[:, sl] + _dot_tn(v_h, kd[:, sl])
        o_h = o_h * lax.rsqrt(jnp.mean(o_h * o_h, axis=-1, keepdims=True) + NORM_EPS) * og
        o_ref[:, sl] = (o_h * _silu(hz[:, sl])).astype(o_ref.dtype)

    if not single_chunk:
        @pl.when(c == n_chunks - 1)
        def _():
            for h in range(HGRN_HEADS):
                sout_ref[0, h] = st_ref[h].T


def _hgrn(proj, lb_param, out_g, s0, *, row0, n_batch, t_len, chunk, out_dtype, layer):
    n_chunks = t_len // chunk
    levels = []
    h = chunk // 2
    while h >= 8:
        levels.append(h)
        h //= 2
    blk0 = row0 // chunk
    rmap = lambda col: (lambda b, c: (blk0 + b * n_chunks + c, col))
    in_specs = [pl.BlockSpec(lb_param.shape, lambda b, c: (0, 0)),
                pl.BlockSpec((1, HEAD_DIM), lambda b, c: (0, 0)),
                pl.BlockSpec((chunk, HGRN_WIDTH), rmap(0)),
                pl.BlockSpec((chunk, HGRN_WIDTH), rmap(1)),
                pl.BlockSpec((chunk, HGRN_WIDTH), rmap(2)),
                pl.BlockSpec((chunk, HGRN_WIDTH), rmap(3))]
    args = [lb_param, out_g.reshape(1, HEAD_DIM), proj, proj, proj, proj]
    if s0 is not None:
        in_specs.append(pl.BlockSpec((1, HGRN_HEADS, HEAD_DIM, HEAD_DIM), lambda b, c: (b, 0, 0, 0)))
        args.append(s0)
    return pl.pallas_call(
        functools.partial(_hgrn_kernel, chunk=chunk, levels=tuple(levels), has_s0=s0 is not None, layer=layer,
                          single_chunk=s0 is not None and n_chunks == 1),
        grid=(n_batch, n_chunks),
        in_specs=in_specs,
        out_specs=[pl.BlockSpec((chunk, HGRN_WIDTH), lambda b, c: (b * n_chunks + c, 0)),
                   pl.BlockSpec((1, HGRN_HEADS, HEAD_DIM, HEAD_DIM), lambda b, c: (b, 0, 0, 0))],
        out_shape=[jax.ShapeDtypeStruct((n_batch * t_len, HGRN_WIDTH), out_dtype),
                   jax.ShapeDtypeStruct((n_batch, HGRN_HEADS, HEAD_DIM, HEAD_DIM), F32)],
        scratch_shapes=[pltpu.VMEM((HGRN_HEADS, HEAD_DIM, HEAD_DIM), F32)],
        compiler_params=_cparams(("arbitrary", "arbitrary")),
        name="hgrn",
    )(*args)


def _head_rms(x, g, n_heads):
    outs = []
    for h in range(n_heads):
        xh = x[:, h * HEAD_DIM:(h + 1) * HEAD_DIM]
        outs.append(xh * lax.rsqrt(jnp.mean(xh * xh, axis=-1, keepdims=True) + NORM_EPS) * g)
    return jnp.concatenate(outs, axis=1)


def _prep_kernel(nq_ref, rows_ref, win_ref, ng_ref, qg_ref, kg_ref,
                 q_ref, rows_o, win_o, selk_o, selvt_o, wink_o, winvt_o, gate_o, *, q_transposed):
    kg = kg_ref[...]
    qn = _head_rms(nq_ref[...], qg_ref[...], NSA_HEADS) * (HEAD_DIM ** -0.5 * LOG2E)
    if q_transposed:
        q_ref[...] = qn.T.astype(q_ref.dtype)
    else:
        q_ref[...] = qn
    rows = rows_ref[...]
    ks = _head_rms(rows[:, 2 * KV_WIDTH:3 * KV_WIDTH], kg[1:2], NSA_KV_HEADS)
    for cg in range(4 * NSA_KV_HEADS):
        sl = slice(cg * HEAD_DIM, (cg + 1) * HEAD_DIM)
        if 2 * NSA_KV_HEADS <= cg < 3 * NSA_KV_HEADS:
            rows_o[:, cg, :] = ks[:, (cg - 2 * NSA_KV_HEADS) * HEAD_DIM:(cg - 2 * NSA_KV_HEADS + 1) * HEAD_DIM]
        else:
            rows_o[:, cg, :] = rows[:, sl]
    selk_o[...] = ks.astype(BF16)
    selvt_o[...] = rows[:, 3 * KV_WIDTH:].T.astype(BF16)
    win = win_ref[...]
    kw = _head_rms(win[:, :KV_WIDTH], kg[2:3], NSA_KV_HEADS)
    for cg in range(2 * NSA_KV_HEADS):
        sl = slice(cg * HEAD_DIM, (cg + 1) * HEAD_DIM)
        win_o[:, cg, :] = kw[:, sl] if cg < NSA_KV_HEADS else win[:, sl]
    wink_o[...] = kw.astype(BF16)
    winvt_o[...] = win[:, KV_WIDTH:].T.astype(BF16)
    gate_o[...] = jax.nn.sigmoid(ng_ref[...])


def _prep(proj, q_g, k_g, row0, n_rows, q_transposed):
    t = min(ROW_TILE, n_rows)
    r0 = row0 // t
    q_spec = pl.BlockSpec((NSA_WIDTH, t), lambda i: (0, i)) if q_transposed else pl.BlockSpec((t, 2048), lambda i: (i, 0))
    q_shape = (jax.ShapeDtypeStruct((NSA_WIDTH, n_rows), BF16) if q_transposed
               else jax.ShapeDtypeStruct((n_rows, NSA_WIDTH), F32))
    return pl.pallas_call(
        functools.partial(_prep_kernel, q_transposed=q_transposed),
        grid=(n_rows // t,),
        in_specs=[pl.BlockSpec((t, NSA_WIDTH), lambda i: (r0 + i, COL_NQ // 2048)),
                  pl.BlockSpec((t, 2048), lambda i: (r0 + i, COL_ROWS // 2048)),
                  pl.BlockSpec((t, 1024), lambda i: (r0 + i, COL_WIN // 1024)),
                  pl.BlockSpec((t, 128), lambda i: (r0 + i, COL_NG // 128)),
                  pl.BlockSpec((1, HEAD_DIM), lambda i: (0, 0)),
                  pl.BlockSpec((3, HEAD_DIM), lambda i: (0, 0))],
        out_specs=[q_spec,
                   pl.BlockSpec((t, 4 * NSA_KV_HEADS, HEAD_DIM), lambda i: (i, 0, 0)),
                   pl.BlockSpec((t, 2 * NSA_KV_HEADS, HEAD_DIM), lambda i: (i, 0, 0)),
                   pl.BlockSpec((t, KV_WIDTH), lambda i: (i, 0)),
                   pl.BlockSpec((KV_WIDTH, t), lambda i: (0, i)),
                   pl.BlockSpec((t, KV_WIDTH), lambda i: (i, 0)),
                   pl.BlockSpec((KV_WIDTH, t), lambda i: (0, i)),
                   pl.BlockSpec((t, 128), lambda i: (i, 0))],
        out_shape=[q_shape,
                   jax.ShapeDtypeStruct((n_rows, 4 * NSA_KV_HEADS, HEAD_DIM), F32),
                   jax.ShapeDtypeStruct((n_rows, 2 * NSA_KV_HEADS, HEAD_DIM), F32),
                   jax.ShapeDtypeStruct((n_rows, KV_WIDTH), BF16),
                   jax.ShapeDtypeStruct((KV_WIDTH, n_rows), BF16),
                   jax.ShapeDtypeStruct((n_rows, KV_WIDTH), BF16),
                   jax.ShapeDtypeStruct((KV_WIDTH, n_rows), BF16),
                   jax.ShapeDtypeStruct((n_rows, 128), F32)],
        compiler_params=_cparams(("arbitrary",)),
        name="prep",
    )(proj, proj, proj, proj, q_g.reshape(1, HEAD_DIM), k_g)


def _compress_kernel(pt_ref, *refs, n_pg):
    pages = refs[:n_pg]
    wc_ref, pe_ref, w2_ref, kcg_ref, kc_ref, vc_ref, cvec_ref, carry_ref, out_scr = refs[n_pg:]
    b = pl.program_id(0)
    i = pl.program_id(1)
    nsub = PAGE // CMP_STRIDE
    ntile = nsub * n_pg
    G = NSA_KV_HEADS
    rows = ntile * G

    @pl.when((b == 0) & (i == 0))
    def _():
        for ch in range(2):
            lo = jnp.broadcast_to(pe_ref[ch, 0], (8, CMP_STRIDE * HEAD_DIM))
            hi = jnp.broadcast_to(pe_ref[ch, 1], (8, CMP_STRIDE * HEAD_DIM))
            cvec_ref[ch] = _dot(lo, wc_ref[ch][:, :CMP_HIDDEN]) + _dot(hi, wc_ref[ch][:, CMP_HIDDEN:])

    @pl.when(i == 0)
    def _():
        carry_ref[...] = jnp.zeros(carry_ref.shape, F32)

    def gathered(p):
        lo, hi = [], []
        for r in range(n_pg):
            a = pages[r][0, pl.ds(p, nsub // 2, stride=2 * CMP_STRIDE), :, :]
            bb = pages[r][0, pl.ds(p + CMP_STRIDE, nsub // 2, stride=2 * CMP_STRIDE), :, :]
            lo.append(jnp.concatenate([a[:, :G], bb[:, :G]], axis=1).reshape(nsub * G, HEAD_DIM))
            hi.append(jnp.concatenate([a[:, G:], bb[:, G:]], axis=1).reshape(nsub * G, HEAD_DIM))
        return jnp.concatenate(lo, axis=0).astype(BF16), jnp.concatenate(hi, axis=0).astype(BF16)

    xs = [gathered(p) for p in range(CMP_STRIDE)]
    u = [jnp.dot(jnp.concatenate([x[ch] for x in xs], axis=1), wc_ref[ch], preferred_element_type=F32)
         for ch in range(2)]

    row8 = lax.broadcasted_iota(jnp.int32, (8, 1), 0)
    for ch in range(2):
        u0, u1 = u[ch][:, :CMP_HIDDEN], u[ch][:, CMP_HIDDEN:]
        rolled = pltpu.roll(u0, G, axis=0)
        head = jnp.where(row8 < G, carry_ref[ch], rolled[0:8])
        carry_ref[ch] = rolled[0:8]
        prev = jnp.concatenate([head, rolled[8:]], axis=0)
        pre = prev + u1 + cvec_ref[ch][0:1, :]
        out = _dot(_silu(pre), w2_ref[ch])
        if ch == 0:
            out = out * lax.rsqrt(jnp.mean(out * out, axis=-1, keepdims=True) + NORM_EPS) * kcg_ref[...]
        out_scr[...] = out
        dst = kc_ref if ch == 0 else vc_ref
        for g in range(G):
            dst[0, g] = out_scr[pl.ds(g, ntile, stride=G), :].astype(dst.dtype)


def _compress(src, page_table, wc, pe2, w2, kc_g):
    nb, n_pages = page_table.shape
    n_pg = CMP_PAGES if n_pages % CMP_PAGES == 0 else PAGES_PER_TILE
    n_tiles = n_pages // n_pg
    nsub = PAGE // CMP_STRIDE
    ntile = nsub * n_pg

    def page_spec(r):
        return pl.BlockSpec((1, PAGE, 2 * NSA_KV_HEADS, HEAD_DIM),
                            lambda b, i, pt: (pt[b, i * n_pg + r], 0, 0, 0))

    const = lambda *shape: pl.BlockSpec(shape, lambda b, i, pt: (0,) * len(shape))
    out_spec = pl.BlockSpec((1, NSA_KV_HEADS, ntile, HEAD_DIM), lambda b, i, pt: (b, 0, i, 0))
    grid_spec = pltpu.PrefetchScalarGridSpec(
        num_scalar_prefetch=1,
        grid=(nb, n_tiles),
        in_specs=[page_spec(r) for r in range(n_pg)] + [
            const(2, CMP_STRIDE * HEAD_DIM, 2 * CMP_HIDDEN),
            const(2, 2, 1, CMP_STRIDE * HEAD_DIM),
            const(2, CMP_HIDDEN, HEAD_DIM),
            const(1, HEAD_DIM)],
        out_specs=[out_spec, out_spec],
        scratch_shapes=[pltpu.VMEM((2, 8, CMP_HIDDEN), F32),
                        pltpu.VMEM((2, 8, CMP_HIDDEN), F32),
                        pltpu.VMEM((NSA_KV_HEADS * ntile, HEAD_DIM), F32)],
    )
    shape = jax.ShapeDtypeStruct((nb, NSA_KV_HEADS, n_tiles * ntile, HEAD_DIM), BF16)
    return pl.pallas_call(
        functools.partial(_compress_kernel, n_pg=n_pg),
        grid_spec=grid_spec,
        out_shape=[shape, shape],
        compiler_params=_cparams(("arbitrary", "arbitrary")),
        name="compress",
    )(page_table, *([src] * n_pg), wc, pe2, w2, kc_g.reshape(1, HEAD_DIM))


def _split3(x):
    hi = x.astype(BF16)
    r = x - hi.astype(F32)
    mid = r.astype(BF16)
    lo = (r - mid.astype(F32)).astype(BF16)
    return hi, mid, lo


def _cmp_branch(qg, kc, vc, tpos4, n_sub):
    s = _dot_nt(qg, kc)
    npr = lax.broadcasted_iota(jnp.int32, (1, n_sub), 1)
    valid = (npr >= 1) & (npr * CMP_STRIDE + (CMP_STRIDE - 1) <= tpos4)
    s = jnp.where(valid, s, NEG)
    m = jnp.max(s, axis=-1, keepdims=True)
    p = jnp.where(valid, jnp.exp2(s - m), 0.0)
    l = jnp.sum(p, axis=-1, keepdims=True)
    p = p * jnp.where(l > 0.0, 1.0 / l, 0.0)
    return _dot(p, vc), p


def _softmax_cols(s, mask):
    s = jnp.where(mask, s, NEG)
    m = jnp.maximum(jnp.max(s, axis=0, keepdims=True), 0.1 * NEG)
    p = jnp.exp2(s - m)
    l = jnp.sum(p, axis=0, keepdims=True)
    return p, jnp.where(l > 0.0, 1.0 / l, 0.0)


def _select_t(imp, tpos, nsp):
    sb = lax.broadcasted_iota(jnp.int32, (nsp, 1), 0)
    cur = tpos // SEL_BLOCK
    causal = sb * SEL_BLOCK <= tpos
    forced = (sb == 0) | (sb == cur) | (sb == cur - 1)
    score = jnp.where(causal, jnp.where(forced, -NEG, imp), NEG)
    sel = jnp.zeros(score.shape, jnp.bool_)
    sbf = sb.astype(F32)
    for _ in range(SEL_TOP):
        m = jnp.max(score, axis=0, keepdims=True)
        idx = jnp.min(jnp.where(score == m, sbf, float(nsp)), axis=0, keepdims=True)
        hit = sbf == idx
        sel = sel | hit
        score = jnp.where(hit, NEG, score)
    return sel & causal


def _importance(psum, n_sub, nsp):
    npr = lax.broadcasted_iota(jnp.int32, (n_sub, nsp), 0)
    sb = lax.broadcasted_iota(jnp.int32, (n_sub, nsp), 1)
    r = SEL_BLOCK // CMP_STRIDE
    ov = ((npr >= r * sb) & (npr <= r * sb + r) & (npr >= 1)).astype(BF16)
    hi, mid, lo = _split3(psum)
    dot = lambda a: jnp.dot(a, ov, preferred_element_type=F32)
    return dot(hi) + dot(mid) + dot(lo)


def _select(imp, tpos, nsp):
    sb = lax.broadcasted_iota(jnp.int32, (1, nsp), 1)
    cur = tpos // SEL_BLOCK
    causal = sb * SEL_BLOCK <= tpos
    forced = (sb == 0) | (sb == cur) | (sb == cur - 1)
    score = jnp.where(causal, jnp.where(forced, -NEG, imp), NEG)
    sel = jnp.zeros(score.shape, jnp.bool_)
    sbf = sb.astype(F32)
    for _ in range(SEL_TOP):
        m = jnp.max(score, axis=-1, keepdims=True)
        idx = jnp.min(jnp.where(score == m, sbf, float(nsp)), axis=-1, keepdims=True)
        hit = sbf == idx
        sel = sel | hit
        score = jnp.where(hit, NEG, score)
    return sel & causal


def _online_step(carry, s, vt):
    m, l, acc = carry
    m_new = jnp.maximum(m, jnp.max(s, axis=-1, keepdims=True))
    alpha = jnp.exp2(m - m_new)
    p = jnp.exp2(s - m_new)
    l = alpha * l + jnp.sum(p, axis=-1, keepdims=True)
    acc = alpha * acc + _dot(p, vt)
    return m_new, l, acc


def _block_onehot(key0, n_keys, nsp):
    shift = SEL_BLOCK.bit_length() - 1
    kb = jnp.right_shift(lax.broadcasted_iota(jnp.int32, (n_keys, 1), 0), shift)
    sb = lax.broadcasted_iota(jnp.int32, (1, nsp), 1) - jnp.right_shift(key0, shift)
    return (kb == sb).astype(BF16)


def _window_branch(qg, kw, vw, kpos, tpos4):
    s = _dot_nt(qg, kw)
    mask = (kpos <= tpos4) & (kpos >= tpos4 - WINDOW) & (kpos >= 0)
    s = jnp.where(mask, s, NEG)
    m = jnp.max(s, axis=-1, keepdims=True)
    p = jnp.where(mask, jnp.exp2(s - m), 0.0)
    l = jnp.sum(p, axis=-1, keepdims=True)
    return _dot(p, vw) * jnp.where(l > 0.0, 1.0 / l, 0.0)


def _gate_col(gates, g, br, tq):
    return jnp.concatenate(
        [gates[:, 3 * (NSA_GROUP * g + j) + br:3 * (NSA_GROUP * g + j) + br + 1] for j in range(NSA_GROUP)], axis=0)


def _finish(o_ref, oacc_ref, nz, og):
    o = oacc_ref[...]
    o = o * lax.rsqrt(jnp.mean(o * o, axis=-1, keepdims=True) + NORM_EPS) * og
    o_ref[...] = (o * _silu(nz)).astype(o_ref.dtype)


def _nsa_prompt_kernel(q_ref, gate_ref, nz_ref, kc_ref, vc_ref, selk_ref, selvt_ref, *rest, n_sub, nsp):
    nwin = WINDOW // Q_BLOCK + 1
    wink_refs = rest[:nwin]
    winvt_refs = rest[nwin:2 * nwin]
    og_ref, o_ref, oacc_ref = rest[2 * nwin:]
    i = pl.program_id(0)
    tq = Q_BLOCK
    lanes = NSA_GROUP * tq
    t0 = i * tq
    tpos = t0 + lax.broadcasted_iota(jnp.int32, (1, tq), 1)
    tpos4 = jnp.concatenate([tpos] * NSA_GROUP, axis=1)
    q = q_ref[...]
    gates_t = gate_ref[...].T

    def gate_row(g, br):
        rows = [3 * (NSA_GROUP * g + j) + br for j in range(NSA_GROUP)]
        return jnp.concatenate([gates_t[r:r + 1, :] for r in rows], axis=1)

    npr = lax.broadcasted_iota(jnp.int32, (n_sub, 1), 0)
    cvalid = (npr >= 1) & (npr * CMP_STRIDE + (CMP_STRIDE - 1) <= tpos4)
    sb_o = lax.broadcasted_iota(jnp.int32, (nsp, n_sub), 0)
    np_o = lax.broadcasted_iota(jnp.int32, (nsp, n_sub), 1)
    r = SEL_BLOCK // CMP_STRIDE
    ovt = ((np_o >= r * sb_o) & (np_o <= r * sb_o + r) & (np_o >= 1)).astype(BF16)

    groups = range(NSA_KV_HEADS)
    qts = [jnp.concatenate([q[(NSA_GROUP * g + j) * HEAD_DIM:(NSA_GROUP * g + j + 1) * HEAD_DIM, :]
                            for j in range(NSA_GROUP)], axis=1) for g in groups]

    cs = [jnp.dot(kc_ref[0, g], qts[g], preferred_element_type=F32) for g in groups]
    cps = [p * inv for p, inv in (_softmax_cols(s, cvalid) for s in cs)]
    octs = [_dot_tn(vc_ref[0, g], cps[g]) for g in groups]
    imps = []
    for g in groups:
        psum = cps[g][:, 0:tq]
        for j in range(1, NSA_GROUP):
            psum = psum + cps[g][:, j * tq:(j + 1) * tq]
        imps.append(sum(jnp.dot(ovt, part, preferred_element_type=F32) for part in _split3(psum)))
    sel = _select_t(jnp.concatenate(imps, axis=1), tpos4, nsp)
    bias_t = jnp.where(sel, 0.0, NEG).astype(BF16)

    kpos = t0 - WINDOW + lax.broadcasted_iota(jnp.int32, (WINDOW + tq, 1), 0)
    wmask = (kpos <= tpos4) & (kpos >= tpos4 - WINDOW) & (kpos >= 0)
    ws = [jnp.dot(jnp.concatenate([w[:, g * HEAD_DIM:(g + 1) * HEAD_DIM] for w in wink_refs], axis=0), qts[g],
                  preferred_element_type=F32) for g in groups]
    wsb = [jnp.where(wmask, s, NEG).astype(BF16) for s in ws]
    wps = [jnp.exp2(s - jnp.max(s, axis=0, keepdims=True)) for s in wsb]
    wones = jnp.ones((16, WINDOW + tq), BF16)
    owts = []
    for g in groups:
        vwt = jnp.concatenate([w[g * HEAD_DIM:(g + 1) * HEAD_DIM, :] for w in winvt_refs], axis=1)
        pv = jnp.dot(jnp.concatenate([vwt, wones], axis=0), wps[g], preferred_element_type=F32)
        owts.append(pv[:HEAD_DIM] * (1.0 / pv[HEAD_DIM:HEAD_DIM + 1]))
    n_full = t0 // KV_TILE
    krow = lax.broadcasted_iota(jnp.int32, (KV_TILE, 1), 0)
    qaugs = []
    for g in range(NSA_KV_HEADS):
        bg = bias_t[:, g * tq:(g + 1) * tq]
        qaugs.append(jnp.concatenate([qts[g], jnp.concatenate([bg] * NSA_GROUP, axis=1)], axis=0))

    def tile_step(c, carries, diagonal):
        k0 = pl.multiple_of(c * KV_TILE, KV_TILE)
        onehot = _block_onehot(k0, KV_TILE, nsp)
        scores = []
        for g in range(NSA_KV_HEADS):
            kt = selk_ref[pl.ds(k0, KV_TILE), pl.ds(g * HEAD_DIM, HEAD_DIM)]
            s = jnp.dot(jnp.concatenate([kt, onehot], axis=1), qaugs[g], preferred_element_type=F32)
            if diagonal:
                s = jnp.where(k0 + krow <= tpos4, s, NEG)
            scores.append(s.astype(BF16))
        stats = []
        for g in range(NSA_KV_HEADS):
            m = carries[g][0]
            m_new = jnp.maximum(m, jnp.max(scores[g], axis=0, keepdims=True).astype(F32))
            stats.append((m_new, jnp.exp2(m - m_new), jnp.exp2(scores[g] - m_new.astype(BF16))))
        out = []
        ones = jnp.ones((16, KV_TILE), BF16)
        for g in range(NSA_KV_HEADS):
            m_new, alpha, p = stats[g]
            vtt = selvt_ref[pl.ds(g * HEAD_DIM, HEAD_DIM), pl.ds(k0, KV_TILE)]
            pv = jnp.dot(jnp.concatenate([vtt, ones], axis=0), p, preferred_element_type=F32)
            out.append((m_new, alpha * carries[g][1] + pv[HEAD_DIM:HEAD_DIM + 1], alpha * carries[g][2] + pv[:HEAD_DIM]))
        return tuple(out)

    init = tuple((jnp.full((1, lanes), NEG, F32), jnp.zeros((1, lanes), F32), jnp.zeros((HEAD_DIM, lanes), F32))
                 for _ in range(NSA_KV_HEADS))
    carries = lax.fori_loop(0, n_full, lambda c, cr: tile_step(c, cr, False), init)
    carries = tile_step(n_full, carries, True)

    for g in groups:
        m, l, acc = carries[g]
        o_s = acc * (1.0 / l)
        o_g = gate_row(g, 0) * octs[g] + gate_row(g, 1) * o_s + gate_row(g, 2) * owts[g]
        for j in range(NSA_GROUP):
            hd = NSA_GROUP * g + j
            oacc_ref[:, hd * HEAD_DIM:(hd + 1) * HEAD_DIM] = o_g[:, j * tq:(j + 1) * tq].T

    _finish(o_ref, oacc_ref, nz_ref[...], og_ref[...])


def _nsa_prompt(qn, gates, proj, kc, vc, selk, selvt, wink, winvt, nsa_g, t_len):
    n_sub = kc.shape[2]
    n_sel = t_len // SEL_BLOCK
    nsp = -(-n_sel // 128) * 128
    nq = t_len // Q_BLOCK
    nwin = WINDOW // Q_BLOCK + 1
    wblk = lambda r: (lambda i: jnp.maximum(i - (nwin - 1) + r, 0))
    win_specs = ([pl.BlockSpec((Q_BLOCK, KV_WIDTH), (lambda f: (lambda i: (f(i), 0)))(wblk(r))) for r in range(nwin)]
                 + [pl.BlockSpec((KV_WIDTH, Q_BLOCK), (lambda f: (lambda i: (0, f(i))))(wblk(r))) for r in range(nwin)])
    return pl.pallas_call(
        functools.partial(_nsa_prompt_kernel, n_sub=n_sub, nsp=nsp),
        grid=(nq,),
        in_specs=[pl.BlockSpec((NSA_WIDTH, Q_BLOCK), lambda i: (0, i)),
                  pl.BlockSpec((Q_BLOCK, 128), lambda i: (i, 0)),
                  pl.BlockSpec((Q_BLOCK, NSA_WIDTH), lambda i: (i, COL_NZ // 2048)),
                  pl.BlockSpec((1, NSA_KV_HEADS, n_sub, HEAD_DIM), lambda i: (0, 0, 0, 0)),
                  pl.BlockSpec((1, NSA_KV_HEADS, n_sub, HEAD_DIM), lambda i: (0, 0, 0, 0)),
                  pl.BlockSpec((t_len, KV_WIDTH), lambda i: (0, 0), pipeline_mode=pl.Buffered(1)),
                  pl.BlockSpec((KV_WIDTH, t_len), lambda i: (0, 0), pipeline_mode=pl.Buffered(1))]
        + win_specs + [pl.BlockSpec((1, NSA_WIDTH), lambda i: (0, 0))],
        out_specs=pl.BlockSpec((Q_BLOCK, NSA_WIDTH), lambda i: (i, 0)),
        out_shape=jax.ShapeDtypeStruct((t_len, NSA_WIDTH), BF16),
        scratch_shapes=[pltpu.VMEM((Q_BLOCK, NSA_WIDTH), F32)],
        compiler_params=_cparams(("arbitrary",)),
        name="nsa_prompt",
    )(qn, gates, proj, kc, vc, selk, selvt, *([wink] * nwin), *([winvt] * nwin), nsa_g.reshape(1, NSA_WIDTH))


def _sample_select_kernel(q_ref, kc_ref, vc_ref, bias_ref, oc_ref, *, n_sub, nsp, past_len, t_new, bs):
    tq = t_new
    G = NSA_KV_HEADS
    tpos = past_len + lax.broadcasted_iota(jnp.int32, (tq, 1), 0)
    tpos4 = jnp.concatenate([tpos] * NSA_GROUP, axis=0)
    q = q_ref[...]
    psums = []
    for b in range(bs):
        for g in range(G):
            qg = jnp.concatenate([q[b * tq:(b + 1) * tq, (NSA_GROUP * g + j) * HEAD_DIM:(NSA_GROUP * g + j + 1) * HEAD_DIM]
                                  for j in range(NSA_GROUP)], axis=0)
            oc, p = _cmp_branch(qg.astype(BF16), kc_ref[b, g], vc_ref[b, g], tpos4, n_sub)
            oc_ref[b, g] = oc
            psum = p[0:tq]
            for j in range(1, NSA_GROUP):
                psum = psum + p[j * tq:(j + 1) * tq]
            psums.append(psum)
    imp = _importance(jnp.concatenate(psums, axis=0), n_sub, nsp)
    sel = _select(imp, jnp.concatenate([tpos] * (bs * G), axis=0), nsp)
    bias = jnp.where(sel, 0.0, NEG)
    lanes = G * NSA_GROUP * tq
    for b in range(bs):
        bb = jnp.concatenate([bias[(b * G + g) * tq:(b * G + g + 1) * tq]
                              for g in range(G) for _ in range(NSA_GROUP)], axis=0)
        for c in range(nsp // lanes):
            bias_ref[b, c * lanes:(c + 1) * lanes, :] = bb[:, c * lanes:(c + 1) * lanes].T


def _sample_select(qn, kc, vc, past_len, t_new):
    nb = kc.shape[0]
    n_sub = kc.shape[2]
    n_sel = past_len // SEL_BLOCK + 1
    nsp = -(-n_sel // 128) * 128
    bs = 8 if nb % 8 == 0 else nb
    rows = NSA_GROUP * t_new
    lanes = NSA_KV_HEADS * rows
    assert lanes == 128 and nsp % lanes == 0
    kv_spec = pl.BlockSpec((bs, NSA_KV_HEADS, n_sub, HEAD_DIM), lambda i: (i, 0, 0, 0))
    return pl.pallas_call(
        functools.partial(_sample_select_kernel, n_sub=n_sub, nsp=nsp, past_len=past_len, t_new=t_new, bs=bs),
        grid=(nb // bs,),
        in_specs=[pl.BlockSpec((bs * t_new, NSA_WIDTH), lambda i: (i, 0)), kv_spec, kv_spec],
        out_specs=[pl.BlockSpec((bs, nsp, lanes), lambda i: (i, 0, 0)),
                   pl.BlockSpec((bs, NSA_KV_HEADS, rows, HEAD_DIM), lambda i: (i, 0, 0, 0))],
        out_shape=[jax.ShapeDtypeStruct((nb, nsp, lanes), F32),
                   jax.ShapeDtypeStruct((nb, NSA_KV_HEADS, rows, HEAD_DIM), F32)],
        compiler_params=_cparams(("arbitrary",)),
        name="sample_select",
    )(qn, kc, vc)


def _nsa_sample_kernel(pt_ref, *refs, nsp, past_len, t_new):
    pages = refs[:PAGES_PER_TILE]
    (q_ref, gate_ref, nz_ref, bias_ref, oc_ref, newsel_ref, swin_ref, newwin_ref, og_ref,
     o_ref, qbd_ref, m_ref, l_ref, acc_ref, oacc_ref) = refs[PAGES_PER_TILE:]
    tau = pl.program_id(1)
    n_tau = pl.num_programs(1)
    tq = t_new
    G = NSA_KV_HEADS
    rows = NSA_GROUP * tq
    lanes = G * rows
    tpos = past_len + lax.broadcasted_iota(jnp.int32, (tq, 1), 0)
    tpos4 = jnp.concatenate([tpos] * NSA_GROUP, axis=0)
    lane_i = lax.broadcasted_iota(jnp.int32, (1, lanes), 1)

    def group_q():
        q = q_ref[...]
        return [jnp.concatenate([q[:, (NSA_GROUP * g + j) * HEAD_DIM:(NSA_GROUP * g + j + 1) * HEAD_DIM]
                                 for j in range(NSA_GROUP)], axis=0) for g in range(G)]

    def to_col(row_vec):
        eye = lax.broadcasted_iota(jnp.int32, (lanes, lanes), 0) == lax.broadcasted_iota(jnp.int32, (lanes, lanes), 1)
        return jnp.sum(jnp.where(eye, jnp.broadcast_to(row_vec, (lanes, lanes)), 0.0), axis=1, keepdims=True)

    def online(s, v_all, m, l, acc):
        m_new = jnp.maximum(m, jnp.max(s, axis=0, keepdims=True))
        alpha = jnp.exp2(m - m_new)
        p = jnp.exp2(s - m_new)
        l_new = alpha * l + jnp.sum(p, axis=0, keepdims=True)
        pv = _dot_tn(p, v_all)
        upd = jnp.concatenate([pv[g * rows:(g + 1) * rows, g * HEAD_DIM:(g + 1) * HEAD_DIM] for g in range(G)], axis=0)
        return m_new, l_new, to_col(alpha) * acc + upd

    @pl.when(tau == 0)
    def _():
        q_t = jnp.concatenate(group_q(), axis=0).T
        qbd_ref[...] = jnp.concatenate(
            [jnp.where(lane_i // rows == g, q_t, 0.0) for g in range(G)], axis=0).astype(BF16)
        m_ref[...] = jnp.full((1, lanes), NEG, F32)
        l_ref[...] = jnp.zeros((1, lanes), F32)
        acc_ref[...] = jnp.zeros((lanes, HEAD_DIM), F32)

    planes = [jnp.swapaxes(p[0], 0, 1) for p in pages]
    k_all = jnp.concatenate([jnp.concatenate([pl_[g] for g in range(G)], axis=1) for pl_ in planes], axis=0)
    v_all = jnp.concatenate([jnp.concatenate([pl_[G + g] for g in range(G)], axis=1) for pl_ in planes], axis=0)
    n_keys = PAGES_PER_TILE * PAGE
    n_blk = n_keys // SEL_BLOCK
    b_rows = bias_ref[0, pl.ds(pl.multiple_of(tau * n_blk, n_blk), n_blk), :]
    bias_keys = jnp.broadcast_to(b_rows[:, None, :], (n_blk, SEL_BLOCK, lanes)).reshape(n_keys, lanes)
    s = jnp.dot(k_all.astype(BF16), qbd_ref[...], preferred_element_type=F32) + bias_keys
    m, l, acc = online(s, v_all, m_ref[...], l_ref[...], acc_ref[...])
    m_ref[...] = m
    l_ref[...] = l
    acc_ref[...] = acc

    @pl.when(tau == n_tau - 1)
    def _():
        qgs = group_q()
        gates = gate_ref[...]
        n_buf = swin_ref.shape[1]
        kpos = past_len - n_buf + lax.broadcasted_iota(jnp.int32, (1, n_buf + tq), 1)
        last_blk = past_len // SEL_BLOCK
        pad = 16
        new = newsel_ref[...]
        zpad = jnp.zeros((pad - tq, G * HEAD_DIM), F32)
        kn = jnp.concatenate([jnp.concatenate([new[:, g, :] for g in range(G)], axis=1), zpad], axis=0)
        vn = jnp.concatenate([jnp.concatenate([new[:, G + g, :] for g in range(G)], axis=1), zpad], axis=0)
        krow = lax.broadcasted_iota(jnp.int32, (pad, 1), 0)
        s_new = jnp.dot(kn.astype(BF16), qbd_ref[...], preferred_element_type=F32) + bias_ref[0, last_blk:last_blk + 1, :]
        s_new = jnp.where((krow <= lane_i % tq) & (krow < tq), s_new, NEG)
        m, l, acc = online(s_new, vn, m_ref[...], l_ref[...], acc_ref[...])
        o_sel = acc * to_col(1.0 / l)
        swin = jnp.swapaxes(swin_ref[0], 0, 1)
        nwin = newwin_ref[...]
        for g in range(G):
            kw = jnp.concatenate([swin[g], nwin[:, g, :]], axis=0)
            vw = jnp.concatenate([swin[G + g], nwin[:, G + g, :]], axis=0)
            o_w = _window_branch(qgs[g], kw, vw, kpos, tpos4)
            o_g = (_gate_col(gates, g, 0, tq) * oc_ref[0, g] + _gate_col(gates, g, 1, tq) * o_sel[g * rows:(g + 1) * rows]
                   + _gate_col(gates, g, 2, tq) * o_w)
            for j in range(NSA_GROUP):
                hd = NSA_GROUP * g + j
                oacc_ref[:, hd * HEAD_DIM:(hd + 1) * HEAD_DIM] = o_g[j * tq:(j + 1) * tq]
        _finish(o_ref, oacc_ref, nz_ref[...], og_ref[...])


def _nsa_sample(cache, page_table, qn, gates, proj, bias_t, oc, rows_s, state_win, win_s, nsa_g, row0, t_new):
    nb, n_pages = page_table.shape
    past_len = n_pages * PAGE
    nsp = bias_t.shape[1]
    n_tau = n_pages // PAGES_PER_TILE
    rb = row0 // t_new
    n_buf = state_win.shape[1]
    rows = NSA_GROUP * t_new
    lanes = NSA_KV_HEADS * rows
    assert lanes == 128 and nsp % lanes == 0, "the sample kernel packs all query rows of a batch into one lane tile"
    cg = 2 * NSA_KV_HEADS

    def page_spec(r):
        return pl.BlockSpec((1, PAGE, cg, HEAD_DIM),
                            lambda b, i, pt: (pt[b, i * PAGES_PER_TILE + r], 0, 1, 0))

    grid_spec = pltpu.PrefetchScalarGridSpec(
        num_scalar_prefetch=1,
        grid=(nb, n_tau),
        in_specs=[page_spec(r) for r in range(PAGES_PER_TILE)] + [
            pl.BlockSpec((t_new, NSA_WIDTH), lambda b, i, pt: (b, 0)),
            pl.BlockSpec((t_new, 128), lambda b, i, pt: (b, 0)),
            pl.BlockSpec((t_new, NSA_WIDTH), lambda b, i, pt: (rb + b, COL_NZ // 2048)),
            pl.BlockSpec((1, nsp, lanes), lambda b, i, pt: (b, 0, 0)),
            pl.BlockSpec((1, NSA_KV_HEADS, rows, HEAD_DIM), lambda b, i, pt: (b, 0, 0, 0)),
            pl.BlockSpec((t_new, cg, HEAD_DIM), lambda b, i, pt: (b, 1, 0)),
            pl.BlockSpec((1, n_buf, cg, HEAD_DIM), lambda b, i, pt: (b, 0, 0, 0)),
            pl.BlockSpec((t_new, cg, HEAD_DIM), lambda b, i, pt: (b, 0, 0)),
            pl.BlockSpec((1, NSA_WIDTH), lambda b, i, pt: (0, 0))],
        out_specs=pl.BlockSpec((t_new, NSA_WIDTH), lambda b, i, pt: (b, 0)),
        scratch_shapes=[pltpu.VMEM((NSA_KV_HEADS * HEAD_DIM, lanes), BF16),
                        pltpu.VMEM((1, lanes), F32),
                        pltpu.VMEM((1, lanes), F32),
                        pltpu.VMEM((lanes, HEAD_DIM), F32),
                        pltpu.VMEM((t_new, NSA_WIDTH), F32)],
    )
    return pl.pallas_call(
        functools.partial(_nsa_sample_kernel, nsp=nsp, past_len=past_len, t_new=t_new),
        grid_spec=grid_spec,
        out_shape=jax.ShapeDtypeStruct((nb * t_new, NSA_WIDTH), F32),
        compiler_params=_cparams(("arbitrary", "arbitrary")),
        name="nsa_sample",
    )(page_table, *([cache] * PAGES_PER_TILE), qn, gates, proj, bias_t, oc, rows_s, state_win, win_s,
      nsa_g.reshape(1, NSA_WIDTH))


def _out_kernel(oh_ref, on_ref, wh_ref, wn_ref, x_ref, gate_ref, y_ref):
    acc = _dot(oh_ref[...], wh_ref[...]) + _dot(on_ref[...], wn_ref[...])
    y_ref[...] = x_ref[...] + gate_ref[...] * acc


def _out_proj(o_h, o_n, w_out, x, gate, tm, tn=512):
    m, d = x.shape
    half = o_h.shape[1]
    grow = gate.shape[0]
    gmap = (lambda i, j: (0, j)) if grow == 1 else (lambda i, j: (i, j))
    return pl.pallas_call(
        _out_kernel,
        grid=(m // tm, d // tn),
        in_specs=[pl.BlockSpec((tm, half), lambda i, j: (i, 0)),
                  pl.BlockSpec((tm, half), lambda i, j: (i, 0)),
                  pl.BlockSpec((half, tn), lambda i, j: (0, j)),
                  pl.BlockSpec((half, tn), lambda i, j: (1, j)),
                  pl.BlockSpec((tm, tn), lambda i, j: (i, j)),
                  pl.BlockSpec((1 if grow == 1 else tm, tn), gmap)],
        out_specs=pl.BlockSpec((tm, tn), lambda i, j: (i, j)),
        out_shape=jax.ShapeDtypeStruct((m, d), F32),
        compiler_params=_cparams(("arbitrary", "arbitrary")),
        name="out_proj",
    )(o_h, o_n, w_out, w_out, x, gate)


def _layer(xp, xs, c_all, cache, s_win, s_hgrn, page_table, lb_param, layer, norm_g, w_ada, b_ada, w_in,
           hgrn_out_g, q_g, k_g, pe, w1, w2, nsa_out_g, w_out):
    t_len = xp.shape[0]
    nb, t_new, _ = xs.shape
    n_s = nb * t_new
    d = D_MODEL

    mod = _ada(c_all, w_ada, b_ada)
    shift, scale, gate = mod[:, :d], mod[:, d:2 * d], mod[:, 2 * d:]
    rep = lambda a: jnp.pad(jnp.repeat(a[1:1 + nb], t_new, axis=0), ((0, SAMPLE_ROWS - n_s), (0, 0)))
    xs_pad = jnp.pad(xs.reshape(n_s, d), ((0, SAMPLE_ROWS - n_s), (0, 0)))
    h_all = _norm(xp, xs_pad, norm_g.reshape(1, d), scale[0:1], shift[0:1], rep(scale), rep(shift))

    proj = _matmul(h_all, jnp.swapaxes(w_in, 0, 1))

    oh_p, st_p = _hgrn(proj, lb_param, hgrn_out_g, None, row0=0, n_batch=1, t_len=t_len,
                       chunk=min(64, t_len), out_dtype=BF16, layer=layer)
    oh_s, st_s = _hgrn(proj, lb_param, hgrn_out_g, s_hgrn, row0=t_len, n_batch=nb, t_len=t_new,
                       chunk=t_new, out_dtype=F32, layer=layer)

    qn_p, rows_p, win_p3, selk_p, selvt_p, wink_p, winvt_p, gates_p = _prep(proj, q_g, k_g, 0, t_len, True)
    qn_s, rows_s, win_s3, _, _, _, _, gates_s = _prep(proj, q_g, k_g, t_len, SAMPLE_ROWS, False)

    w1r = w1.reshape(2, 2, CMP_STRIDE * HEAD_DIM, CMP_HIDDEN)
    wc = jnp.concatenate([w1r[:, 0], w1r[:, 1]], axis=-1).astype(BF16)
    pe2 = pe.reshape(2, 2, 1, CMP_STRIDE * HEAD_DIM)
    w2b = w2.astype(BF16)

    n_pp = t_len // PAGE
    kc_p, vc_p = _compress(rows_p.reshape(n_pp, PAGE, 4 * NSA_KV_HEADS, HEAD_DIM),
                           jnp.arange(n_pp, dtype=jnp.int32).reshape(1, n_pp), wc, pe2, w2b, k_g[0])
    cache4 = cache.reshape(cache.shape[0], PAGE, 4 * NSA_KV_HEADS, HEAD_DIM)
    kc_s, vc_s = _compress(cache4, page_table, wc, pe2, w2b, k_g[0])

    on_p = _nsa_prompt(qn_p, gates_p, proj, kc_p, vc_p, selk_p, selvt_p, wink_p, winvt_p, nsa_out_g, t_len)
    n_buf = s_win.shape[1]
    s_win4 = s_win.reshape(nb, n_buf, 2 * NSA_KV_HEADS, HEAD_DIM)
    bias_s, oc_s = _sample_select(qn_s, kc_s, vc_s, page_table.shape[1] * PAGE, t_new)
    on_s = _nsa_sample(cache4, page_table, qn_s, gates_s, proj, bias_s, oc_s, rows_s, s_win4, win_s3,
                       nsa_out_g, t_len, t_new)

    y_p = _out_proj(oh_p, on_p, w_out, xp, gate[0:1], tm=min(1024, t_len))
    y_s = _out_proj(oh_s, on_s, w_out, xs.reshape(n_s, d), jnp.repeat(gate[1:1 + nb], t_new, axis=0), tm=n_s)

    n_w = min(WINDOW, t_len)
    kv_p = rows_p.reshape(1, t_len, 4, NSA_KV_HEADS, HEAD_DIM)
    kv_s = rows_s[:n_s].reshape(nb, t_new, 4, NSA_KV_HEADS, HEAD_DIM)
    win_p = win_p3[t_len - n_w:].reshape(1, n_w, 2, NSA_KV_HEADS, HEAD_DIM)
    win_new = win_s3[:n_s].reshape(nb, t_new, 2, NSA_KV_HEADS, HEAD_DIM)
    win_s = jnp.concatenate([s_win, win_new], axis=1)[:, -n_buf:]
    return y_p, y_s.reshape(nb, t_new, d), kv_p, kv_s, win_p, win_s, st_p, st_s


def kernel(x_prompt, x_sample, cache_kv, state_win, state_hgrn, page_table, c_prompt, c_sample, norm_g, w_ada,
           b_ada, w_in, hgrn_lb, hgrn_out_g, q_norm_g, k_norm_g, cmp_pe, cmp_w1, cmp_w2, nsa_out_g, w_out):
    depth = w_in.shape[0]
    assert depth == 1 and x_prompt.shape[0] == 1
    nb = x_sample.shape[0]
    c_all = jnp.concatenate([c_prompt, c_sample], axis=0)
    c_all = jnp.pad(c_all, ((0, (-c_all.shape[0]) % 8), (0, 0)))
    l = 0
    outs = _layer(x_prompt[0], x_sample, c_all, cache_kv[l], state_win[l], state_hgrn[l], page_table,
                  hgrn_lb, l, norm_g[l], w_ada[l], b_ada[l], w_in[l], hgrn_out_g[l], q_norm_g[l], k_norm_g[l],
                  cmp_pe[l], cmp_w1[l], cmp_w2[l], nsa_out_g[l], w_out[l])
    y_p, y_s, kv_p, kv_s, win_p, win_s, st_p, st_s = outs
    return (y_p[None], y_s, kv_p[None], kv_s[None], win_p[None], win_s[None], st_p[None], st_s[None])
```
